```python
import math
import jax, jax.numpy as jnp
from jax import lax
import numpy as np

D_MODEL = 1024
BATCH = 16
SEQ = 256
DEPTH = 4
DEC_BATCH = 4
DEC_SEQ = 1024
PAST_LEN = 256

GRID_W = 64
N_MIXERS = 3
N_GMLP_LAYERS = (DEPTH + 2) // 3
N_ATTN_LAYERS = (DEPTH + 1) // 3
N_SSM_LAYERS = DEPTH // 3
N_MOD = 9
D_FF = 2816
EPS = 1e-6
GMLP_HALF = 3 * D_MODEL
GMLP_GROUPS = 8
GMLP_GROUP_DIM = GMLP_HALF // GMLP_GROUPS
CHUNK = 128
HEAD_DIM = 64
N_Q_HEADS = D_MODEL // HEAD_DIM
N_KV_HEADS = 4
Q_PER_KV = N_Q_HEADS // N_KV_HEADS
Q_DIM = N_Q_HEADS * HEAD_DIM
KV_DIM = N_KV_HEADS * HEAD_DIM
WINDOW = 128
ATTN_BLOCK = 128
ATTN_SCALE = HEAD_DIM ** -0.5
ROPE_BASE = 10000.0
ROT_PAIRS = HEAD_DIM // 4
NEG_INF = -1e30
SSM_INNER = 2 * D_MODEL
SSM_HEAD_DIM = 64
SSM_HEADS = SSM_INNER // SSM_HEAD_DIM
SSM_GROUPS = 4
SSM_STATE = 128
SSM_CONV = 3
SSM_CHUNK = 128
SSM_GN = SSM_GROUPS * SSM_STATE
SSM_CONV_DIM = SSM_INNER + 2 * SSM_GN
SSM_IN_DIM = SSM_INNER + SSM_CONV_DIM + 2 * SSM_HEADS

kernel_name = 'hybrid_diffusion_prefix_trunk_step'


def rms_norm(x, g):
    xf = x.astype(jnp.float32)
    y = xf * lax.rsqrt(jnp.mean(xf * xf, axis=-1, keepdims=True) + EPS)
    return (y * g.astype(jnp.float32)).astype(x.dtype)


def layer_norm(x, g, b):
    xf = x.astype(jnp.float32)
    mu = jnp.mean(xf, axis=-1, keepdims=True)
    xc = xf - mu
    var = jnp.mean(xc * xc, axis=-1, keepdims=True)
    return (xc * lax.rsqrt(var + EPS) * g.astype(jnp.float32) + b.astype(jnp.float32)).astype(x.dtype)


def modulation(cond, w, b):
    return (jax.nn.silu(cond) @ w + b).reshape(cond.shape[0], N_MOD, D_MODEL)


def adaln(x, g, shift, scale):
    return rms_norm(x, g) * (1 + scale[:, None]) + shift[:, None]


def swiglu(h, w_in, w_out):
    gu = h @ w_in
    return (jax.nn.silu(gu[..., :D_FF]) * gu[..., D_FF:]) @ w_out


def chunk_mlp(h, w_in, ln_g, ln_b, w_s, b_s, w_out):
    b, n, _ = h.shape
    uv = jax.nn.gelu(h @ w_in, approximate=False)
    u, v = uv[..., :GMLP_HALF], uv[..., GMLP_HALF:]
    v = layer_norm(v, ln_g, ln_b).reshape(b, n // CHUNK, CHUNK, GMLP_GROUPS, GMLP_GROUP_DIM)
    v = jnp.einsum('gij,bcjgd->bcigd', w_s, v) + b_s.T[:, :, None]
    return (u * v.reshape(b, n, GMLP_HALF)) @ w_out


def axial_rope(t):
    n = t.shape[1]
    rows = n // GRID_W
    pos_r = jnp.repeat(jnp.arange(rows, dtype=jnp.float32), GRID_W)
    pos_c = (jnp.arange(n) % GRID_W).astype(jnp.float32)
    inv = ROPE_BASE ** (-jnp.arange(ROT_PAIRS, dtype=jnp.float32) / ROT_PAIRS)
    bshape = (1, n) + (1,) * (t.ndim - 3) + (ROT_PAIRS,)
    tf = t.astype(jnp.float32)

    def rot(x, pos):
        ang = (pos[:, None] * inv).reshape(bshape)
        cos, sin = jnp.cos(ang), jnp.sin(ang)
        x1, x2 = x[..., :ROT_PAIRS], x[..., ROT_PAIRS:]
        return jnp.concatenate([x1 * cos - x2 * sin, x2 * cos + x1 * sin], axis=-1)

    half = HEAD_DIM // 2
    return jnp.concatenate([rot(tf[..., :half], pos_r), rot(tf[..., half:], pos_c)], axis=-1).astype(t.dtype)


def project_qkv(h, w_qkv):
    b, n, _ = h.shape
    qkv = h @ w_qkv
    q = qkv[..., :Q_DIM].reshape(b, n, N_KV_HEADS, Q_PER_KV, HEAD_DIM)
    k = qkv[..., Q_DIM:Q_DIM + KV_DIM].reshape(b, n, N_KV_HEADS, HEAD_DIM)
    v = qkv[..., Q_DIM + KV_DIM:].reshape(b, n, N_KV_HEADS, HEAD_DIM)
    return q, k, v


def sink_probs(logits, sinks):
    sink = jnp.broadcast_to(sinks.astype(jnp.float32)[None, :, :, None, None], logits.shape[:-1] + (1,))
    return jax.nn.softmax(jnp.concatenate([logits, sink], axis=-1), axis=-1)[..., :-1]


def attn_context(h, w_qkv, sinks, w_out):
    b, s, _ = h.shape
    q, k, v = project_qkv(h, w_qkv)
    nb = s // ATTN_BLOCK
    qb = jnp.moveaxis(q.reshape(b, nb, ATTN_BLOCK, N_KV_HEADS, Q_PER_KV, HEAD_DIM), 1, 0)
    sk = sinks.reshape(N_KV_HEADS, Q_PER_KV)

    def block(qi):
        logits = jnp.einsum('bqhgd,bkhd->bhgqk', qi, k).astype(jnp.float32) * ATTN_SCALE
        p = sink_probs(logits, sk).astype(v.dtype)
        return jnp.einsum('bhgqk,bkhd->bqhgd', p, v)

    o = jnp.moveaxis(lax.map(block, qb), 0, 1).reshape(b, s, Q_DIM)
    return o @ w_out, k, v


def attn_latent(h, w_qkv, sinks, w_out, ck, cv):
    b, n, _ = h.shape
    q, k, v = project_qkv(h, w_qkv)
    q, k = axial_rope(q), axial_rope(k)
    nb = n // ATTN_BLOCK
    pad = ((0, 0), (ATTN_BLOCK, ATTN_BLOCK), (0, 0), (0, 0))
    kp, vp = jnp.pad(k, pad), jnp.pad(v, pad)
    qb = jnp.moveaxis(q.reshape(b, nb, ATTN_BLOCK, N_KV_HEADS, Q_PER_KV, HEAD_DIM), 1, 0)
    sk = sinks.reshape(N_KV_HEADS, Q_PER_KV)
    span = 3 * ATTN_BLOCK

    def block(args):
        i, qi = args
        start = i * ATTN_BLOCK
        kb = lax.dynamic_slice_in_dim(kp, start, span, axis=1)
        vb = lax.dynamic_slice_in_dim(vp, start, span, axis=1)
        qpos = start + jnp.arange(ATTN_BLOCK)
        kpos = start - ATTN_BLOCK + jnp.arange(span)
        valid = (kpos >= 0) & (kpos < n) & (jnp.abs(qpos[:, None] - kpos[None, :]) <= WINDOW)
        s_loc = jnp.einsum('bqhgd,bkhd->bhgqk', qi, kb).astype(jnp.float32) * ATTN_SCALE
        s_loc = jnp.where(valid, s_loc, NEG_INF)
        s_ctx = jnp.einsum('bqhgd,bkhd->bhgqk', qi, ck).astype(jnp.float32) * ATTN_SCALE
        p = sink_probs(jnp.concatenate([s_loc, s_ctx], axis=-1), sk).astype(vb.dtype)
        return (jnp.einsum('bhgqk,bkhd->bqhgd', p[..., :span], vb)
                + jnp.einsum('bhgqk,bkhd->bqhgd', p[..., span:], cv))

    o = lax.map(block, (jnp.arange(nb), qb))
    o = jnp.moveaxis(o, 0, 1).reshape(b, n, Q_DIM)
    return o @ w_out


def dwconv(x, w, b):
    ch = x.shape[-1]
    padw = SSM_CONV // 2
    y = lax.conv_general_dilated(x, w[:, None, :], window_strides=(1,), padding=[(padw, padw)],
                                 dimension_numbers=('NWC', 'WIO', 'NWC'), feature_group_count=ch)
    return y + b


def ssd_scan(x, dt, a, bm, cm, h0):
    b, n, h, p = x.shape
    g, s = bm.shape[2], bm.shape[3]
    hg, L = h // g, SSM_CHUNK
    c = n // L
    f32 = jnp.float32
    x = x.astype(f32).reshape(b, c, L, g, hg, p)
    dt = dt.reshape(b, c, L, g, hg)
    bm = bm.astype(f32).reshape(b, c, L, g, s)
    cm = cm.astype(f32).reshape(b, c, L, g, s)
    acum = jnp.cumsum(dt * a.reshape(g, hg), axis=2)
    xdt = x * dt[..., None]
    tril = jnp.tril(jnp.ones((L, L), dtype=bool))[:, :, None, None]
    seg = acum[:, :, :, None] - acum[:, :, None, :]
    decay = jnp.exp(jnp.where(tril, seg, -jnp.inf))
    w = jnp.einsum('bcign,bcjgn->bcijg', cm, bm)[..., None] * decay
    y_diag = jnp.einsum('bcijgh,bcjghp->bcighp', w, xdt)
    to_end = jnp.exp(acum[:, :, -1:] - acum)
    states = jnp.einsum('bcjgn,bcjghp->bcghpn', bm, xdt * to_end[..., None])
    chunk_decay = jnp.exp(acum[:, :, -1])

    def step(hs, inp):
        dec, st = inp
        return dec[..., None, None] * hs + st, hs

    h_final, h_prev = lax.scan(step, h0.astype(f32).reshape(b, g, hg, p, s),
                               (jnp.moveaxis(chunk_decay, 1, 0), jnp.moveaxis(states, 1, 0)))
    h_prev = jnp.moveaxis(h_prev, 0, 1)
    y_off = jnp.einsum('bcign,bcghpn->bcighp', cm, h_prev) * jnp.exp(acum)[..., None]
    return (y_diag + y_off).reshape(b, n, h, p), h_final.reshape(b, h, p, s)


def ssm_mixer(h, w_in, conv_w, conv_b, dt_bias, a_log, d_skip, norm_g, w_out, h0):
    b, n, _ = h.shape
    zxbcdt = h @ w_in
    z = zxbcdt[..., :SSM_INNER]
    xbc = jax.nn.silu(dwconv(zxbcdt[..., SSM_INNER:SSM_INNER + SSM_CONV_DIM], conv_w, conv_b))
    dt = zxbcdt[..., SSM_INNER + SSM_CONV_DIM:].reshape(b, n, 2, SSM_HEADS).astype(jnp.float32)
    dt = jax.nn.softplus(dt + dt_bias.astype(jnp.float32))
    a = -jnp.exp(a_log.astype(jnp.float32))
    xs = xbc[..., :SSM_INNER].reshape(b, n, SSM_HEADS, SSM_HEAD_DIM)
    bm = xbc[..., SSM_INNER:SSM_INNER + SSM_GN].reshape(b, n, SSM_GROUPS, SSM_STATE)
    cm = xbc[..., SSM_INNER + SSM_GN:].reshape(b, n, SSM_GROUPS, SSM_STATE)
    y_f, s_f = ssd_scan(xs, dt[:, :, 0], a[0], bm, cm, h0[:, 0])

    def rev(t):
        return jnp.flip(t, axis=1)

    y_b, s_b = ssd_scan(rev(xs), rev(dt[:, :, 1]), a[1], rev(bm), rev(cm), h0[:, 1])
    y = y_f + rev(y_b) + d_skip.astype(jnp.float32)[:, None] * xs.astype(jnp.float32)
    y = y.reshape(b, n, SSM_INNER) * jax.nn.silu(z.astype(jnp.float32))
    y = rms_norm(y, norm_g).astype(h.dtype)
    return y @ w_out, jnp.stack([s_f, s_b], axis=1)


def trunk(x, cond, W, ctx_k, ctx_v, ctx_state):
    latent = ctx_k is not None
    new_k, new_v, new_s = [], [], []
    for i in range(DEPTH):
        mod = modulation(cond, W['w_mod'][i], W['b_mod'][i])
        g = W['norm_g'][i]
        f = swiglu(adaln(x, g[0], mod[:, 0], mod[:, 1]), W['ffn_in'][i, 0], W['ffn_out'][i, 0])
        x = x + 0.5 * mod[:, 2][:, None] * rms_norm(f, g[1])
        h = adaln(x, g[2], mod[:, 3], mod[:, 4])
        kind, j = i % N_MIXERS, i // N_MIXERS
        if kind == 0:
            m = chunk_mlp(h, W['gmlp_in'][j], W['gmlp_ln_g'][j], W['gmlp_ln_b'][j],
                          W['gmlp_ws'][j], W['gmlp_bs'][j], W['gmlp_out'][j])
        elif kind == 1:
            if latent:
                m = attn_latent(h, W['attn_qkv'][j], W['attn_sink'][j], W['attn_out'][j],
                                ctx_k[:, j], ctx_v[:, j])
            else:
                m, k, v = attn_context(h, W['attn_qkv'][j], W['attn_sink'][j], W['attn_out'][j])
                new_k.append(k)
                new_v.append(v)
        else:
            if latent:
                h0 = ctx_state[:, j]
            else:
                h0 = jnp.zeros((x.shape[0], 2, SSM_HEADS, SSM_HEAD_DIM, SSM_STATE), jnp.float32)
            m, s = ssm_mixer(h, W['ssm_in'][j], W['ssm_conv_w'][j], W['ssm_conv_b'][j],
                             W['ssm_dt_bias'][j], W['ssm_a_log'][j], W['ssm_d'][j],
                             W['ssm_norm'][j], W['ssm_out'][j], h0)
            if not latent:
                new_s.append(s)
        x = x + mod[:, 5][:, None] * rms_norm(m, g[3])
        f = swiglu(adaln(x, g[4], mod[:, 6], mod[:, 7]), W['ffn_in'][i, 1], W['ffn_out'][i, 1])
        x = x + 0.5 * mod[:, 8][:, None] * rms_norm(f, g[5])
    return x, new_k, new_v, new_s


def setup_inputs(seed: int = 0) -> dict:
    key = jax.random.key(seed)
    ks = jax.random.split(key, 32)
    D = D_MODEL

    def nrm(i, shape, scale=1.0):
        return jax.random.normal(ks[i], shape, jnp.float32) * scale

    dt0 = jnp.exp(jax.random.uniform(ks[25], (N_SSM_LAYERS, 2, SSM_HEADS), jnp.float32,
                                     math.log(1e-3), math.log(1e-1)))
    return {
        'x_prompt': nrm(0, (BATCH, SEQ, D)),
        'x_sample': nrm(1, (DEC_BATCH, DEC_SEQ, D)),
        'cache_k': nrm(2, (DEC_BATCH, N_ATTN_LAYERS, PAST_LEN, N_KV_HEADS, HEAD_DIM)),
        'cache_v': nrm(3, (DEC_BATCH, N_ATTN_LAYERS, PAST_LEN, N_KV_HEADS, HEAD_DIM)),
        'state_ssm': nrm(4, (DEC_BATCH, N_SSM_LAYERS, 2, SSM_HEADS, SSM_HEAD_DIM, SSM_STATE), 0.1),
        'c': nrm(5, (DEC_BATCH, D)),
        'c_ctx': nrm(6, (D,)),
        'w_mod': nrm(7, (DEPTH, D, N_MOD * D), D ** -0.5),
        'b_mod': nrm(8, (DEPTH, N_MOD * D), 0.02),
        'norm_g': 1.0 + nrm(9, (DEPTH, 6, D), 0.02),
        'ffn_in': nrm(10, (DEPTH, 2, D, 2 * D_FF), D ** -0.5),
        'ffn_out': nrm(11, (DEPTH, 2, D_FF, D), D_FF ** -0.5),
        'gmlp_in': nrm(12, (N_GMLP_LAYERS, D, 2 * GMLP_HALF), D ** -0.5),
        'gmlp_ln_g': 1.0 + nrm(13, (N_GMLP_LAYERS, GMLP_HALF), 0.02),
        'gmlp_ln_b': nrm(14, (N_GMLP_LAYERS, GMLP_HALF), 0.02),
        'gmlp_ws': nrm(15, (N_GMLP_LAYERS, GMLP_GROUPS, CHUNK, CHUNK), CHUNK ** -0.5),
        'gmlp_bs': 1.0 + nrm(16, (N_GMLP_LAYERS, GMLP_GROUPS, CHUNK), 0.02),
        'gmlp_out': nrm(17, (N_GMLP_LAYERS, GMLP_HALF, D), GMLP_HALF ** -0.5),
        'attn_qkv': nrm(18, (N_ATTN_LAYERS, D, Q_DIM + 2 * KV_DIM), D ** -0.5),
        'attn_sink': nrm(19, (N_ATTN_LAYERS, N_Q_HEADS), 0.5),
        'attn_out': nrm(20, (N_ATTN_LAYERS, Q_DIM, D), Q_DIM ** -0.5),
        'ssm_in': nrm(21, (N_SSM_LAYERS, D, SSM_IN_DIM), D ** -0.5),
        'ssm_conv_w': nrm(22, (N_SSM_LAYERS, SSM_CONV, SSM_CONV_DIM), SSM_CONV ** -0.5),
        'ssm_conv_b': nrm(23, (N_SSM_LAYERS, SSM_CONV_DIM), 0.02),
        'ssm_dt_bias': dt0 + jnp.log(-jnp.expm1(-dt0)),
        'ssm_a_log': jnp.log(jax.random.uniform(ks[26], (N_SSM_LAYERS, 2, SSM_HEADS), jnp.float32, 1.0, 16.0)),
        'ssm_d': 1.0 + nrm(27, (N_SSM_LAYERS, SSM_HEADS), 0.02),
        'ssm_norm': 1.0 + nrm(28, (N_SSM_LAYERS, SSM_INNER), 0.02),
        'ssm_out': nrm(29, (N_SSM_LAYERS, SSM_INNER, D), SSM_INNER ** -0.5),
    }


def reference(x_prompt, x_sample, cache_k, cache_v, state_ssm, c, c_ctx,
              w_mod, b_mod, norm_g, ffn_in, ffn_out,
              gmlp_in, gmlp_ln_g, gmlp_ln_b, gmlp_ws, gmlp_bs, gmlp_out,
              attn_qkv, attn_sink, attn_out,
              ssm_in, ssm_conv_w, ssm_conv_b, ssm_dt_bias, ssm_a_log, ssm_d, ssm_norm, ssm_out):
    W = {
        'w_mod': w_mod, 'b_mod': b_mod, 'norm_g': norm_g, 'ffn_in': ffn_in, 'ffn_out': ffn_out,
        'gmlp_in': gmlp_in, 'gmlp_ln_g': gmlp_ln_g, 'gmlp_ln_b': gmlp_ln_b, 'gmlp_ws': gmlp_ws,
        'gmlp_bs': gmlp_bs, 'gmlp_out': gmlp_out,
        'attn_qkv': attn_qkv, 'attn_sink': attn_sink, 'attn_out': attn_out,
        'ssm_in': ssm_in, 'ssm_conv_w': ssm_conv_w, 'ssm_conv_b': ssm_conv_b,
        'ssm_dt_bias': ssm_dt_bias, 'ssm_a_log': ssm_a_log, 'ssm_d': ssm_d,
        'ssm_norm': ssm_norm, 'ssm_out': ssm_out,
    }
    y_prompt, ks_new, vs_new, ss_new = trunk(x_prompt, c_ctx[None, :], W, None, None, None)
    y_sample, _, _, _ = trunk(x_sample, c, W, cache_k, cache_v, state_ssm)
    new_cache_k = jnp.stack(ks_new, axis=1)
    new_cache_v = jnp.stack(vs_new, axis=1)
    new_state_ssm = jnp.stack(ss_new, axis=1)
    return (y_prompt, y_sample, new_cache_k, new_cache_v, new_state_ssm)
```

```python
import functools
import math

import jax
import jax.numpy as jnp
from jax import lax
from jax.experimental import pallas as pl
from jax.experimental.pallas import tpu as pltpu

F32 = jnp.float32
BF16 = jnp.bfloat16

D = 1024
N_CTX_SEQ, CTX_LEN = 16, 256
N_LAT_SEQ, LAT_LEN = 4, 1024
CTX_TOK = N_CTX_SEQ * CTX_LEN
LAT_TOK = N_LAT_SEQ * LAT_LEN
TOK = CTX_TOK + LAT_TOK
DEPTH = 4
N_MOD = 9
MOD_ROWS = 8
D_FF = 2816
EPS = 1e-6
GRID_W = 64
GMLP_HALF = 3 * D
GMLP_GROUPS = 8
GMLP_GD = GMLP_HALF // GMLP_GROUPS
CHUNK = 128
HEAD_DIM = 64
N_Q_HEADS = 16
N_KV_HEADS = 4
Q_PER_KV = 4
Q_DIM = N_Q_HEADS * HEAD_DIM
KV_DIM = N_KV_HEADS * HEAD_DIM
QKV_DIM = Q_DIM + 2 * KV_DIM
WINDOW = 128
ATTN_SCALE = HEAD_DIM ** -0.5
ROPE_BASE = 10000.0
ROT_PAIRS = HEAD_DIM // 4
NEG_INF = -1e30
SSM_INNER = 2 * D
SSM_HEADS = 32
SSM_P = 64
SSM_GROUPS = 4
SSM_HG = SSM_HEADS // SSM_GROUPS
SSM_STATE = 128
SSM_GN = SSM_GROUPS * SSM_STATE
SSM_CONV_DIM = SSM_INNER + 2 * SSM_GN
SSM_MAIN = SSM_INNER + SSM_CONV_DIM
LANES = 128
HALO = 8

VMEM_LIMIT = 56 * 1024 * 1024


def _cparams(sem):
    return pltpu.CompilerParams(dimension_semantics=sem, vmem_limit_bytes=VMEM_LIMIT)


def _sigmoid(x):
    return 1.0 / (1.0 + jnp.exp(-x))


def _silu(x):
    return x * _sigmoid(x)


def _rms(x, g):
    return x * lax.rsqrt(jnp.mean(x * x, axis=-1, keepdims=True) + EPS) * g


def _mod_row(i, tm):
    t0 = i * tm
    return jnp.where(t0 < CTX_TOK, 0, 1 + (t0 - CTX_TOK) // LAT_LEN)


def _mod_spec(layer, k, tm, grid_rank):
    base = layer * MOD_ROWS * N_MOD + k
    if grid_rank == 1:
        return pl.BlockSpec((None, 1, D), lambda i: (base + _mod_row(i, tm) * N_MOD, 0, 0))
    return pl.BlockSpec((None, 1, D), lambda i, j: (base + _mod_row(i, tm) * N_MOD, 0, 0))


def _gain_spec(layer, k, grid_rank):
    idx = layer * 6 + k
    if grid_rank == 1:
        return pl.BlockSpec((None, 1, D), lambda i: (idx, 0, 0))
    return pl.BlockSpec((None, 1, D), lambda i, j: (idx, 0, 0))


def _mod_kernel(cond_ref, w_ref, b_ref, o_ref):
    s = _silu(cond_ref[...]).astype(BF16)
    o_ref[...] = jnp.dot(s, w_ref[...].astype(BF16), preferred_element_type=F32) + b_ref[...]


def _modulation(cond, w_mod, b_mod):
    tn = 2304
    n = N_MOD * D
    return pl.pallas_call(
        _mod_kernel,
        grid=(DEPTH, n // tn),
        in_specs=[
            pl.BlockSpec((MOD_ROWS, D), lambda l, j: (0, 0)),
            pl.BlockSpec((None, D, tn), lambda l, j: (l, 0, j)),
            pl.BlockSpec((None, 1, tn), lambda l, j: (l, 0, j)),
        ],
        out_specs=pl.BlockSpec((None, MOD_ROWS, tn), lambda l, j: (l, 0, j)),
        out_shape=jax.ShapeDtypeStruct((DEPTH, MOD_ROWS, n), F32),
        compiler_params=_cparams(("arbitrary", "arbitrary")),
        name="modulation",
    )(cond, w_mod, b_mod.reshape(DEPTH, 1, n))


def _ffn_kernel(x_ref, sh_ref, sc_ref, gt_ref, g0_ref, g1_ref, wg_ref, wu_ref, wo_ref,
                o_ref, h_ref, acc_ref, *, n_f):
    j = pl.program_id(1)

    @pl.when(j == 0)
    def _():
        x = x_ref[...]
        h = _rms(x, g0_ref[...]) * (1.0 + sc_ref[...]) + sh_ref[...]
        h_ref[...] = h.astype(BF16)
        acc_ref[...] = jnp.zeros_like(acc_ref)

    h = h_ref[...]
    g = jnp.dot(h, wg_ref[...].astype(BF16), preferred_element_type=F32)
    u = jnp.dot(h, wu_ref[...].astype(BF16), preferred_element_type=F32)
    a = (_silu(g) * u).astype(BF16)
    acc_ref[...] += jnp.dot(a, wo_ref[...].astype(BF16), preferred_element_type=F32)

    @pl.when(j == n_f - 1)
    def _():
        f = acc_ref[...]
        o_ref[...] = x_ref[...] + (0.5 * gt_ref[...]) * _rms(f, g1_ref[...])


def _ffn(x, mods, gains, ffn_in, ffn_out, layer, which):
    tm, tf = 1024, 256
    n_f = D_FF // tf
    k0 = 0 if which == 0 else 6
    gi = 0 if which == 0 else 4
    return pl.pallas_call(
        functools.partial(_ffn_kernel, n_f=n_f),
        grid=(TOK // tm, n_f),
        in_specs=[
            pl.BlockSpec((tm, D), lambda i, j: (i, 0)),
            _mod_spec(layer, k0 + 0, tm, 2),
            _mod_spec(layer, k0 + 1, tm, 2),
            _mod_spec(layer, k0 + 2, tm, 2),
            _gain_spec(layer, gi, 2),
            _gain_spec(layer, gi + 1, 2),
            pl.BlockSpec((None, None, D, tf), lambda i, j: (layer, which, 0, j)),
            pl.BlockSpec((None, None, D, tf), lambda i, j: (layer, which, 0, n_f + j)),
            pl.BlockSpec((None, None, tf, D), lambda i, j: (layer, which, j, 0)),
        ],
        out_specs=pl.BlockSpec((tm, D), lambda i, j: (i, 0)),
        out_shape=jax.ShapeDtypeStruct((TOK, D), F32),
        scratch_shapes=[pltpu.VMEM((tm, D), BF16), pltpu.VMEM((tm, D), F32)],
        compiler_params=_cparams(("arbitrary", "arbitrary")),
        name=f"ffn_l{layer}_{which}",
    )(x, mods, mods, mods, gains, gains, ffn_in, ffn_in, ffn_out)


def _gelu_exact(x):
    return 0.5 * x * (1.0 + lax.erf(x * (1.0 / math.sqrt(2.0))))


def _softplus(x):
    return jnp.maximum(x, 0.0) + jnp.log1p(jnp.exp(-jnp.abs(x)))


def _proj_kernel(x_ref, sh_ref, sc_ref, g_ref, w_ref, *rest, act, has_bias):
    if has_bias:
        b_ref, o_ref, h_ref = rest
    else:
        o_ref, h_ref = rest
    j = pl.program_id(1)

    @pl.when(j == 0)
    def _():
        h = _rms(x_ref[...], g_ref[...]) * (1.0 + sc_ref[...]) + sh_ref[...]
        h_ref[...] = h.astype(BF16)

    y = jnp.dot(h_ref[...], w_ref[...].astype(BF16), preferred_element_type=F32)
    if has_bias:
        y = y + b_ref[...]
    if act == "gelu":
        y = _gelu_exact(y)
    elif act == "softplus":
        y = _softplus(y)
    o_ref[...] = y.astype(o_ref.dtype)


def _adaln_proj(x, mods, gains, w, widx, layer, *, n_out, tn, act=None, bias=None, tm=512):
    in_specs = [
        pl.BlockSpec((tm, D), lambda i, j: (i, 0)),
        _mod_spec(layer, 3, tm, 2),
        _mod_spec(layer, 4, tm, 2),
        _gain_spec(layer, 2, 2),
        pl.BlockSpec((None, D, tn), lambda i, j: (widx, 0, j)),
    ]
    args = [x, mods, mods, gains, w]
    if bias is not None:
        in_specs.append(pl.BlockSpec((1, tn), lambda i, j: (0, j)))
        args.append(bias)
    return pl.pallas_call(
        functools.partial(_proj_kernel, act=act, has_bias=bias is not None),
        grid=(TOK // tm, n_out // tn),
        in_specs=in_specs,
        out_specs=pl.BlockSpec((tm, tn), lambda i, j: (i, j)),
        out_shape=jax.ShapeDtypeStruct((TOK, n_out), F32),
        scratch_shapes=[pltpu.VMEM((tm, D), BF16)],
        compiler_params=_cparams(("arbitrary", "arbitrary")),
        name=f"proj_l{layer}_{n_out}",
    )(*args)


def _out_kernel(a_ref, w_ref, x_ref, gt_ref, g_ref, o_ref):
    m = jnp.dot(a_ref[...].astype(BF16), w_ref[...].astype(BF16), preferred_element_type=F32)
    o_ref[...] = x_ref[...] + gt_ref[...] * _rms(m, g_ref[...])


def _out_proj(a, w, widx, x, mods, gains, layer, tm=512):
    k = a.shape[1]
    return pl.pallas_call(
        _out_kernel,
        grid=(TOK // tm,),
        in_specs=[
            pl.BlockSpec((tm, k), lambda i: (i, 0)),
            pl.BlockSpec((None, k, D), lambda i: (widx, 0, 0)),
            pl.BlockSpec((tm, D), lambda i: (i, 0)),
            _mod_spec(layer, 5, tm, 1),
            _gain_spec(layer, 3, 1),
        ],
        out_specs=pl.BlockSpec((tm, D), lambda i: (i, 0)),
        out_shape=jax.ShapeDtypeStruct((TOK, D), F32),
        compiler_params=_cparams(("arbitrary",)),
        name=f"outproj_l{layer}",
    )(a, w, x, mods, gains)


GMLP_GPS = 2
GMLP_STEPS = GMLP_GROUPS // GMLP_GPS
GMLP_COLS = GMLP_GPS * GMLP_GD


def _gmlp_kernel(u_ref, v_ref, lg_ref, lb_ref, ws_ref, bs_ref, wo_ref, x_ref, gt_ref, g_ref,
                 o_ref, vn_ref, acc_ref, *, tm):
    j = pl.program_id(1)

    @pl.when(j == 0)
    def _():
        v = v_ref[...]
        mu = jnp.mean(v, axis=-1, keepdims=True)
        vc = v - mu
        var = jnp.mean(vc * vc, axis=-1, keepdims=True)
        vn = vc * lax.rsqrt(var + EPS) * lg_ref[...] + lb_ref[...]
        for s in range(GMLP_STEPS):
            vn_ref[s] = vn[:, s * GMLP_COLS:(s + 1) * GMLP_COLS].astype(BF16)
        acc_ref[...] = jnp.zeros_like(acc_ref)

    vn = vn_ref[j]
    u = u_ref[...]
    rows = []
    for c in range(tm // CHUNK):
        cols = []
        for gg in range(GMLP_GPS):
            vg = vn[c * CHUNK:(c + 1) * CHUNK, gg * GMLP_GD:(gg + 1) * GMLP_GD]
            mixed = jnp.dot(ws_ref[gg].astype(BF16), vg, preferred_element_type=F32) + bs_ref[gg]
            cols.append(mixed)
        rows.append(jnp.concatenate(cols, axis=1))
    sv = jnp.concatenate(rows, axis=0)
    a = (u * sv).astype(BF16)
    acc_ref[...] += jnp.dot(a, wo_ref[...].astype(BF16), preferred_element_type=F32)

    @pl.when(j == GMLP_STEPS - 1)
    def _():
        o_ref[...] = x_ref[...] + gt_ref[...] * _rms(acc_ref[...], g_ref[...])


def _gmlp_core(uv, ln_g, ln_b, w_s, b_s, w_out, widx, x, mods, gains, layer, tm=256):
    return pl.pallas_call(
        functools.partial(_gmlp_kernel, tm=tm),
        grid=(TOK // tm, GMLP_STEPS),
        in_specs=[
            pl.BlockSpec((tm, GMLP_COLS), lambda i, j: (i, j)),
            pl.BlockSpec((tm, GMLP_HALF), lambda i, j: (i, 1)),
            pl.BlockSpec((None, 1, GMLP_HALF), lambda i, j: (widx, 0, 0)),
            pl.BlockSpec((None, 1, GMLP_HALF), lambda i, j: (widx, 0, 0)),
            pl.BlockSpec((None, GMLP_GPS, CHUNK, CHUNK), lambda i, j: (widx, j, 0, 0)),
            pl.BlockSpec((None, GMLP_GPS, CHUNK, 1), lambda i, j: (widx, j, 0, 0)),
            pl.BlockSpec((None, GMLP_COLS, D), lambda i, j: (widx, j, 0)),
            pl.BlockSpec((tm, D), lambda i, j: (i, 0)),
            _mod_spec(layer, 5, tm, 2),
            _gain_spec(layer, 3, 2),
        ],
        out_specs=pl.BlockSpec((tm, D), lambda i, j: (i, 0)),
        out_shape=jax.ShapeDtypeStruct((TOK, D), F32),
        scratch_shapes=[pltpu.VMEM((GMLP_STEPS, tm, GMLP_COLS), BF16), pltpu.VMEM((tm, D), F32)],
        compiler_params=_cparams(("arbitrary", "arbitrary")),
        name=f"gmlp_l{layer}",
    )(uv, uv, ln_g, ln_b, w_s, b_s, w_out, x, mods, gains)


def _rope_tables():
    pos = jnp.arange(LAT_LEN)
    pos_r = (pos // GRID_W).astype(F32)
    pos_c = (pos % GRID_W).astype(F32)
    inv = ROPE_BASE ** (-jnp.arange(ROT_PAIRS, dtype=F32) / ROT_PAIRS)
    ang_r = pos_r[:, None] * inv
    ang_c = pos_c[:, None] * inv
    cos = jnp.concatenate([jnp.cos(ang_r)] * 2 + [jnp.cos(ang_c)] * 2, axis=1)
    sin = jnp.concatenate([-jnp.sin(ang_r), jnp.sin(ang_r), -jnp.sin(ang_c), jnp.sin(ang_c)], axis=1)
    reps = LANES // HEAD_DIM
    return jnp.tile(cos, (1, reps)), jnp.tile(sin, (1, reps))


def _qkv_kernel(x_ref, sh_ref, sc_ref, g_ref, w_ref, cos_ref, sin_ref, o_ref, *, tm):
    i = pl.program_id(0)
    h = _rms(x_ref[...], g_ref[...]) * (1.0 + sc_ref[...]) + sh_ref[...]
    y = jnp.dot(h.astype(BF16), w_ref[...].astype(BF16), preferred_element_type=F32)

    @pl.when(i * tm < CTX_TOK)
    def _():
        o_ref[...] = y

    @pl.when(i * tm >= CTX_TOK)
    def _():
        cos = cos_ref[...]
        sin = sin_ref[...]
        lane = lax.broadcasted_iota(jnp.int32, (tm, LANES), 1)
        first = (lane % (2 * ROT_PAIRS)) < ROT_PAIRS
        for cb in range((Q_DIM + KV_DIM) // LANES):
            t = y[:, cb * LANES:(cb + 1) * LANES]
            partner = jnp.where(first, pltpu.roll(t, LANES - ROT_PAIRS, axis=1),
                                pltpu.roll(t, ROT_PAIRS, axis=1))
            o_ref[:, cb * LANES:(cb + 1) * LANES] = t * cos + partner * sin
        o_ref[:, Q_DIM + KV_DIM:] = y[:, Q_DIM + KV_DIM:]


def _qkv_proj(x, mods, gains, w, widx, layer, tm=512):
    cos, sin = _rope_tables()
    per_seq = LAT_LEN // tm
    tab = pl.BlockSpec((tm, LANES), lambda i: (jnp.maximum(i - CTX_TOK // tm, 0) % per_seq, 0))
    return pl.pallas_call(
        functools.partial(_qkv_kernel, tm=tm),
        grid=(TOK // tm,),
        in_specs=[
            pl.BlockSpec((tm, D), lambda i: (i, 0)),
            _mod_spec(layer, 3, tm, 1),
            _mod_spec(layer, 4, tm, 1),
            _gain_spec(layer, 2, 1),
            pl.BlockSpec((None, D, QKV_DIM), lambda i: (widx, 0, 0)),
            tab, tab,
        ],
        out_specs=pl.BlockSpec((tm, QKV_DIM), lambda i: (i, 0)),
        out_shape=jax.ShapeDtypeStruct((TOK, QKV_DIM), F32),
        compiler_params=_cparams(("arbitrary",)),
        name=f"qkv_l{layer}",
    )(x, mods, mods, gains, w, cos, sin)


def _attend(q4, keys, vals, sink_col, masks):
    logits = []
    for k, mk in zip(keys, masks):
        s = lax.dot_general(q4, k, (((1,), (1,)), ((), ())), preferred_element_type=F32) * ATTN_SCALE
        if mk is not None:
            s = jnp.where(mk, s, NEG_INF)
        logits.append(s)
    m = sink_col
    for s in logits:
        m = jnp.maximum(m, jnp.max(s, axis=-1, keepdims=True))
    denom = jnp.exp(sink_col - m)
    o = None
    for s, v in zip(logits, vals):
        p = jnp.exp(s - m)
        denom = denom + jnp.sum(p, axis=-1, keepdims=True)
        pv = jnp.dot(p.astype(BF16), v, preferred_element_type=F32)
        o = pv if o is None else o + pv
    return o / denom


def _attn_ctx_kernel(sink_ref, q_ref, k_ref, v_ref, o_ref):
    rows = CTX_LEN
    outs = [None] * N_Q_HEADS
    for hk in range(N_KV_HEADS):
        k = k_ref[:, hk * HEAD_DIM:(hk + 1) * HEAD_DIM].astype(BF16)
        v = v_ref[:, hk * HEAD_DIM:(hk + 1) * HEAD_DIM].astype(BF16)
        heads = [hk * Q_PER_KV + g for g in range(Q_PER_KV)]
        q4 = jnp.concatenate([q_ref[:, h * HEAD_DIM:(h + 1) * HEAD_DIM] for h in heads], axis=0).astype(BF16)
        sink_col = jnp.concatenate([jnp.full((rows, 1), sink_ref[h], F32) for h in heads], axis=0)
        o4 = _attend(q4, [k], [v], sink_col, [None])
        for g, h in enumerate(heads):
            outs[h] = o4[g * rows:(g + 1) * rows]
    o_ref[...] = jnp.concatenate(outs, axis=1).astype(o_ref.dtype)


def _attn_lat_kernel(sink_ref, q_ref, kp_ref, kc_ref, kn_ref, vp_ref, vc_ref, vn_ref, ck_ref, cv_ref,
                     o_ref, *, n_blk):
    qi = pl.program_id(1)
    rows = CHUNK
    r4 = Q_PER_KV * rows
    a = lax.broadcasted_iota(jnp.int32, (r4, CHUNK), 0) % rows
    s = lax.broadcasted_iota(jnp.int32, (r4, CHUNK), 1)
    mask_prev = (s >= a) & (qi > 0)
    mask_next = (s <= a) & (qi < n_blk - 1)
    outs = [None] * N_Q_HEADS
    for hk in range(N_KV_HEADS):
        sl = slice(hk * HEAD_DIM, (hk + 1) * HEAD_DIM)
        keys = [kp_ref[:, sl].astype(BF16), kc_ref[:, sl].astype(BF16), kn_ref[:, sl].astype(BF16),
                ck_ref[:, sl].astype(BF16)]
        vals = [vp_ref[:, sl].astype(BF16), vc_ref[:, sl].astype(BF16), vn_ref[:, sl].astype(BF16),
                cv_ref[:, sl].astype(BF16)]
        heads = [hk * Q_PER_KV + g for g in range(Q_PER_KV)]
        q4 = jnp.concatenate([q_ref[:, h * HEAD_DIM:(h + 1) * HEAD_DIM] for h in heads], axis=0).astype(BF16)
        sink_col = jnp.concatenate([jnp.full((rows, 1), sink_ref[h], F32) for h in heads], axis=0)
        o4 = _attend(q4, keys, vals, sink_col, [mask_prev, None, mask_next, None])
        for g, h in enumerate(heads):
            outs[h] = o4[g * rows:(g + 1) * rows]
    o_ref[...] = jnp.concatenate(outs, axis=1).astype(o_ref.dtype)


def _attention(qkv, sinks, ctx_k, ctx_v):
    smem = pl.BlockSpec(memory_space=pltpu.SMEM)
    kcol, vcol = Q_DIM // KV_DIM, Q_DIM // KV_DIM + 1
    ctx_blk = CTX_LEN
    o_ctx = pl.pallas_call(
        _attn_ctx_kernel,
        grid=(N_CTX_SEQ,),
        in_specs=[
            smem,
            pl.BlockSpec((ctx_blk, Q_DIM), lambda b: (b, 0)),
            pl.BlockSpec((ctx_blk, KV_DIM), lambda b: (b, kcol)),
            pl.BlockSpec((ctx_blk, KV_DIM), lambda b: (b, vcol)),
        ],
        out_specs=pl.BlockSpec((ctx_blk, Q_DIM), lambda b: (b, 0)),
        out_shape=jax.ShapeDtypeStruct((CTX_TOK, Q_DIM), BF16),
        compiler_params=_cparams(("arbitrary",)),
        name="attn_ctx",
    )(sinks, qkv, qkv, qkv)

    n_blk = LAT_LEN // CHUNK
    base = CTX_TOK // CHUNK

    def rb(b, qi, off):
        return base + b * n_blk + jnp.clip(qi + off, 0, n_blk - 1)

    def kv_spec(col, off):
        return pl.BlockSpec((CHUNK, KV_DIM), lambda b, qi: (rb(b, qi, off), col))

    o_lat = pl.pallas_call(
        functools.partial(_attn_lat_kernel, n_blk=n_blk),
        grid=(N_LAT_SEQ, n_blk),
        in_specs=[
            smem,
            pl.BlockSpec((CHUNK, Q_DIM), lambda b, qi: (rb(b, qi, 0), 0)),
            kv_spec(kcol, -1), kv_spec(kcol, 0), kv_spec(kcol, 1),
            kv_spec(vcol, -1), kv_spec(vcol, 0), kv_spec(vcol, 1),
            pl.BlockSpec((None, CTX_LEN, KV_DIM), lambda b, qi: (b, 0, 0)),
            pl.BlockSpec((None, CTX_LEN, KV_DIM), lambda b, qi: (b, 0, 0)),
        ],
        out_specs=pl.BlockSpec((CHUNK, Q_DIM), lambda b, qi: (b * n_blk + qi, 0)),
        out_shape=jax.ShapeDtypeStruct((LAT_TOK, Q_DIM), BF16),
        compiler_params=_cparams(("arbitrary", "arbitrary")),
        name="attn_lat",
    )(sinks, qkv, qkv, qkv, qkv, qkv, qkv, qkv, ctx_k, ctx_v)
    return jnp.concatenate([o_ctx, o_lat], axis=0)


def _ssd_kernel(*refs, nc, has_h0, emit_state):
    (z_ref, x_ref, xp_ref, xn_ref, bc_ref, bcp_ref, bcn_ref, dt_ref,
     cw_ref, cb_ref, a_ref, dsk_ref, ng_ref) = refs[:13]
    pos = 13
    h0_ref = None
    if has_h0:
        h0_ref = refs[pos]
        pos += 1
    y_ref = refs[pos]
    pos += 1
    st_ref = None
    if emit_state:
        st_ref = refs[pos]
        pos += 1
    hs_ref, yf_ref = refs[pos:pos + 2]

    s = pl.program_id(1)
    bwd = s >= nc
    c = jnp.where(bwd, 2 * nc - 1 - s, s)
    L = CHUNK

    @pl.when((s == 0) | (s == nc))
    def _():
        if has_h0:
            @pl.when(s == 0)
            def _():
                hs_ref[...] = h0_ref[0]

            @pl.when(s == nc)
            def _():
                hs_ref[...] = h0_ref[1]
        else:
            hs_ref[...] = jnp.zeros_like(hs_ref)

    row = lax.broadcasted_iota(jnp.int32, (L, 1), 0)

    def conv_silu(cur, prev_blk, next_blk, lo, hi):
        w = cw_ref[:, lo:hi]
        prev_row = jnp.where(c > 0, prev_blk[HALO - 1:HALO, :], 0.0)
        next_row = jnp.where(c < nc - 1, next_blk[0:1, :], 0.0)
        up = jnp.where(row == 0, prev_row, pltpu.roll(cur, 1, axis=0))
        dn = jnp.where(row == L - 1, next_row, pltpu.roll(cur, L - 1, axis=0))
        y = up * w[0:1] + cur * w[1:2] + dn * w[2:3] + cb_ref[:, lo:hi]
        return _silu(y)

    xs = conv_silu(x_ref[...], xp_ref[...], xn_ref[...], 0, SSM_INNER)
    bc = conv_silu(bc_ref[...], bcp_ref[...], bcn_ref[...], SSM_INNER, SSM_CONV_DIM)

    dt_all = dt_ref[...]
    dt = jnp.where(bwd, pltpu.roll(dt_all, LANES - SSM_HEADS, axis=1), dt_all)
    a_row = jnp.where(bwd, a_ref[1:2, :], a_ref[0:1, :])
    dta = dt * a_row
    ii = lax.broadcasted_iota(jnp.int32, (L, L), 0)
    jj = lax.broadcasted_iota(jnp.int32, (L, L), 1)
    tri = jnp.where(bwd, jj - ii, ii - jj) >= 0
    tri_b = jnp.where(tri, 1.0, 0.0).astype(BF16)
    hi_p = dta.astype(BF16)
    r1 = dta - hi_p.astype(F32)
    mid_p = r1.astype(BF16)
    lo_p = (r1 - mid_p.astype(F32)).astype(BF16)
    acum = (jnp.dot(tri_b, hi_p, preferred_element_type=F32)
            + jnp.dot(tri_b, mid_p, preferred_element_type=F32)
            + jnp.dot(tri_b, lo_p, preferred_element_type=F32))
    total = jnp.where(bwd, acum[0:1, :], acum[L - 1:L, :])
    acum_t = acum.T
    w_end = dt * jnp.exp(total - acum)
    e_in = jnp.exp(acum)
    cdec = jnp.exp(total)
    cdec_t = jnp.exp(jnp.where(bwd, acum_t[:, 0:1], acum_t[:, L - 1:L]))

    dsk = dsk_ref[...]
    y_parts = []
    for g in range(SSM_GROUPS):
        bm = bc[:, g * SSM_STATE:(g + 1) * SSM_STATE].astype(BF16)
        cm = bc[:, SSM_GN + g * SSM_STATE:SSM_GN + (g + 1) * SSM_STATE].astype(BF16)
        cb = lax.dot_general(cm, bm, (((1,), (1,)), ((), ())), preferred_element_type=F32)
        gsl = slice(g * SSM_HG * SSM_P, (g + 1) * SSM_HG * SSM_P)
        hprev = hs_ref[gsl, :]
        y_off = lax.dot_general(cm, hprev.astype(BF16), (((1,), (1,)), ((), ())),
                                preferred_element_type=F32)
        xw_cols, y_cols, dec_rows = [], [], []
        for k in range(SSM_HG):
            h = g * SSM_HG + k
            xh = xs[:, h * SSM_P:(h + 1) * SSM_P]
            seg = acum[:, h:h + 1] - acum_t[h:h + 1, :]
            decay = jnp.exp(jnp.where(tri, seg, -jnp.inf))
            wmat = (cb * decay).astype(BF16)
            xdt = (xh * dt[:, h:h + 1]).astype(BF16)
            yd = jnp.dot(wmat, xdt, preferred_element_type=F32)
            y_cols.append(yd + y_off[:, k * SSM_P:(k + 1) * SSM_P] * e_in[:, h:h + 1])
            xw_cols.append((xh * w_end[:, h:h + 1]).astype(BF16))
            dec_rows.append(jnp.broadcast_to(cdec_t[h:h + 1, :], (SSM_P, 1)))
        xw = jnp.concatenate(xw_cols, axis=1)
        st = lax.dot_general(xw, bm, (((0,), (0,)), ((), ())), preferred_element_type=F32)
        hs_ref[gsl, :] = jnp.concatenate(dec_rows, axis=0) * hprev + st
        y_parts.append(jnp.concatenate(y_cols, axis=1))
    y = jnp.concatenate(y_parts, axis=1)
    del cdec

    off = pl.multiple_of(c * L, L)

    @pl.when(jnp.logical_not(bwd))
    def _():
        yf_ref[pl.ds(off, L), :] = y + dsk * xs

    @pl.when(bwd)
    def _():
        yt = (y + yf_ref[pl.ds(off, L), :]) * _silu(z_ref[...])
        y_ref[...] = _rms(yt, ng_ref[...]).astype(y_ref.dtype)

    if emit_state:
        @pl.when(s == nc - 1)
        def _():
            st_ref[0] = hs_ref[...]

        @pl.when(s == 2 * nc - 1)
        def _():
            st_ref[1] = hs_ref[...]


def _ssd(zx, dt, conv_w, conv_b, a_pad, dsk, norm_g, h0, *, seq0, n_seq, seq_len, emit_state):
    nc = seq_len // CHUNK
    hb = CHUNK // HALO
    n_rows_halo = TOK // HALO
    chunk0 = seq0 // CHUNK

    def cidx(b, s):
        return chunk0 + b * nc + jnp.where(s >= nc, 2 * nc - 1 - s, s)

    def zidx(b, s):
        return chunk0 + b * nc + jnp.where(s >= nc, 2 * nc - 1 - s, nc - 1)

    def prev_halo(b, s):
        return jnp.maximum(cidx(b, s) * hb - 1, 0)

    def next_halo(b, s):
        return jnp.minimum(cidx(b, s) * hb + hb, n_rows_halo - 1)

    xcol = SSM_INNER // SSM_INNER
    bccol = (2 * SSM_INNER) // (2 * SSM_GN)
    in_specs = [
        pl.BlockSpec((CHUNK, SSM_INNER), lambda b, s: (zidx(b, s), 0)),
        pl.BlockSpec((CHUNK, SSM_INNER), lambda b, s: (cidx(b, s), xcol)),
        pl.BlockSpec((HALO, SSM_INNER), lambda b, s: (prev_halo(b, s), xcol)),
        pl.BlockSpec((HALO, SSM_INNER), lambda b, s: (next_halo(b, s), xcol)),
        pl.BlockSpec((CHUNK, 2 * SSM_GN), lambda b, s: (cidx(b, s), bccol)),
        pl.BlockSpec((HALO, 2 * SSM_GN), lambda b, s: (prev_halo(b, s), bccol)),
        pl.BlockSpec((HALO, 2 * SSM_GN), lambda b, s: (next_halo(b, s), bccol)),
        pl.BlockSpec((CHUNK, LANES), lambda b, s: (cidx(b, s), 0)),
        pl.BlockSpec((3, SSM_CONV_DIM), lambda b, s: (0, 0)),
        pl.BlockSpec((1, SSM_CONV_DIM), lambda b, s: (0, 0)),
        pl.BlockSpec((2, LANES), lambda b, s: (0, 0)),
        pl.BlockSpec((1, SSM_INNER), lambda b, s: (0, 0)),
        pl.BlockSpec((1, SSM_INNER), lambda b, s: (0, 0)),
    ]
    args = [zx, zx, zx, zx, zx, zx, zx, dt, conv_w, conv_b, a_pad, dsk, norm_g]
    if h0 is not None:
        in_specs.append(pl.BlockSpec((None, 2, SSM_INNER, SSM_STATE), lambda b, s: (b, 0, 0, 0)))
        args.append(h0)
    out_specs = [pl.BlockSpec((CHUNK, SSM_INNER), lambda b, s: (zidx(b, s) - chunk0, 0))]
    out_shape = [jax.ShapeDtypeStruct((n_seq * seq_len, SSM_INNER), BF16)]
    if emit_state:
        out_specs.append(pl.BlockSpec((None, 2, SSM_INNER, SSM_STATE), lambda b, s: (b, 0, 0, 0)))
        out_shape.append(jax.ShapeDtypeStruct((n_seq, 2, SSM_INNER, SSM_STATE), F32))
    return pl.pallas_call(
        functools.partial(_ssd_kernel, nc=nc, has_h0=h0 is not None, emit_state=emit_state),
        grid=(n_seq, 2 * nc),
        in_specs=in_specs,
        out_specs=out_specs,
        out_shape=out_shape,
        scratch_shapes=[pltpu.VMEM((SSM_INNER, SSM_STATE), F32), pltpu.VMEM((seq_len, SSM_INNER), F32)],
        compiler_params=_cparams(("arbitrary", "arbitrary")),
        name=f"ssd_{seq_len}",
    )(*args)


def kernel(x_prompt, x_sample, cache_k, cache_v, state_ssm, c, c_ctx, w_mod, b_mod, norm_g, ffn_in, ffn_out,
           gmlp_in, gmlp_ln_g, gmlp_ln_b, gmlp_ws, gmlp_bs, gmlp_out, attn_qkv, attn_sink, attn_out,
           ssm_in, ssm_conv_w, ssm_conv_b, ssm_dt_bias, ssm_a_log, ssm_d, ssm_norm, ssm_out):
    x = jnp.concatenate([x_prompt.reshape(CTX_TOK, D), x_sample.reshape(LAT_TOK, D)], axis=0)
    cond = jnp.concatenate([c_ctx[None, :], c, jnp.zeros((MOD_ROWS - 1 - N_LAT_SEQ, D), F32)], axis=0)
    mods = _modulation(cond, w_mod, b_mod).reshape(DEPTH * MOD_ROWS * N_MOD, 1, D)
    gains = norm_g.reshape(DEPTH * 6, 1, D)

    new_k = new_v = new_s = None
    for layer in range(DEPTH):
        x = _ffn(x, mods, gains, ffn_in, ffn_out, layer, 0)
        kind, j = layer % 3, layer // 3
        if kind == 0:
            uv = _adaln_proj(x, mods, gains, gmlp_in, j, layer, n_out=2 * GMLP_HALF, tn=1024, act="gelu")
            x = _gmlp_core(uv, gmlp_ln_g[:, None, :], gmlp_ln_b[:, None, :], gmlp_ws,
                           gmlp_bs[..., None], gmlp_out, j, x, mods, gains, layer)
        elif kind == 1:
            qkv = _qkv_proj(x, mods, gains, attn_qkv, j, layer)
            kv = qkv[:CTX_TOK, Q_DIM:].reshape(N_CTX_SEQ, CTX_LEN, 2, N_KV_HEADS, HEAD_DIM)
            new_k, new_v = kv[:, :, 0], kv[:, :, 1]
            o = _attention(qkv, attn_sink[j],
                           cache_k[:, j].reshape(N_LAT_SEQ, CTX_LEN, KV_DIM),
                           cache_v[:, j].reshape(N_LAT_SEQ, CTX_LEN, KV_DIM))
            x = _out_proj(o, attn_out, j, x, mods, gains, layer)
        else:
            zx = _adaln_proj(x, mods, gains, ssm_in, j, layer, n_out=SSM_MAIN, tn=1024)
            pad = LANES - 2 * SSM_HEADS
            w_dt = jnp.pad(ssm_in[j][:, SSM_MAIN:], ((0, 0), (0, pad)))[None]
            b_dt = jnp.pad(ssm_dt_bias[j].reshape(1, 2 * SSM_HEADS), ((0, 0), (0, pad)))
            dt = _adaln_proj(x, mods, gains, w_dt, 0, layer, n_out=LANES, tn=LANES, act="softplus", bias=b_dt)
            a_pad = jnp.pad(-jnp.exp(ssm_a_log[j]), ((0, 0), (0, LANES - SSM_HEADS)))
            dsk = jnp.repeat(ssm_d[j], SSM_P)[None, :]
            ng = ssm_norm[j][None, :]
            cw, cb = ssm_conv_w[j], ssm_conv_b[j][None, :]
            y_ctx, st = _ssd(zx, dt, cw, cb, a_pad, dsk, ng, None,
                             seq0=0, n_seq=N_CTX_SEQ, seq_len=CTX_LEN, emit_state=True)
            (y_lat,) = _ssd(zx, dt, cw, cb, a_pad, dsk, ng,
                            state_ssm[:, j].reshape(N_LAT_SEQ, 2, SSM_INNER, SSM_STATE),
                            seq0=CTX_TOK, n_seq=N_LAT_SEQ, seq_len=LAT_LEN, emit_state=False)
            new_s = st.reshape(N_CTX_SEQ, 2, SSM_HEADS, SSM_P, SSM_STATE)
            x = _out_proj(jnp.concatenate([y_ctx, y_lat], axis=0), ssm_out, j, x, mods, gains, layer)
        x = _ffn(x, mods, gains, ffn_in, ffn_out, layer, 1)

    y_prompt = x[:CTX_TOK].reshape(N_CTX_SEQ, CTX_LEN, D)
    y_sample = x[CTX_TOK:].reshape(N_LAT_SEQ, LAT_LEN, D)
    return (y_prompt, y_sample, new_k[:, None], new_v[:, None], new_s[:, None])
```

```python
import functools
import math

import jax
import jax.numpy as jnp
from jax import lax
from jax.experimental import pallas as pl
from jax.experimental.pallas import tpu as pltpu

F32 = jnp.float32
BF16 = jnp.bfloat16

D = 1024
N_CTX_SEQ, CTX_LEN = 16, 256
N_LAT_SEQ, LAT_LEN = 4, 1024
CTX_TOK = N_CTX_SEQ * CTX_LEN
LAT_TOK = N_LAT_SEQ * LAT_LEN
TOK = CTX_TOK + LAT_TOK
DEPTH = 4
N_MOD = 9
MOD_ROWS = 8
D_FF = 2816
EPS = 1e-6
GRID_W = 64
GMLP_HALF = 3 * D
GMLP_GROUPS = 8
GMLP_GD = GMLP_HALF // GMLP_GROUPS
CHUNK = 128
HEAD_DIM = 64
N_Q_HEADS = 16
N_KV_HEADS = 4
Q_PER_KV = 4
Q_DIM = N_Q_HEADS * HEAD_DIM
KV_DIM = N_KV_HEADS * HEAD_DIM
QKV_DIM = Q_DIM + 2 * KV_DIM
ATTN_SCALE = HEAD_DIM ** -0.5
ROPE_BASE = 10000.0
ROT_PAIRS = HEAD_DIM // 4
NEG_INF = -1e30
SSM_INNER = 2 * D
SSM_HEADS = 32
SSM_P = 64
SSM_GROUPS = 4
SSM_HG = SSM_HEADS // SSM_GROUPS
SSM_STATE = 128
SSM_GN = SSM_GROUPS * SSM_STATE
SSM_CONV_DIM = SSM_INNER + 2 * SSM_GN
SSM_MAIN = SSM_INNER + SSM_CONV_DIM
LANES = 128
HALO = 8

VMEM_LIMIT = 56 * 1024 * 1024
VMEM_LIMIT_GMLP = 60 * 1024 * 1024


def _cparams(sem, limit=VMEM_LIMIT):
    return pltpu.CompilerParams(dimension_semantics=sem, vmem_limit_bytes=limit)


def _sigmoid(x):
    return 1.0 / (1.0 + jnp.exp(-x))


def _silu(x):
    return x * _sigmoid(x)


def _rms(x, g):
    return x * lax.rsqrt(jnp.mean(x * x, axis=-1, keepdims=True) + EPS) * g


def _adaln(x, g, shift, scale):
    return _rms(x, g) * (1.0 + scale) + shift


def _mod_row(i, tm):
    t0 = i * tm
    return jnp.where(t0 < CTX_TOK, 0, 1 + (t0 - CTX_TOK) // LAT_LEN)


def _mod_spec(layer, k, tm, grid_rank):
    base = layer * MOD_ROWS * N_MOD + k
    if grid_rank == 1:
        return pl.BlockSpec((None, 1, D), lambda i: (base + _mod_row(i, tm) * N_MOD, 0, 0))
    return pl.BlockSpec((None, 1, D), lambda i, j: (base + _mod_row(i, tm) * N_MOD, 0, 0))


def _gain_spec(layer, k, grid_rank):
    idx = layer * 6 + k
    if grid_rank == 1:
        return pl.BlockSpec((None, 1, D), lambda i: (idx, 0, 0))
    return pl.BlockSpec((None, 1, D), lambda i, j: (idx, 0, 0))


def _split_specs(tm, width, grid_rank):
    na = CTX_TOK // tm
    if grid_rank == 1:
        return (pl.BlockSpec((tm, width), lambda i: (jnp.minimum(i, na - 1), 0)),
                pl.BlockSpec((tm, width), lambda i: (jnp.maximum(i - na, 0), 0)))
    return (pl.BlockSpec((tm, width), lambda i, j: (jnp.minimum(i, na - 1), 0)),
            pl.BlockSpec((tm, width), lambda i, j: (jnp.maximum(i - na, 0), 0)))


def _mod_kernel(cond_ref, w_ref, b_ref, o_ref):
    s = _silu(cond_ref[...]).astype(BF16)
    o_ref[...] = jnp.dot(s, w_ref[...].astype(BF16), preferred_element_type=F32) + b_ref[...]


def _modulation(cond, w_mod, b_mod):
    tn = 2304
    n = N_MOD * D
    return pl.pallas_call(
        _mod_kernel,
        grid=(DEPTH, n // tn),
        in_specs=[
            pl.BlockSpec((MOD_ROWS, D), lambda l, j: (0, 0)),
            pl.BlockSpec((None, D, tn), lambda l, j: (l, 0, j)),
            pl.BlockSpec((None, 1, tn), lambda l, j: (l, 0, j)),
        ],
        out_specs=pl.BlockSpec((None, MOD_ROWS, tn), lambda l, j: (l, 0, j)),
        out_shape=jax.ShapeDtypeStruct((DEPTH, MOD_ROWS, n), F32),
        compiler_params=_cparams(("arbitrary", "arbitrary")),
        name="modulation",
    )(cond, w_mod, b_mod.reshape(DEPTH, 1, n))


def _ffn_kernel(*refs, n_f, tm, split_in, split_out):
    n_x = 2 if split_in else 1
    n_o = 2 if split_out else 1
    x_refs = refs[:n_x]
    sh_ref, sc_ref, gt_ref, g0_ref, g1_ref, wg_ref, wu_ref, wo_ref = refs[n_x:n_x + 8]
    o_refs = refs[n_x + 8:n_x + 8 + n_o]
    h_ref, acc_ref = refs[n_x + 8 + n_o:]
    i = pl.program_id(0)
    j = pl.program_id(1)
    is_ctx = i * tm < CTX_TOK

    def load_x():
        if split_in:
            return jnp.where(is_ctx, x_refs[0][...], x_refs[1][...])
        return x_refs[0][...]

    @pl.when(j == 0)
    def _():
        h_ref[...] = _adaln(load_x(), g0_ref[...], sh_ref[...], sc_ref[...]).astype(BF16)
        acc_ref[...] = jnp.zeros_like(acc_ref)

    h = h_ref[...]
    g = jnp.dot(h, wg_ref[...].astype(BF16), preferred_element_type=F32)
    u = jnp.dot(h, wu_ref[...].astype(BF16), preferred_element_type=F32)
    a = (_silu(g) * u).astype(BF16)
    acc_ref[...] += jnp.dot(a, wo_ref[...].astype(BF16), preferred_element_type=F32)

    @pl.when(j == n_f - 1)
    def _():
        res = load_x() + (0.5 * gt_ref[...]) * _rms(acc_ref[...], g1_ref[...])
        if split_out:
            @pl.when(is_ctx)
            def _():
                o_refs[0][...] = res

            @pl.when(jnp.logical_not(is_ctx))
            def _():
                o_refs[1][...] = res
        else:
            o_refs[0][...] = res


def _ffn(x, mods, gains, ffn_in, ffn_out, layer, which, split_in=False, split_out=False):
    tm, tf = 1024, 256
    n_f = D_FF // tf
    k0 = 0 if which == 0 else 6
    gi = 0 if which == 0 else 4
    x_specs = list(_split_specs(tm, D, 2)) if split_in else [pl.BlockSpec((tm, D), lambda i, j: (i, 0))]
    x_args = list(x) if split_in else [x]
    if split_out:
        out_specs = list(_split_specs(tm, D, 2))
        out_shape = [jax.ShapeDtypeStruct((CTX_TOK, D), F32), jax.ShapeDtypeStruct((LAT_TOK, D), F32)]
    else:
        out_specs = pl.BlockSpec((tm, D), lambda i, j: (i, 0))
        out_shape = jax.ShapeDtypeStruct((TOK, D), F32)
    return pl.pallas_call(
        functools.partial(_ffn_kernel, n_f=n_f, tm=tm, split_in=split_in, split_out=split_out),
        grid=(TOK // tm, n_f),
        in_specs=x_specs + [
            _mod_spec(layer, k0 + 0, tm, 2),
            _mod_spec(layer, k0 + 1, tm, 2),
            _mod_spec(layer, k0 + 2, tm, 2),
            _gain_spec(layer, gi, 2),
            _gain_spec(layer, gi + 1, 2),
            pl.BlockSpec((None, None, D, tf), lambda i, j: (layer, which, 0, j)),
            pl.BlockSpec((None, None, D, tf), lambda i, j: (layer, which, 0, n_f + j)),
            pl.BlockSpec((None, None, tf, D), lambda i, j: (layer, which, j, 0)),
        ],
        out_specs=out_specs,
        out_shape=out_shape,
        scratch_shapes=[pltpu.VMEM((tm, D), BF16), pltpu.VMEM((tm, D), F32)],
        compiler_params=_cparams(("arbitrary", "arbitrary")),
        name=f"ffn_l{layer}_{which}",
    )(*x_args, mods, mods, mods, gains, gains, ffn_in, ffn_in, ffn_out)


def _out_kernel(aa_ref, ab_ref, w_ref, x_ref, gt_ref, g_ref, o_ref, *, tm):
    a = jnp.where(pl.program_id(0) * tm < CTX_TOK, aa_ref[...], ab_ref[...])
    m = jnp.dot(a, w_ref[...].astype(BF16), preferred_element_type=F32)
    o_ref[...] = x_ref[...] + gt_ref[...] * _rms(m, g_ref[...])


def _out_proj(a_ctx, a_lat, w, widx, x, mods, gains, layer, tm=512):
    k = a_ctx.shape[1]
    sa, sb = _split_specs(tm, k, 1)
    return pl.pallas_call(
        functools.partial(_out_kernel, tm=tm),
        grid=(TOK // tm,),
        in_specs=[
            sa, sb,
            pl.BlockSpec((None, k, D), lambda i: (widx, 0, 0)),
            pl.BlockSpec((tm, D), lambda i: (i, 0)),
            _mod_spec(layer, 5, tm, 1),
            _gain_spec(layer, 3, 1),
        ],
        out_specs=pl.BlockSpec((tm, D), lambda i: (i, 0)),
        out_shape=jax.ShapeDtypeStruct((TOK, D), F32),
        compiler_params=_cparams(("arbitrary",)),
        name=f"outproj_l{layer}",
    )(a_ctx, a_lat, w, x, mods, gains)


GMLP_GPS = 2
GMLP_SLABS = GMLP_GROUPS // GMLP_GPS
GMLP_COLS = GMLP_GPS * GMLP_GD


def _gelu_exact(x):
    return 0.5 * x * (1.0 + lax.erf(x * (1.0 / math.sqrt(2.0))))


def _gmlp_kernel(x_ref, sh_ref, sc_ref, g2_ref, win_ref, lg_ref, lb_ref, ws_ref, bs_ref, wo_ref,
                 gt_ref, g3_ref, o_ref, h_ref, v_ref, mu_ref, rstd_ref, *, tm):
    j = pl.program_id(1)

    @pl.when(j == 0)
    def _():
        h_ref[...] = _adaln(x_ref[...], g2_ref[...], sh_ref[...], sc_ref[...]).astype(BF16)

    y = _gelu_exact(jnp.dot(h_ref[...], win_ref[...].astype(BF16), preferred_element_type=F32))

    @pl.when(j < GMLP_SLABS)
    def _():
        v_ref[j] = y

    @pl.when(j == GMLP_SLABS)
    def _():
        tot = jnp.zeros((tm, 1), F32)
        for s in range(GMLP_SLABS):
            tot = tot + jnp.sum(v_ref[s], axis=-1, keepdims=True)
        mu = tot * (1.0 / GMLP_HALF)
        sq = jnp.zeros((tm, 1), F32)
        for s in range(GMLP_SLABS):
            vc = v_ref[s] - mu
            sq = sq + jnp.sum(vc * vc, axis=-1, keepdims=True)
        mu_ref[...] = mu
        rstd_ref[...] = lax.rsqrt(sq * (1.0 / GMLP_HALF) + EPS)
        o_ref[...] = jnp.zeros_like(o_ref)

    @pl.when(j >= GMLP_SLABS)
    def _():
        s = j - GMLP_SLABS
        vn = ((v_ref[s] - mu_ref[...]) * rstd_ref[...] * lg_ref[...] + lb_ref[...]).astype(BF16)
        rows = []
        for c in range(tm // CHUNK):
            cols = []
            for gg in range(GMLP_GPS):
                vg = vn[c * CHUNK:(c + 1) * CHUNK, gg * GMLP_GD:(gg + 1) * GMLP_GD]
                cols.append(jnp.dot(ws_ref[gg].astype(BF16), vg, preferred_element_type=F32) + bs_ref[gg])
            rows.append(jnp.concatenate(cols, axis=1))
        sv = jnp.concatenate(rows, axis=0)
        a = (y * sv).astype(BF16)
        o_ref[...] += jnp.dot(a, wo_ref[...].astype(BF16), preferred_element_type=F32)

    @pl.when(j == 2 * GMLP_SLABS - 1)
    def _():
        o_ref[...] = x_ref[...] + gt_ref[...] * _rms(o_ref[...], g3_ref[...])


def _gmlp(x, mods, gains, w_in, ln_g, ln_b, w_s, b_s, w_out, widx, layer, tm=1024):
    ns = GMLP_SLABS

    def u_slab(j):
        return jnp.maximum(j - ns, 0)

    single = pl.Buffered(1)
    return pl.pallas_call(
        functools.partial(_gmlp_kernel, tm=tm),
        grid=(TOK // tm, 2 * ns),
        in_specs=[
            pl.BlockSpec((tm, D), lambda i, j: (i, 0), pipeline_mode=single),
            _mod_spec(layer, 3, tm, 2),
            _mod_spec(layer, 4, tm, 2),
            _gain_spec(layer, 2, 2),
            pl.BlockSpec((None, D, GMLP_COLS), lambda i, j: (widx, 0, jnp.where(j < ns, ns + j, j - ns))),
            pl.BlockSpec((None, 1, GMLP_COLS), lambda i, j: (widx, 0, u_slab(j))),
            pl.BlockSpec((None, 1, GMLP_COLS), lambda i, j: (widx, 0, u_slab(j))),
            pl.BlockSpec((None, GMLP_GPS, CHUNK, CHUNK), lambda i, j: (widx, u_slab(j), 0, 0)),
            pl.BlockSpec((None, GMLP_GPS, CHUNK, 1), lambda i, j: (widx, u_slab(j), 0, 0)),
            pl.BlockSpec((None, GMLP_COLS, D), lambda i, j: (widx, u_slab(j), 0)),
            _mod_spec(layer, 5, tm, 2),
            _gain_spec(layer, 3, 2),
        ],
        out_specs=pl.BlockSpec((tm, D), lambda i, j: (i, 0)),
        out_shape=jax.ShapeDtypeStruct((TOK, D), F32),
        scratch_shapes=[pltpu.VMEM((tm, D), BF16), pltpu.VMEM((ns, tm, GMLP_COLS), F32),
                        pltpu.VMEM((tm, 1), F32), pltpu.VMEM((tm, 1), F32)],
        compiler_params=_cparams(("arbitrary", "arbitrary"), VMEM_LIMIT_GMLP),
        name=f"gmlp_l{layer}",
    )(x, mods, mods, gains, w_in, ln_g, ln_b, w_s, b_s, w_out, mods, gains)


def _rope_tables():
    pos = jnp.arange(LAT_LEN)
    pos_r = (pos // GRID_W).astype(F32)
    pos_c = (pos % GRID_W).astype(F32)
    inv = ROPE_BASE ** (-jnp.arange(ROT_PAIRS, dtype=F32) / ROT_PAIRS)
    ang_r = pos_r[:, None] * inv
    ang_c = pos_c[:, None] * inv
    cos = jnp.concatenate([jnp.cos(ang_r)] * 2 + [jnp.cos(ang_c)] * 2, axis=1)
    sin = jnp.concatenate([-jnp.sin(ang_r), jnp.sin(ang_r), -jnp.sin(ang_c), jnp.sin(ang_c)], axis=1)
    reps = LANES // HEAD_DIM
    return jnp.tile(cos, (1, reps)), jnp.tile(sin, (1, reps))


def _qkv_kernel(x_ref, sh_ref, sc_ref, g_ref, w_ref, cos_ref, sin_ref, o_ref, *, tm):
    i = pl.program_id(0)
    h = _adaln(x_ref[...], g_ref[...], sh_ref[...], sc_ref[...])
    y = jnp.dot(h.astype(BF16), w_ref[...].astype(BF16), preferred_element_type=F32)

    @pl.when(i * tm < CTX_TOK)
    def _():
        o_ref[...] = y

    @pl.when(i * tm >= CTX_TOK)
    def _():
        cos = cos_ref[...]
        sin = sin_ref[...]
        lane = lax.broadcasted_iota(jnp.int32, (tm, LANES), 1)
        first = (lane % (2 * ROT_PAIRS)) < ROT_PAIRS
        for cb in range((Q_DIM + KV_DIM) // LANES):
            t = y[:, cb * LANES:(cb + 1) * LANES]
            partner = jnp.where(first, pltpu.roll(t, LANES - ROT_PAIRS, axis=1),
                                pltpu.roll(t, ROT_PAIRS, axis=1))
            o_ref[:, cb * LANES:(cb + 1) * LANES] = t * cos + partner * sin
        o_ref[:, Q_DIM + KV_DIM:] = y[:, Q_DIM + KV_DIM:]


def _qkv_proj(x, mods, gains, w, widx, layer, tm=512):
    cos, sin = _rope_tables()
    per_seq = LAT_LEN // tm
    tab = pl.BlockSpec((tm, LANES), lambda i: (jnp.maximum(i - CTX_TOK // tm, 0) % per_seq, 0))
    return pl.pallas_call(
        functools.partial(_qkv_kernel, tm=tm),
        grid=(TOK // tm,),
        in_specs=[
            pl.BlockSpec((tm, D), lambda i: (i, 0)),
            _mod_spec(layer, 3, tm, 1),
            _mod_spec(layer, 4, tm, 1),
            _gain_spec(layer, 2, 1),
            pl.BlockSpec((None, D, QKV_DIM), lambda i: (widx, 0, 0)),
            tab, tab,
        ],
        out_specs=pl.BlockSpec((tm, QKV_DIM), lambda i: (i, 0)),
        out_shape=jax.ShapeDtypeStruct((TOK, QKV_DIM), F32),
        compiler_params=_cparams(("arbitrary",)),
        name=f"qkv_l{layer}",
    )(x, mods, mods, gains, w, cos, sin)


def _attend(q4, keys, vals, sink_col, masks):
    logits = []
    for k, mk in zip(keys, masks):
        s = lax.dot_general(q4, k, (((1,), (1,)), ((), ())), preferred_element_type=F32) * ATTN_SCALE
        if mk is not None:
            s = jnp.where(mk, s, NEG_INF)
        logits.append(s)
    mx = None
    for s in logits:
        for cb in range(s.shape[1] // LANES):
            blk = s[:, cb * LANES:(cb + 1) * LANES]
            mx = blk if mx is None else jnp.maximum(mx, blk)
    m = jnp.maximum(sink_col, jnp.max(mx, axis=-1, keepdims=True))
    o = rs = None
    for s, v in zip(logits, vals):
        p = jnp.exp(s - m).astype(BF16)
        pv = jnp.dot(p, v, preferred_element_type=F32)
        ps = jnp.dot(p, jnp.ones((s.shape[1], LANES), BF16), preferred_element_type=F32)
        o = pv if o is None else o + pv
        rs = ps if rs is None else rs + ps
    return o / (rs[:, :HEAD_DIM] + jnp.exp(sink_col - m))


def _attn_ctx_kernel(sink_ref, q_ref, k_ref, v_ref, o_ref):
    rows = CTX_LEN
    outs = [None] * N_Q_HEADS
    for hk in range(N_KV_HEADS):
        k = k_ref[:, hk * HEAD_DIM:(hk + 1) * HEAD_DIM].astype(BF16)
        v = v_ref[:, hk * HEAD_DIM:(hk + 1) * HEAD_DIM].astype(BF16)
        heads = [hk * Q_PER_KV + g for g in range(Q_PER_KV)]
        q4 = jnp.concatenate([q_ref[:, h * HEAD_DIM:(h + 1) * HEAD_DIM] for h in heads], axis=0).astype(BF16)
        sink_col = jnp.concatenate([jnp.full((rows, 1), sink_ref[h], F32) for h in heads], axis=0)
        o4 = _attend(q4, [k], [v], sink_col, [None])
        for g, h in enumerate(heads):
            outs[h] = o4[g * rows:(g + 1) * rows]
    o_ref[...] = jnp.concatenate(outs, axis=1).astype(o_ref.dtype)


def _attn_lat_kernel(sink_ref, q_ref, kp_ref, kc_ref, kn_ref, vp_ref, vc_ref, vn_ref, ck_ref, cv_ref,
                     o_ref, *, n_blk):
    qi = pl.program_id(1)
    rows = CHUNK
    r4 = Q_PER_KV * rows
    a = lax.broadcasted_iota(jnp.int32, (r4, CHUNK), 0) % rows
    s = lax.broadcasted_iota(jnp.int32, (r4, CHUNK), 1)
    mask_prev = (s >= a) & (qi > 0)
    mask_next = (s <= a) & (qi < n_blk - 1)
    outs = [None] * N_Q_HEADS
    for hk in range(N_KV_HEADS):
        sl = slice(hk * HEAD_DIM, (hk + 1) * HEAD_DIM)
        keys = [kp_ref[:, sl].astype(BF16), kc_ref[:, sl].astype(BF16), kn_ref[:, sl].astype(BF16),
                ck_ref[:, sl].astype(BF16)]
        vals = [vp_ref[:, sl].astype(BF16), vc_ref[:, sl].astype(BF16), vn_ref[:, sl].astype(BF16),
                cv_ref[:, sl].astype(BF16)]
        heads = [hk * Q_PER_KV + g for g in range(Q_PER_KV)]
        q4 = jnp.concatenate([q_ref[:, h * HEAD_DIM:(h + 1) * HEAD_DIM] for h in heads], axis=0).astype(BF16)
        sink_col = jnp.concatenate([jnp.full((rows, 1), sink_ref[h], F32) for h in heads], axis=0)
        o4 = _attend(q4, keys, vals, sink_col, [mask_prev, None, mask_next, None])
        for g, h in enumerate(heads):
            outs[h] = o4[g * rows:(g + 1) * rows]
    o_ref[...] = jnp.concatenate(outs, axis=1).astype(o_ref.dtype)


def _attention(qkv, sinks, ctx_k, ctx_v):
    smem = pl.BlockSpec(memory_space=pltpu.SMEM)
    kcol, vcol = Q_DIM // KV_DIM, Q_DIM // KV_DIM + 1
    ctx_blk = CTX_LEN
    o_ctx = pl.pallas_call(
        _attn_ctx_kernel,
        grid=(N_CTX_SEQ,),
        in_specs=[
            smem,
            pl.BlockSpec((ctx_blk, Q_DIM), lambda b: (b, 0)),
            pl.BlockSpec((ctx_blk, KV_DIM), lambda b: (b, kcol)),
            pl.BlockSpec((ctx_blk, KV_DIM), lambda b: (b, vcol)),
        ],
        out_specs=pl.BlockSpec((ctx_blk, Q_DIM), lambda b: (b, 0)),
        out_shape=jax.ShapeDtypeStruct((CTX_TOK, Q_DIM), BF16),
        compiler_params=_cparams(("arbitrary",)),
        name="attn_ctx",
    )(sinks, qkv, qkv, qkv)

    n_blk = LAT_LEN // CHUNK
    base = CTX_TOK // CHUNK

    def rb(b, qi, off):
        return base + b * n_blk + jnp.clip(qi + off, 0, n_blk - 1)

    def kv_spec(col, off):
        return pl.BlockSpec((CHUNK, KV_DIM), lambda b, qi: (rb(b, qi, off), col))

    o_lat = pl.pallas_call(
        functools.partial(_attn_lat_kernel, n_blk=n_blk),
        grid=(N_LAT_SEQ, n_blk),
        in_specs=[
            smem,
            pl.BlockSpec((CHUNK, Q_DIM), lambda b, qi: (rb(b, qi, 0), 0)),
            kv_spec(kcol, -1), kv_spec(kcol, 0), kv_spec(kcol, 1),
            kv_spec(vcol, -1), kv_spec(vcol, 0), kv_spec(vcol, 1),
            pl.BlockSpec((None, CTX_LEN, KV_DIM), lambda b, qi: (b, 0, 0)),
            pl.BlockSpec((None, CTX_LEN, KV_DIM), lambda b, qi: (b, 0, 0)),
        ],
        out_specs=pl.BlockSpec((CHUNK, Q_DIM), lambda b, qi: (b * n_blk + qi, 0)),
        out_shape=jax.ShapeDtypeStruct((LAT_TOK, Q_DIM), BF16),
        compiler_params=_cparams(("arbitrary", "arbitrary")),
        name="attn_lat",
    )(sinks, qkv, qkv, qkv, qkv, qkv, qkv, qkv, ctx_k, ctx_v)
    return o_ctx, o_lat


def _softplus(x):
    return jnp.maximum(x, 0.0) + jnp.log1p(jnp.exp(-jnp.abs(x)))


def _ssm_in_kernel(x_ref, sh_ref, sc_ref, g_ref, w_ref, wdt_ref, bdt_ref, zx_ref, dt_ref, h_ref, *, n_main):
    j = pl.program_id(1)

    @pl.when(j == 0)
    def _():
        h_ref[...] = _adaln(x_ref[...], g_ref[...], sh_ref[...], sc_ref[...]).astype(BF16)

    @pl.when(j < n_main)
    def _():
        zx_ref[...] = jnp.dot(h_ref[...], w_ref[...].astype(BF16), preferred_element_type=F32)

    @pl.when(j == n_main)
    def _():
        y = jnp.dot(h_ref[...], wdt_ref[...].astype(BF16), preferred_element_type=F32)
        dt_ref[...] = _softplus(y + bdt_ref[...])


def _ssm_in(x, mods, gains, w, widx, w_dt, b_dt, layer, tm=1024, tn=1024):
    n_main = SSM_MAIN // tn
    return pl.pallas_call(
        functools.partial(_ssm_in_kernel, n_main=n_main),
        grid=(TOK // tm, n_main + 1),
        in_specs=[
            pl.BlockSpec((tm, D), lambda i, j: (i, 0)),
            _mod_spec(layer, 3, tm, 2),
            _mod_spec(layer, 4, tm, 2),
            _gain_spec(layer, 2, 2),
            pl.BlockSpec((None, D, tn), lambda i, j: (widx, 0, jnp.minimum(j, n_main - 1))),
            pl.BlockSpec((D, LANES), lambda i, j: (0, 0)),
            pl.BlockSpec((1, LANES), lambda i, j: (0, 0)),
        ],
        out_specs=[
            pl.BlockSpec((tm, tn), lambda i, j: (i, jnp.minimum(j, n_main - 1))),
            pl.BlockSpec((tm, LANES), lambda i, j: (i, 0)),
        ],
        out_shape=[jax.ShapeDtypeStruct((TOK, SSM_MAIN), F32), jax.ShapeDtypeStruct((TOK, LANES), F32)],
        scratch_shapes=[pltpu.VMEM((tm, D), BF16)],
        compiler_params=_cparams(("arbitrary", "arbitrary")),
        name=f"ssm_in_l{layer}",
    )(x, mods, mods, gains, w, w_dt, b_dt)


def _split3(q):
    hi = q.astype(BF16)
    r1 = q - hi.astype(F32)
    mid = r1.astype(BF16)
    lo = (r1 - mid.astype(F32)).astype(BF16)
    return hi, mid, lo


def _spread(q, r_ref):
    hi, mid, lo = _split3(q[:, :SSM_HEADS])
    return jnp.dot(jnp.concatenate([hi, mid, lo], axis=1), r_ref[...], preferred_element_type=F32)


def _spread_consts():
    r = jnp.arange(3 * SSM_HEADS) % SSM_HEADS
    r1 = (jnp.arange(SSM_HEADS * SSM_P)[None, :] // SSM_P == r[:, None]).astype(BF16)
    r2 = (jnp.arange(SSM_HEADS * CHUNK)[None, :] // CHUNK == r[:, None]).astype(BF16)
    return r1, r2


def _ssd_kernel(*refs, nc, has_h0, emit_state):
    (z_ref, x_ref, xp_ref, xn_ref, bc_ref, bcp_ref, bcn_ref, dt_ref,
     cw_ref, cb_ref, a_ref, dsk_ref, ng_ref, r1_ref, r2_ref) = refs[:15]
    pos = 15
    h0_ref = None
    if has_h0:
        h0_ref = refs[pos]
        pos += 1
    y_ref = refs[pos]
    pos += 1
    st_ref = None
    if emit_state:
        st_ref = refs[pos]
        pos += 1
    hs_ref, yf_ref, xs_ref, bcs_ref = refs[pos:pos + 4]

    s = pl.program_id(1)
    bwd = s >= nc
    c = jnp.where(bwd, 2 * nc - 1 - s, s)
    L = CHUNK
    off = pl.multiple_of(c * L, L)

    @pl.when((s == 0) | (s == nc))
    def _():
        if has_h0:
            @pl.when(s == 0)
            def _():
                hs_ref[...] = h0_ref[0]

            @pl.when(s == nc)
            def _():
                hs_ref[...] = h0_ref[1]
        else:
            hs_ref[...] = jnp.zeros_like(hs_ref)

    @pl.when(jnp.logical_not(bwd))
    def _():
        row = lax.broadcasted_iota(jnp.int32, (L, 1), 0)

        def conv_silu(cur, prev_blk, next_blk, lo, hi):
            w = cw_ref[:, lo:hi]
            prev_row = jnp.where(c > 0, prev_blk[HALO - 1:HALO, :], 0.0)
            next_row = jnp.where(c < nc - 1, next_blk[0:1, :], 0.0)
            up = jnp.where(row == 0, prev_row, pltpu.roll(cur, 1, axis=0))
            dn = jnp.where(row == L - 1, next_row, pltpu.roll(cur, L - 1, axis=0))
            return _silu(up * w[0:1] + cur * w[1:2] + dn * w[2:3] + cb_ref[:, lo:hi])

        xs_ref[pl.ds(off, L), :] = conv_silu(x_ref[...], xp_ref[...], xn_ref[...], 0, SSM_INNER)
        bcs_ref[pl.ds(off, L), :] = conv_silu(bc_ref[...], bcp_ref[...], bcn_ref[...],
                                              SSM_INNER, SSM_CONV_DIM).astype(BF16)

    xs = xs_ref[pl.ds(off, L), :]
    bcb = bcs_ref[pl.ds(off, L), :]

    dt_all = dt_ref[...]
    dt = jnp.where(bwd, pltpu.roll(dt_all, LANES - SSM_HEADS, axis=1), dt_all)
    a_row = jnp.where(bwd, a_ref[1:2, :], a_ref[0:1, :])
    ii = lax.broadcasted_iota(jnp.int32, (L, L), 0)
    jj = lax.broadcasted_iota(jnp.int32, (L, L), 1)
    tri = jnp.where(bwd, jj - ii, ii - jj) >= 0
    tri_b = jnp.where(tri, 1.0, 0.0).astype(BF16)
    neg_mask = jnp.where(tri, 0.0, -jnp.inf)
    hi_p, mid_p, lo_p = _split3(dt * a_row)
    acum = (jnp.dot(tri_b, hi_p, preferred_element_type=F32)
            + jnp.dot(tri_b, mid_p, preferred_element_type=F32)
            + jnp.dot(tri_b, lo_p, preferred_element_type=F32))
    total = jnp.where(bwd, acum[0:1, :], acum[L - 1:L, :])
    acum_t = acum.T
    cdec_t = jnp.exp(jnp.where(bwd, acum_t[:, 0:1], acum_t[:, L - 1:L]))

    e_acum = _spread(acum, r2_ref)
    e_dt = _spread(dt, r1_ref)
    e_end = _spread(dt * jnp.exp(total - acum), r1_ref)
    e_in = _spread(jnp.exp(acum), r1_ref)
    xdt = (xs * e_dt).astype(BF16)
    xw = (xs * e_end).astype(BF16)
    left = lax.broadcasted_iota(jnp.int32, (L, LANES), 1) < SSM_P

    y_parts = []
    for g in range(SSM_GROUPS):
        bm = bcb[:, g * SSM_STATE:(g + 1) * SSM_STATE]
        cm = bcb[:, SSM_GN + g * SSM_STATE:SSM_GN + (g + 1) * SSM_STATE]
        cb = lax.dot_general(cm, bm, (((1,), (1,)), ((), ())), preferred_element_type=F32)
        gsl = slice(g * SSM_HG * SSM_P, (g + 1) * SSM_HG * SSM_P)
        hprev = hs_ref[gsl, :]
        y_off = lax.dot_general(cm, hprev.astype(BF16), (((1,), (1,)), ((), ())),
                                preferred_element_type=F32)
        y_pairs = []
        for pr in range(SSM_HG // 2):
            wm = []
            for h in (g * SSM_HG + 2 * pr, g * SSM_HG + 2 * pr + 1):
                seg = e_acum[:, h * L:(h + 1) * L] - acum_t[h:h + 1, :] + neg_mask
                wm.append((cb * jnp.exp(seg)).astype(BF16))
            pair = xdt[:, (g * SSM_HG + 2 * pr) * SSM_P:(g * SSM_HG + 2 * pr + 2) * SSM_P]
            rhs = jnp.concatenate([jnp.where(left, pair, jnp.zeros_like(pair)),
                                   jnp.where(left, jnp.zeros_like(pair), pair)], axis=0)
            y_pairs.append(jnp.dot(jnp.concatenate(wm, axis=1), rhs, preferred_element_type=F32))
        y_parts.append(jnp.concatenate(y_pairs, axis=1) + y_off * e_in[:, gsl])
        st = lax.dot_general(xw[:, gsl], bm, (((0,), (0,)), ((), ())), preferred_element_type=F32)
        dec = jnp.concatenate([jnp.broadcast_to(cdec_t[g * SSM_HG + k:g * SSM_HG + k + 1, :], (SSM_P, 1))
                               for k in range(SSM_HG)], axis=0)
        hs_ref[gsl, :] = dec * hprev + st
    y = jnp.concatenate(y_parts, axis=1)

    @pl.when(jnp.logical_not(bwd))
    def _():
        yf_ref[pl.ds(off, L), :] = y + dsk_ref[...] * xs

    @pl.when(bwd)
    def _():
        yt = (y + yf_ref[pl.ds(off, L), :]) * _silu(z_ref[...])
        y_ref[...] = _rms(yt, ng_ref[...]).astype(y_ref.dtype)

    if emit_state:
        @pl.when(s == nc - 1)
        def _():
            st_ref[0] = hs_ref[...]

        @pl.when(s == 2 * nc - 1)
        def _():
            st_ref[1] = hs_ref[...]


def _ssd(zx, dt, conv_w, conv_b, a_pad, dsk, norm_g, r1, r2, h0, *, seq0, n_seq, seq_len, emit_state):
    nc = seq_len // CHUNK
    hb = CHUNK // HALO
    n_rows_halo = TOK // HALO
    chunk0 = seq0 // CHUNK

    def cidx(b, s):
        return chunk0 + b * nc + jnp.where(s >= nc, 2 * nc - 1 - s, s)

    def fidx(b, s):
        return chunk0 + b * nc + jnp.minimum(s, nc - 1)

    def zidx(b, s):
        return chunk0 + b * nc + jnp.where(s >= nc, 2 * nc - 1 - s, nc - 1)

    def prev_halo(b, s):
        return jnp.maximum(fidx(b, s) * hb - 1, 0)

    def next_halo(b, s):
        return jnp.minimum(fidx(b, s) * hb + hb, n_rows_halo - 1)

    xcol = 1
    bccol = (2 * SSM_INNER) // (2 * SSM_GN)
    const = lambda b, s: (0, 0)
    in_specs = [
        pl.BlockSpec((CHUNK, SSM_INNER), lambda b, s: (zidx(b, s), 0)),
        pl.BlockSpec((CHUNK, SSM_INNER), lambda b, s: (fidx(b, s), xcol)),
        pl.BlockSpec((HALO, SSM_INNER), lambda b, s: (prev_halo(b, s), xcol)),
        pl.BlockSpec((HALO, SSM_INNER), lambda b, s: (next_halo(b, s), xcol)),
        pl.BlockSpec((CHUNK, 2 * SSM_GN), lambda b, s: (fidx(b, s), bccol)),
        pl.BlockSpec((HALO, 2 * SSM_GN), lambda b, s: (prev_halo(b, s), bccol)),
        pl.BlockSpec((HALO, 2 * SSM_GN), lambda b, s: (next_halo(b, s), bccol)),
        pl.BlockSpec((CHUNK, LANES), lambda b, s: (cidx(b, s), 0)),
        pl.BlockSpec((3, SSM_CONV_DIM), const),
        pl.BlockSpec((1, SSM_CONV_DIM), const),
        pl.BlockSpec((2, LANES), const),
        pl.BlockSpec((1, SSM_INNER), const),
        pl.BlockSpec((1, SSM_INNER), const),
        pl.BlockSpec(r1.shape, const),
        pl.BlockSpec(r2.shape, const),
    ]
    args = [zx, zx, zx, zx, zx, zx, zx, dt, conv_w, conv_b, a_pad, dsk, norm_g, r1, r2]
    if h0 is not None:
        in_specs.append(pl.BlockSpec((None, 2, SSM_INNER, SSM_STATE), lambda b, s: (b, 0, 0, 0)))
        args.append(h0)
    out_specs = [pl.BlockSpec((CHUNK, SSM_INNER), lambda b, s: (zidx(b, s) - chunk0, 0))]
    out_shape = [jax.ShapeDtypeStruct((n_seq * seq_len, SSM_INNER), BF16)]
    if emit_state:
        out_specs.append(pl.BlockSpec((None, 2, SSM_INNER, SSM_STATE), lambda b, s: (b, 0, 0, 0)))
        out_shape.append(jax.ShapeDtypeStruct((n_seq, 2, SSM_INNER, SSM_STATE), F32))
    return pl.pallas_call(
        functools.partial(_ssd_kernel, nc=nc, has_h0=h0 is not None, emit_state=emit_state),
        grid=(n_seq, 2 * nc),
        in_specs=in_specs,
        out_specs=out_specs,
        out_shape=out_shape,
        scratch_shapes=[pltpu.VMEM((SSM_INNER, SSM_STATE), F32), pltpu.VMEM((seq_len, SSM_INNER), F32),
                        pltpu.VMEM((seq_len, SSM_INNER), F32), pltpu.VMEM((seq_len, 2 * SSM_GN), BF16)],
        compiler_params=_cparams(("arbitrary", "arbitrary")),
        name=f"ssd_{seq_len}",
    )(*args)


def kernel(x_prompt, x_sample, cache_k, cache_v, state_ssm, c, c_ctx, w_mod, b_mod, norm_g, ffn_in, ffn_out,
           gmlp_in, gmlp_ln_g, gmlp_ln_b, gmlp_ws, gmlp_bs, gmlp_out, attn_qkv, attn_sink, attn_out,
           ssm_in, ssm_conv_w, ssm_conv_b, ssm_dt_bias, ssm_a_log, ssm_d, ssm_norm, ssm_out):
    cond = jnp.concatenate([c_ctx[None, :], c, jnp.zeros((MOD_ROWS - 1 - N_LAT_SEQ, D), F32)], axis=0)
    mods = _modulation(cond, w_mod, b_mod).reshape(DEPTH * MOD_ROWS * N_MOD, 1, D)
    gains = norm_g.reshape(DEPTH * 6, 1, D)

    x = (x_prompt.reshape(CTX_TOK, D), x_sample.reshape(LAT_TOK, D))
    new_k = new_v = new_s = None
    for layer in range(DEPTH):
        x = _ffn(x, mods, gains, ffn_in, ffn_out, layer, 0, split_in=layer == 0)
        kind, j = layer % 3, layer // 3
        if kind == 0:
            x = _gmlp(x, mods, gains, gmlp_in, gmlp_ln_g[:, None, :], gmlp_ln_b[:, None, :], gmlp_ws,
                      gmlp_bs[..., None], gmlp_out, j, layer)
        elif kind == 1:
            qkv = _qkv_proj(x, mods, gains, attn_qkv, j, layer)
            kv = qkv[:CTX_TOK, Q_DIM:].reshape(N_CTX_SEQ, CTX_LEN, 2, N_KV_HEADS, HEAD_DIM)
            new_k, new_v = kv[:, :, 0], kv[:, :, 1]
            o_ctx, o_lat = _attention(qkv, attn_sink[j],
                                      cache_k[:, j].reshape(N_LAT_SEQ, CTX_LEN, KV_DIM),
                                      cache_v[:, j].reshape(N_LAT_SEQ, CTX_LEN, KV_DIM))
            x = _out_proj(o_ctx, o_lat, attn_out, j, x, mods, gains, layer)
        else:
            pad = LANES - 2 * SSM_HEADS
            w_dt = jnp.pad(ssm_in[j][:, SSM_MAIN:], ((0, 0), (0, pad)))
            b_dt = jnp.pad(ssm_dt_bias[j].reshape(1, 2 * SSM_HEADS), ((0, 0), (0, pad)))
            zx, dt = _ssm_in(x, mods, gains, ssm_in, j, w_dt, b_dt, layer)
            a_pad = jnp.pad(-jnp.exp(ssm_a_log[j]), ((0, 0), (0, LANES - SSM_HEADS)))
            dsk = jnp.repeat(ssm_d[j], SSM_P)[None, :]
            ng = ssm_norm[j][None, :]
            cw, cb = ssm_conv_w[j], ssm_conv_b[j][None, :]
            r1, r2 = _spread_consts()
            y_ctx, st = _ssd(zx, dt, cw, cb, a_pad, dsk, ng, r1, r2, None,
                             seq0=0, n_seq=N_CTX_SEQ, seq_len=CTX_LEN, emit_state=True)
            (y_lat,) = _ssd(zx, dt, cw, cb, a_pad, dsk, ng, r1, r2,
                            state_ssm[:, j].reshape(N_LAT_SEQ, 2, SSM_INNER, SSM_STATE),
                            seq0=CTX_TOK, n_seq=N_LAT_SEQ, seq_len=LAT_LEN, emit_state=False)
            new_s = st.reshape(N_CTX_SEQ, 2, SSM_HEADS, SSM_P, SSM_STATE)
            x = _out_proj(y_ctx, y_lat, ssm_out, j, x, mods, gains, layer)
        x = _ffn(x, mods, gains, ffn_in, ffn_out, layer, 1, split_out=layer == DEPTH - 1)

    y_prompt = x[0].reshape(N_CTX_SEQ, CTX_LEN, D)
    y_sample = x[1].reshape(N_LAT_SEQ, LAT_LEN, D)
    return (y_prompt, y_sample, new_k[:, None], new_v[:, None], new_s[:, None])
```

```python
import functools
import math

import jax
import jax.numpy as jnp
from jax import lax
from jax.experimental import pallas as pl
from jax.experimental.pallas import tpu as pltpu

F32 = jnp.float32
BF16 = jnp.bfloat16

D = 1024
N_CTX_SEQ, CTX_LEN = 16, 256
N_LAT_SEQ, LAT_LEN = 4, 1024
CTX_TOK = N_CTX_SEQ * CTX_LEN
LAT_TOK = N_LAT_SEQ * LAT_LEN
TOK = CTX_TOK + LAT_TOK
DEPTH = 4
N_MOD = 9
MOD_ROWS = 8
D_FF = 2816
EPS = 1e-6
GRID_W = 64
GMLP_HALF = 3 * D
GMLP_GROUPS = 8
GMLP_GD = GMLP_HALF // GMLP_GROUPS
CHUNK = 128
HEAD_DIM = 64
N_Q_HEADS = 16
N_KV_HEADS = 4
Q_PER_KV = 4
Q_DIM = N_Q_HEADS * HEAD_DIM
KV_DIM = N_KV_HEADS * HEAD_DIM
QKV_DIM = Q_DIM + 2 * KV_DIM
ATTN_SCALE = HEAD_DIM ** -0.5
ROPE_BASE = 10000.0
ROT_PAIRS = HEAD_DIM // 4
NEG_INF = -1e30
SSM_INNER = 2 * D
SSM_HEADS = 32
SSM_P = 64
SSM_GROUPS = 4
SSM_HG = SSM_HEADS // SSM_GROUPS
SSM_STATE = 128
SSM_GN = SSM_GROUPS * SSM_STATE
SSM_CONV_DIM = SSM_INNER + 2 * SSM_GN
SSM_MAIN = SSM_INNER + SSM_CONV_DIM
LANES = 128
HALO = 8

VMEM_LIMIT = 56 * 1024 * 1024
VMEM_LIMIT_BIG = 60 * 1024 * 1024


def _cparams(sem, limit=VMEM_LIMIT):
    return pltpu.CompilerParams(dimension_semantics=sem, vmem_limit_bytes=limit)


def _sigmoid(x):
    return 1.0 / (1.0 + jnp.exp(-x))


def _silu(x):
    return x * _sigmoid(x)


def _rms(x, g):
    return x * lax.rsqrt(jnp.mean(x * x, axis=-1, keepdims=True) + EPS) * g


def _adaln(x, g, shift, scale):
    return _rms(x, g) * (1.0 + scale) + shift


def _mod_row(i, tm):
    t0 = i * tm
    return jnp.where(t0 < CTX_TOK, 0, 1 + (t0 - CTX_TOK) // LAT_LEN)


def _mod_spec(layer, k, tm, grid_rank):
    base = layer * MOD_ROWS * N_MOD + k
    if grid_rank == 1:
        return pl.BlockSpec((None, 1, D), lambda i: (base + _mod_row(i, tm) * N_MOD, 0, 0))
    return pl.BlockSpec((None, 1, D), lambda i, j: (base + _mod_row(i, tm) * N_MOD, 0, 0))


def _gain_spec(layer, k, grid_rank):
    idx = layer * 6 + k
    if grid_rank == 1:
        return pl.BlockSpec((None, 1, D), lambda i: (idx, 0, 0))
    return pl.BlockSpec((None, 1, D), lambda i, j: (idx, 0, 0))


def _split_specs(tm, width, grid_rank):
    na = CTX_TOK // tm
    if grid_rank == 1:
        return (pl.BlockSpec((tm, width), lambda i: (jnp.minimum(i, na - 1), 0)),
                pl.BlockSpec((tm, width), lambda i: (jnp.maximum(i - na, 0), 0)))
    return (pl.BlockSpec((tm, width), lambda i, j: (jnp.minimum(i, na - 1), 0)),
            pl.BlockSpec((tm, width), lambda i, j: (jnp.maximum(i - na, 0), 0)))


def _mod_kernel(cond_ref, w_ref, b_ref, o_ref):
    s = _silu(cond_ref[...]).astype(BF16)
    o_ref[...] = jnp.dot(s, w_ref[...].astype(BF16), preferred_element_type=F32) + b_ref[...]


def _modulation(cond, w_mod, b_mod):
    tn = 2304
    n = N_MOD * D
    return pl.pallas_call(
        _mod_kernel,
        grid=(DEPTH, n // tn),
        in_specs=[
            pl.BlockSpec((MOD_ROWS, D), lambda l, j: (0, 0)),
            pl.BlockSpec((None, D, tn), lambda l, j: (l, 0, j)),
            pl.BlockSpec((None, 1, tn), lambda l, j: (l, 0, j)),
        ],
        out_specs=pl.BlockSpec((None, MOD_ROWS, tn), lambda l, j: (l, 0, j)),
        out_shape=jax.ShapeDtypeStruct((DEPTH, MOD_ROWS, n), F32),
        compiler_params=_cparams(("arbitrary", "arbitrary")),
        name="modulation",
    )(cond, w_mod, b_mod.reshape(DEPTH, 1, n))


FFN_TF = 256
FFN_SLABS = D_FF // FFN_TF
FFN_SPS = 3
FFN_STEPS = -(-FFN_SLABS // FFN_SPS)


def _ffn_kernel(*refs, tm, split_in, split_out):
    n_x = 2 if split_in else 1
    n_o = 2 if split_out else 1
    x_refs = refs[:n_x]
    sh_ref, sc_ref, gt_ref, g0_ref, g1_ref = refs[n_x:n_x + 5]
    w0 = n_x + 5
    wg = refs[w0:w0 + FFN_SPS]
    wu = refs[w0 + FFN_SPS:w0 + 2 * FFN_SPS]
    wo = refs[w0 + 2 * FFN_SPS:w0 + 3 * FFN_SPS]
    o_refs = refs[w0 + 3 * FFN_SPS:w0 + 3 * FFN_SPS + n_o]
    h_ref, acc_ref = refs[w0 + 3 * FFN_SPS + n_o:]
    i = pl.program_id(0)
    j = pl.program_id(1)
    is_ctx = i * tm < CTX_TOK
    last = FFN_STEPS - 1

    def load_x():
        if split_in:
            return jnp.where(is_ctx, x_refs[0][...], x_refs[1][...])
        return x_refs[0][...]

    def swiglu_slabs(h, n):
        acts = []
        for k in range(n):
            g = jnp.dot(h, wg[k][...].astype(BF16), preferred_element_type=F32)
            u = jnp.dot(h, wu[k][...].astype(BF16), preferred_element_type=F32)
            acts.append((_silu(g) * u).astype(BF16))
        w = jnp.concatenate([wo[k][...].astype(BF16) for k in range(n)], axis=0)
        return jnp.dot(jnp.concatenate(acts, axis=1), w, preferred_element_type=F32)

    @pl.when(j == 0)
    def _():
        h = _adaln(load_x(), g0_ref[...], sh_ref[...], sc_ref[...]).astype(BF16)
        h_ref[...] = h
        acc_ref[...] = swiglu_slabs(h, FFN_SPS)

    @pl.when((j > 0) & (j < last))
    def _():
        acc_ref[...] += swiglu_slabs(h_ref[...], FFN_SPS)

    @pl.when(j == last)
    def _():
        f = acc_ref[...] + swiglu_slabs(h_ref[...], FFN_SLABS - last * FFN_SPS)
        res = load_x() + (0.5 * gt_ref[...]) * _rms(f, g1_ref[...])
        if split_out:
            @pl.when(is_ctx)
            def _():
                o_refs[0][...] = res

            @pl.when(jnp.logical_not(is_ctx))
            def _():
                o_refs[1][...] = res
        else:
            o_refs[0][...] = res


def _ffn(x, mods, gains, ffn_in, ffn_out, layer, which, split_in=False, split_out=False):
    tm, tf = 1024, FFN_TF
    k0 = 0 if which == 0 else 6
    gi = 0 if which == 0 else 4
    single = pl.Buffered(1)
    if split_in:
        sa, sb = _split_specs(tm, D, 2)
        x_specs = [pl.BlockSpec(sa.block_shape, sa.index_map, pipeline_mode=single),
                   pl.BlockSpec(sb.block_shape, sb.index_map, pipeline_mode=single)]
    else:
        x_specs = [pl.BlockSpec((tm, D), lambda i, j: (i, 0))]
    x_args = list(x) if split_in else [x]
    if split_out:
        sa, sb = _split_specs(tm, D, 2)
        out_specs = [pl.BlockSpec(sa.block_shape, sa.index_map, pipeline_mode=single),
                     pl.BlockSpec(sb.block_shape, sb.index_map, pipeline_mode=single)]
        out_shape = [jax.ShapeDtypeStruct((CTX_TOK, D), F32), jax.ShapeDtypeStruct((LAT_TOK, D), F32)]
    else:
        out_specs = pl.BlockSpec((tm, D), lambda i, j: (i, 0))
        out_shape = jax.ShapeDtypeStruct((TOK, D), F32)

    def slab(j, k):
        return jnp.minimum(j * FFN_SPS + k, FFN_SLABS - 1)

    def spec_in(k, col0):
        return pl.BlockSpec((None, None, D, tf), lambda i, j: (layer, which, 0, col0 + slab(j, k)))

    def spec_out(k):
        return pl.BlockSpec((None, None, tf, D), lambda i, j: (layer, which, slab(j, k), 0))

    w_specs = ([spec_in(k, 0) for k in range(FFN_SPS)] + [spec_in(k, FFN_SLABS) for k in range(FFN_SPS)]
               + [spec_out(k) for k in range(FFN_SPS)])
    return pl.pallas_call(
        functools.partial(_ffn_kernel, tm=tm, split_in=split_in, split_out=split_out),
        grid=(TOK // tm, FFN_STEPS),
        in_specs=x_specs + [
            _mod_spec(layer, k0 + 0, tm, 2),
            _mod_spec(layer, k0 + 1, tm, 2),
            _mod_spec(layer, k0 + 2, tm, 2),
            _gain_spec(layer, gi, 2),
            _gain_spec(layer, gi + 1, 2),
        ] + w_specs,
        out_specs=out_specs,
        out_shape=out_shape,
        scratch_shapes=[pltpu.VMEM((tm, D), BF16), pltpu.VMEM((tm, D), F32)],
        compiler_params=_cparams(("arbitrary", "arbitrary"), VMEM_LIMIT_BIG),
        name=f"ffn_l{layer}_{which}",
    )(*x_args, mods, mods, mods, gains, gains, *([ffn_in] * (2 * FFN_SPS)), *([ffn_out] * FFN_SPS))


def _out_kernel(aa_ref, ab_ref, w_ref, x_ref, gt_ref, g_ref, o_ref, *, tm):
    a = jnp.where(pl.program_id(0) * tm < CTX_TOK, aa_ref[...], ab_ref[...])
    m = jnp.dot(a, w_ref[...].astype(BF16), preferred_element_type=F32)
    o_ref[...] = x_ref[...] + gt_ref[...] * _rms(m, g_ref[...])


def _out_proj(a_ctx, a_lat, w, widx, x, mods, gains, layer, tm=512):
    k = a_ctx.shape[1]
    sa, sb = _split_specs(tm, k, 1)
    return pl.pallas_call(
        functools.partial(_out_kernel, tm=tm),
        grid=(TOK // tm,),
        in_specs=[
            sa, sb,
            pl.BlockSpec((None, k, D), lambda i: (widx, 0, 0)),
            pl.BlockSpec((tm, D), lambda i: (i, 0)),
            _mod_spec(layer, 5, tm, 1),
            _gain_spec(layer, 3, 1),
        ],
        out_specs=pl.BlockSpec((tm, D), lambda i: (i, 0)),
        out_shape=jax.ShapeDtypeStruct((TOK, D), F32),
        compiler_params=_cparams(("arbitrary",)),
        name=f"outproj_l{layer}",
    )(a_ctx, a_lat, w, x, mods, gains)


GMLP_GPS = 2
GMLP_SLABS = GMLP_GROUPS // GMLP_GPS
GMLP_COLS = GMLP_GPS * GMLP_GD


def _gelu_exact(x):
    return 0.5 * x * (1.0 + lax.erf(x * (1.0 / math.sqrt(2.0))))


def _gmlp_kernel(x_ref, sh_ref, sc_ref, g2_ref, win_ref, lg_ref, lb_ref, ws_ref, bs_ref, wo_ref,
                 gt_ref, g3_ref, o_ref, h_ref, v_ref, mu_ref, rstd_ref, *, tm):
    j = pl.program_id(1)

    @pl.when(j == 0)
    def _():
        h_ref[...] = _adaln(x_ref[...], g2_ref[...], sh_ref[...], sc_ref[...]).astype(BF16)

    y = _gelu_exact(jnp.dot(h_ref[...], win_ref[...].astype(BF16), preferred_element_type=F32))

    @pl.when(j < GMLP_SLABS)
    def _():
        v_ref[j] = y

    @pl.when(j == GMLP_SLABS)
    def _():
        def fold(fn):
            acc = None
            for s in range(GMLP_SLABS):
                v = fn(v_ref[s])
                for cb in range(GMLP_COLS // LANES):
                    blk = v[:, cb * LANES:(cb + 1) * LANES]
                    acc = blk if acc is None else acc + blk
            return jnp.sum(acc, axis=-1, keepdims=True)

        mu = fold(lambda v: v) * (1.0 / GMLP_HALF)
        sq = fold(lambda v: (v - mu) * (v - mu))
        mu_ref[...] = mu
        rstd_ref[...] = lax.rsqrt(sq * (1.0 / GMLP_HALF) + EPS)
        o_ref[...] = jnp.zeros_like(o_ref)

    @pl.when(j >= GMLP_SLABS)
    def _():
        s = j - GMLP_SLABS
        vn = ((v_ref[s] - mu_ref[...]) * rstd_ref[...] * lg_ref[...] + lb_ref[...]).astype(BF16)
        rows = []
        for c in range(tm // CHUNK):
            cols = []
            for gg in range(GMLP_GPS):
                vg = vn[c * CHUNK:(c + 1) * CHUNK, gg * GMLP_GD:(gg + 1) * GMLP_GD]
                cols.append(jnp.dot(ws_ref[gg].astype(BF16), vg, preferred_element_type=F32) + bs_ref[gg])
            rows.append(jnp.concatenate(cols, axis=1))
        sv = jnp.concatenate(rows, axis=0)
        a = (y * sv).astype(BF16)
        o_ref[...] += jnp.dot(a, wo_ref[...].astype(BF16), preferred_element_type=F32)

    @pl.when(j == 2 * GMLP_SLABS - 1)
    def _():
        o_ref[...] = x_ref[...] + gt_ref[...] * _rms(o_ref[...], g3_ref[...])


def _gmlp(x, mods, gains, w_in, ln_g, ln_b, w_s, b_s, w_out, widx, layer, tm=1024):
    ns = GMLP_SLABS

    def u_slab(j):
        return jnp.maximum(j - ns, 0)

    single = pl.Buffered(1)
    return pl.pallas_call(
        functools.partial(_gmlp_kernel, tm=tm),
        grid=(TOK // tm, 2 * ns),
        in_specs=[
            pl.BlockSpec((tm, D), lambda i, j: (i, 0), pipeline_mode=single),
            _mod_spec(layer, 3, tm, 2),
            _mod_spec(layer, 4, tm, 2),
            _gain_spec(layer, 2, 2),
            pl.BlockSpec((None, D, GMLP_COLS), lambda i, j: (widx, 0, jnp.where(j < ns, ns + j, j - ns))),
            pl.BlockSpec((None, 1, GMLP_COLS), lambda i, j: (widx, 0, u_slab(j))),
            pl.BlockSpec((None, 1, GMLP_COLS), lambda i, j: (widx, 0, u_slab(j))),
            pl.BlockSpec((None, GMLP_GPS, CHUNK, CHUNK), lambda i, j: (widx, u_slab(j), 0, 0)),
            pl.BlockSpec((None, GMLP_GPS, CHUNK, 1), lambda i, j: (widx, u_slab(j), 0, 0)),
            pl.BlockSpec((None, GMLP_COLS, D), lambda i, j: (widx, u_slab(j), 0)),
            _mod_spec(layer, 5, tm, 2),
            _gain_spec(layer, 3, 2),
        ],
        out_specs=pl.BlockSpec((tm, D), lambda i, j: (i, 0)),
        out_shape=jax.ShapeDtypeStruct((TOK, D), F32),
        scratch_shapes=[pltpu.VMEM((tm, D), BF16), pltpu.VMEM((ns, tm, GMLP_COLS), F32),
                        pltpu.VMEM((tm, 1), F32), pltpu.VMEM((tm, 1), F32)],
        compiler_params=_cparams(("arbitrary", "arbitrary"), VMEM_LIMIT_BIG),
        name=f"gmlp_l{layer}",
    )(x, mods, mods, gains, w_in, ln_g, ln_b, w_s, b_s, w_out, mods, gains)


def _rope_tables():
    pos = jnp.arange(LAT_LEN)
    pos_r = (pos // GRID_W).astype(F32)
    pos_c = (pos % GRID_W).astype(F32)
    inv = ROPE_BASE ** (-jnp.arange(ROT_PAIRS, dtype=F32) / ROT_PAIRS)
    ang_r = pos_r[:, None] * inv
    ang_c = pos_c[:, None] * inv
    cos = jnp.concatenate([jnp.cos(ang_r)] * 2 + [jnp.cos(ang_c)] * 2, axis=1)
    sin = jnp.concatenate([-jnp.sin(ang_r), jnp.sin(ang_r), -jnp.sin(ang_c), jnp.sin(ang_c)], axis=1)
    reps = LANES // HEAD_DIM
    return jnp.tile(cos, (1, reps)), jnp.tile(sin, (1, reps))


def _qkv_kernel(x_ref, sh_ref, sc_ref, g_ref, w_ref, cos_ref, sin_ref, o_ref, *, tm):
    i = pl.program_id(0)
    h = _adaln(x_ref[...], g_ref[...], sh_ref[...], sc_ref[...])
    y = jnp.dot(h.astype(BF16), w_ref[...].astype(BF16), preferred_element_type=F32)

    @pl.when(i * tm < CTX_TOK)
    def _():
        o_ref[...] = y

    @pl.when(i * tm >= CTX_TOK)
    def _():
        cos = cos_ref[...]
        sin = sin_ref[...]
        lane = lax.broadcasted_iota(jnp.int32, (tm, LANES), 1)
        first = (lane % (2 * ROT_PAIRS)) < ROT_PAIRS
        for cb in range((Q_DIM + KV_DIM) // LANES):
            t = y[:, cb * LANES:(cb + 1) * LANES]
            partner = jnp.where(first, pltpu.roll(t, LANES - ROT_PAIRS, axis=1),
                                pltpu.roll(t, ROT_PAIRS, axis=1))
            o_ref[:, cb * LANES:(cb + 1) * LANES] = t * cos + partner * sin
        o_ref[:, Q_DIM + KV_DIM:] = y[:, Q_DIM + KV_DIM:]


def _qkv_proj(x, mods, gains, w, widx, layer, tm=512):
    cos, sin = _rope_tables()
    per_seq = LAT_LEN // tm
    tab = pl.BlockSpec((tm, LANES), lambda i: (jnp.maximum(i - CTX_TOK // tm, 0) % per_seq, 0))
    return pl.pallas_call(
        functools.partial(_qkv_kernel, tm=tm),
        grid=(TOK // tm,),
        in_specs=[
            pl.BlockSpec((tm, D), lambda i: (i, 0)),
            _mod_spec(layer, 3, tm, 1),
            _mod_spec(layer, 4, tm, 1),
            _gain_spec(layer, 2, 1),
            pl.BlockSpec((None, D, QKV_DIM), lambda i: (widx, 0, 0)),
            tab, tab,
        ],
        out_specs=pl.BlockSpec((tm, QKV_DIM), lambda i: (i, 0)),
        out_shape=jax.ShapeDtypeStruct((TOK, QKV_DIM), F32),
        compiler_params=_cparams(("arbitrary",)),
        name=f"qkv_l{layer}",
    )(x, mods, mods, gains, w, cos, sin)


def _attend(q4, keys, vals, sink_rep, masks):
    logits = []
    for k, mk in zip(keys, masks):
        s = lax.dot_general(q4, k, (((1,), (1,)), ((), ())), preferred_element_type=F32) * ATTN_SCALE
        if mk is not None:
            s = jnp.where(mk, s, NEG_INF)
        logits.append([s[:, cb * LANES:(cb + 1) * LANES] for cb in range(s.shape[1] // LANES)])
    mx = None
    for blocks in logits:
        for blk in blocks:
            mx = blk if mx is None else jnp.maximum(mx, blk)
    m = jnp.maximum(sink_rep, jnp.max(mx, axis=-1, keepdims=True))
    o = rs = None
    for blocks, v in zip(logits, vals):
        p = jnp.concatenate([jnp.exp(blk - m) for blk in blocks], axis=1).astype(BF16)
        pv = jnp.dot(p, v, preferred_element_type=F32)
        ps = jnp.dot(p, jnp.ones((p.shape[1], LANES), BF16), preferred_element_type=F32)
        o = pv if o is None else o + pv
        rs = ps if rs is None else rs + ps
    denom = rs + jnp.exp(sink_rep - m)
    return o / denom[:, :HEAD_DIM]


def _attn_ctx_kernel(sink_ref, q_ref, k_ref, v_ref, o_ref):
    rows = CTX_LEN
    outs = [None] * N_Q_HEADS
    for hk in range(N_KV_HEADS):
        k = k_ref[:, hk * HEAD_DIM:(hk + 1) * HEAD_DIM].astype(BF16)
        v = v_ref[:, hk * HEAD_DIM:(hk + 1) * HEAD_DIM].astype(BF16)
        heads = [hk * Q_PER_KV + g for g in range(Q_PER_KV)]
        q4 = jnp.concatenate([q_ref[:, h * HEAD_DIM:(h + 1) * HEAD_DIM] for h in heads], axis=0).astype(BF16)
        sink_rep = jnp.concatenate([jnp.full((rows, LANES), sink_ref[h], F32) for h in heads], axis=0)
        o4 = _attend(q4, [k], [v], sink_rep, [None])
        for g, h in enumerate(heads):
            outs[h] = o4[g * rows:(g + 1) * rows]
    o_ref[...] = jnp.concatenate(outs, axis=1).astype(o_ref.dtype)


def _attn_lat_kernel(sink_ref, q_ref, kp_ref, kc_ref, kn_ref, vp_ref, vc_ref, vn_ref, ck_ref, cv_ref,
                     o_ref, *, n_blk):
    qi = pl.program_id(1)
    rows = CHUNK
    r4 = Q_PER_KV * rows
    a = lax.broadcasted_iota(jnp.int32, (r4, CHUNK), 0) % rows
    s = lax.broadcasted_iota(jnp.int32, (r4, CHUNK), 1)
    mask_prev = (s >= a) & (qi > 0)
    mask_next = (s <= a) & (qi < n_blk - 1)
    outs = [None] * N_Q_HEADS
    for hk in range(N_KV_HEADS):
        sl = slice(hk * HEAD_DIM, (hk + 1) * HEAD_DIM)
        keys = [kp_ref[:, sl].astype(BF16), kc_ref[:, sl].astype(BF16), kn_ref[:, sl].astype(BF16),
                ck_ref[:, sl].astype(BF16)]
        vals = [vp_ref[:, sl].astype(BF16), vc_ref[:, sl].astype(BF16), vn_ref[:, sl].astype(BF16),
                cv_ref[:, sl].astype(BF16)]
        heads = [hk * Q_PER_KV + g for g in range(Q_PER_KV)]
        q4 = jnp.concatenate([q_ref[:, h * HEAD_DIM:(h + 1) * HEAD_DIM] for h in heads], axis=0).astype(BF16)
        sink_rep = jnp.concatenate([jnp.full((rows, LANES), sink_ref[h], F32) for h in heads], axis=0)
        o4 = _attend(q4, keys, vals, sink_rep, [mask_prev, None, mask_next, None])
        for g, h in enumerate(heads):
            outs[h] = o4[g * rows:(g + 1) * rows]
    o_ref[...] = jnp.concatenate(outs, axis=1).astype(o_ref.dtype)


def _attention(qkv, sinks, ctx_k, ctx_v):
    smem = pl.BlockSpec(memory_space=pltpu.SMEM)
    kcol, vcol = Q_DIM // KV_DIM, Q_DIM // KV_DIM + 1
    ctx_blk = CTX_LEN
    o_ctx = pl.pallas_call(
        _attn_ctx_kernel,
        grid=(N_CTX_SEQ,),
        in_specs=[
            smem,
            pl.BlockSpec((ctx_blk, Q_DIM), lambda b: (b, 0)),
            pl.BlockSpec((ctx_blk, KV_DIM), lambda b: (b, kcol)),
            pl.BlockSpec((ctx_blk, KV_DIM), lambda b: (b, vcol)),
        ],
        out_specs=pl.BlockSpec((ctx_blk, Q_DIM), lambda b: (b, 0)),
        out_shape=jax.ShapeDtypeStruct((CTX_TOK, Q_DIM), BF16),
        compiler_params=_cparams(("arbitrary",)),
        name="attn_ctx",
    )(sinks, qkv, qkv, qkv)

    n_blk = LAT_LEN // CHUNK
    base = CTX_TOK // CHUNK

    def rb(b, qi, off):
        return base + b * n_blk + jnp.clip(qi + off, 0, n_blk - 1)

    def kv_spec(col, off):
        return pl.BlockSpec((CHUNK, KV_DIM), lambda b, qi: (rb(b, qi, off), col))

    o_lat = pl.pallas_call(
        functools.partial(_attn_lat_kernel, n_blk=n_blk),
        grid=(N_LAT_SEQ, n_blk),
        in_specs=[
            smem,
            pl.BlockSpec((CHUNK, Q_DIM), lambda b, qi: (rb(b, qi, 0), 0)),
            kv_spec(kcol, -1), kv_spec(kcol, 0), kv_spec(kcol, 1),
            kv_spec(vcol, -1), kv_spec(vcol, 0), kv_spec(vcol, 1),
            pl.BlockSpec((None, CTX_LEN, KV_DIM), lambda b, qi: (b, 0, 0)),
            pl.BlockSpec((None, CTX_LEN, KV_DIM), lambda b, qi: (b, 0, 0)),
        ],
        out_specs=pl.BlockSpec((CHUNK, Q_DIM), lambda b, qi: (b * n_blk + qi, 0)),
        out_shape=jax.ShapeDtypeStruct((LAT_TOK, Q_DIM), BF16),
        compiler_params=_cparams(("arbitrary", "arbitrary")),
        name="attn_lat",
    )(sinks, qkv, qkv, qkv, qkv, qkv, qkv, qkv, ctx_k, ctx_v)
    return o_ctx, o_lat


def _softplus(x):
    return jnp.maximum(x, 0.0) + jnp.log1p(jnp.exp(-jnp.abs(x)))


def _ssm_in_kernel(x_ref, sh_ref, sc_ref, g_ref, w_ref, wdt_ref, bdt_ref, zx_ref, dt_ref, h_ref, *, n_main):
    j = pl.program_id(1)

    @pl.when(j == 0)
    def _():
        h_ref[...] = _adaln(x_ref[...], g_ref[...], sh_ref[...], sc_ref[...]).astype(BF16)

    @pl.when(j < n_main)
    def _():
        zx_ref[...] = jnp.dot(h_ref[...], w_ref[...].astype(BF16), preferred_element_type=F32)

    @pl.when(j == n_main)
    def _():
        y = jnp.dot(h_ref[...], wdt_ref[...].astype(BF16), preferred_element_type=F32)
        dt_ref[...] = _softplus(y + bdt_ref[...])


def _ssm_in(x, mods, gains, w, widx, w_dt, b_dt, layer, tm=1024, tn=1024):
    n_main = SSM_MAIN // tn
    return pl.pallas_call(
        functools.partial(_ssm_in_kernel, n_main=n_main),
        grid=(TOK // tm, n_main + 1),
        in_specs=[
            pl.BlockSpec((tm, D), lambda i, j: (i, 0)),
            _mod_spec(layer, 3, tm, 2),
            _mod_spec(layer, 4, tm, 2),
            _gain_spec(layer, 2, 2),
            pl.BlockSpec((None, D, tn), lambda i, j: (widx, 0, jnp.minimum(j, n_main - 1))),
            pl.BlockSpec((D, LANES), lambda i, j: (0, 0)),
            pl.BlockSpec((1, LANES), lambda i, j: (0, 0)),
        ],
        out_specs=[
            pl.BlockSpec((tm, tn), lambda i, j: (i, jnp.minimum(j, n_main - 1))),
            pl.BlockSpec((tm, LANES), lambda i, j: (i, 0)),
        ],
        out_shape=[jax.ShapeDtypeStruct((TOK, SSM_MAIN), F32), jax.ShapeDtypeStruct((TOK, LANES), F32)],
        scratch_shapes=[pltpu.VMEM((tm, D), BF16)],
        compiler_params=_cparams(("arbitrary", "arbitrary")),
        name=f"ssm_in_l{layer}",
    )(x, mods, mods, gains, w, w_dt, b_dt)


def _split3(q):
    hi = q.astype(BF16)
    r1 = q - hi.astype(F32)
    mid = r1.astype(BF16)
    lo = (r1 - mid.astype(F32)).astype(BF16)
    return hi, mid, lo


def _spread(q, r_ref):
    hi, mid, lo = _split3(q[:, :SSM_HEADS])
    return jnp.dot(jnp.concatenate([hi, mid, lo], axis=1), r_ref[...], preferred_element_type=F32)


def _spread_consts():
    r = jnp.arange(3 * SSM_HEADS) % SSM_HEADS
    r1 = (jnp.arange(SSM_HEADS * SSM_P)[None, :] // SSM_P == r[:, None]).astype(BF16)
    r2 = (jnp.arange(SSM_HEADS * CHUNK)[None, :] // CHUNK == r[:, None]).astype(BF16)
    return r1, r2


def _ssd_kernel(*refs, nc, has_h0, emit_state):
    (z_ref, x_ref, xp_ref, xn_ref, bc_ref, bcp_ref, bcn_ref, dt_ref,
     cw_ref, cb_ref, a_ref, dsk_ref, ng_ref, r1_ref, r2_ref) = refs[:15]
    pos = 15
    h0_ref = None
    if has_h0:
        h0_ref = refs[pos]
        pos += 1
    y_ref = refs[pos]
    pos += 1
    st_ref = None
    if emit_state:
        st_ref = refs[pos]
        pos += 1
    hs_ref, yf_ref, xs_ref, bcs_ref = refs[pos:pos + 4]

    s = pl.program_id(1)
    bwd = s >= nc
    c = jnp.where(bwd, 2 * nc - 1 - s, s)
    L = CHUNK
    off = pl.multiple_of(c * L, L)

    @pl.when((s == 0) | (s == nc))
    def _():
        if has_h0:
            @pl.when(s == 0)
            def _():
                hs_ref[...] = h0_ref[0]

            @pl.when(s == nc)
            def _():
                hs_ref[...] = h0_ref[1]
        else:
            hs_ref[...] = jnp.zeros_like(hs_ref)

    @pl.when(jnp.logical_not(bwd))
    def _():
        row = lax.broadcasted_iota(jnp.int32, (L, 1), 0)

        def conv_silu(cur, prev_blk, next_blk, lo, hi):
            w = cw_ref[:, lo:hi]
            prev_row = jnp.where(c > 0, prev_blk[HALO - 1:HALO, :], 0.0)
            next_row = jnp.where(c < nc - 1, next_blk[0:1, :], 0.0)
            up = jnp.where(row == 0, prev_row, pltpu.roll(cur, 1, axis=0))
            dn = jnp.where(row == L - 1, next_row, pltpu.roll(cur, L - 1, axis=0))
            return _silu(up * w[0:1] + cur * w[1:2] + dn * w[2:3] + cb_ref[:, lo:hi])

        xs_ref[pl.ds(off, L), :] = conv_silu(x_ref[...], xp_ref[...], xn_ref[...], 0, SSM_INNER)
        bcs_ref[pl.ds(off, L), :] = conv_silu(bc_ref[...], bcp_ref[...], bcn_ref[...],
                                              SSM_INNER, SSM_CONV_DIM).astype(BF16)

    xs = xs_ref[pl.ds(off, L), :]
    bcb = bcs_ref[pl.ds(off, L), :]

    dt_all = dt_ref[...]
    dt = jnp.where(bwd, pltpu.roll(dt_all, LANES - SSM_HEADS, axis=1), dt_all)
    a_row = jnp.where(bwd, a_ref[1:2, :], a_ref[0:1, :])
    ii = lax.broadcasted_iota(jnp.int32, (L, L), 0)
    jj = lax.broadcasted_iota(jnp.int32, (L, L), 1)
    tri = jnp.where(bwd, jj - ii, ii - jj) >= 0
    tri_b = jnp.where(tri, 1.0, 0.0).astype(BF16)
    neg_mask = jnp.where(tri, 0.0, -jnp.inf)
    hi_p, mid_p, lo_p = _split3(dt * a_row)
    acum = (jnp.dot(tri_b, hi_p, preferred_element_type=F32)
            + jnp.dot(tri_b, mid_p, preferred_element_type=F32)
            + jnp.dot(tri_b, lo_p, preferred_element_type=F32))
    total = jnp.where(bwd, acum[0:1, :], acum[L - 1:L, :])
    acum_t = acum.T
    cdec_t = jnp.exp(jnp.where(bwd, acum_t[:, 0:1], acum_t[:, L - 1:L]))

    e_acum = _spread(acum, r2_ref)
    e_dt = _spread(dt, r1_ref)
    e_end = _spread(dt * jnp.exp(total - acum), r1_ref)
    e_in = _spread(jnp.exp(acum), r1_ref)
    xdt = (xs * e_dt).astype(BF16)
    xw = (xs * e_end).astype(BF16)
    left = lax.broadcasted_iota(jnp.int32, (L, LANES), 1) < SSM_P

    y_parts = []
    for g in range(SSM_GROUPS):
        bm = bcb[:, g * SSM_STATE:(g + 1) * SSM_STATE]
        cm = bcb[:, SSM_GN + g * SSM_STATE:SSM_GN + (g + 1) * SSM_STATE]
        cb = lax.dot_general(cm, bm, (((1,), (1,)), ((), ())), preferred_element_type=F32)
        gsl = slice(g * SSM_HG * SSM_P, (g + 1) * SSM_HG * SSM_P)
        hprev = hs_ref[gsl, :]
        y_off = lax.dot_general(cm, hprev.astype(BF16), (((1,), (1,)), ((), ())),
                                preferred_element_type=F32)
        y_pairs = []
        for pr in range(SSM_HG // 2):
            wm = []
            for h in (g * SSM_HG + 2 * pr, g * SSM_HG + 2 * pr + 1):
                seg = e_acum[:, h * L:(h + 1) * L] - acum_t[h:h + 1, :] + neg_mask
                wm.append((cb * jnp.exp(seg)).astype(BF16))
            pair = xdt[:, (g * SSM_HG + 2 * pr) * SSM_P:(g * SSM_HG + 2 * pr + 2) * SSM_P]
            rhs = jnp.concatenate([jnp.where(left, pair, jnp.zeros_like(pair)),
                                   jnp.where(left, jnp.zeros_like(pair), pair)], axis=0)
            y_pairs.append(jnp.dot(jnp.concatenate(wm, axis=1), rhs, preferred_element_type=F32))
        y_parts.append(jnp.concatenate(y_pairs, axis=1) + y_off * e_in[:, gsl])
        st = lax.dot_general(xw[:, gsl], bm, (((0,), (0,)), ((), ())), preferred_element_type=F32)
        dec = jnp.concatenate([jnp.broadcast_to(cdec_t[g * SSM_HG + k:g * SSM_HG + k + 1, :], (SSM_P, 1))
                               for k in range(SSM_HG)], axis=0)
        hs_ref[gsl, :] = dec * hprev + st
    y = jnp.concatenate(y_parts, axis=1)

    @pl.when(jnp.logical_not(bwd))
    def _():
        yf_ref[pl.ds(off, L), :] = y + dsk_ref[...] * xs

    @pl.when(bwd)
    def _():
        yt = (y + yf_ref[pl.ds(off, L), :]) * _silu(z_ref[...])
        y_ref[...] = _rms(yt, ng_ref[...]).astype(y_ref.dtype)

    if emit_state:
        @pl.when(s == nc - 1)
        def _():
            st_ref[0] = hs_ref[...]

        @pl.when(s == 2 * nc - 1)
        def _():
            st_ref[1] = hs_ref[...]


def _ssd(zx, dt, conv_w, conv_b, a_pad, dsk, norm_g, r1, r2, h0, *, seq0, n_seq, seq_len, emit_state):
    nc = seq_len // CHUNK
    hb = CHUNK // HALO
    n_rows_halo = TOK // HALO
    chunk0 = seq0 // CHUNK

    def cidx(b, s):
        return chunk0 + b * nc + jnp.where(s >= nc, 2 * nc - 1 - s, s)

    def fidx(b, s):
        return chunk0 + b * nc + jnp.minimum(s, nc - 1)

    def zidx(b, s):
        return chunk0 + b * nc + jnp.where(s >= nc, 2 * nc - 1 - s, nc - 1)

    def prev_halo(b, s):
        return jnp.maximum(fidx(b, s) * hb - 1, 0)

    def next_halo(b, s):
        return jnp.minimum(fidx(b, s) * hb + hb, n_rows_halo - 1)

    xcol = 1
    bccol = (2 * SSM_INNER) // (2 * SSM_GN)
    const = lambda b, s: (0, 0)
    in_specs = [
        pl.BlockSpec((CHUNK, SSM_INNER), lambda b, s: (zidx(b, s), 0)),
        pl.BlockSpec((CHUNK, SSM_INNER), lambda b, s: (fidx(b, s), xcol)),
        pl.BlockSpec((HALO, SSM_INNER), lambda b, s: (prev_halo(b, s), xcol)),
        pl.BlockSpec((HALO, SSM_INNER), lambda b, s: (next_halo(b, s), xcol)),
        pl.BlockSpec((CHUNK, 2 * SSM_GN), lambda b, s: (fidx(b, s), bccol)),
        pl.BlockSpec((HALO, 2 * SSM_GN), lambda b, s: (prev_halo(b, s), bccol)),
        pl.BlockSpec((HALO, 2 * SSM_GN), lambda b, s: (next_halo(b, s), bccol)),
        pl.BlockSpec((CHUNK, LANES), lambda b, s: (cidx(b, s), 0)),
        pl.BlockSpec((3, SSM_CONV_DIM), const),
        pl.BlockSpec((1, SSM_CONV_DIM), const),
        pl.BlockSpec((2, LANES), const),
        pl.BlockSpec((1, SSM_INNER), const),
        pl.BlockSpec((1, SSM_INNER), const),
        pl.BlockSpec(r1.shape, const),
        pl.BlockSpec(r2.shape, const),
    ]
    args = [zx, zx, zx, zx, zx, zx, zx, dt, conv_w, conv_b, a_pad, dsk, norm_g, r1, r2]
    if h0 is not None:
        in_specs.append(pl.BlockSpec((None, 2, SSM_INNER, SSM_STATE), lambda b, s: (b, 0, 0, 0)))
        args.append(h0)
    out_specs = [pl.BlockSpec((CHUNK, SSM_INNER), lambda b, s: (zidx(b, s) - chunk0, 0))]
    out_shape = [jax.ShapeDtypeStruct((n_seq * seq_len, SSM_INNER), BF16)]
    if emit_state:
        out_specs.append(pl.BlockSpec((None, 2, SSM_INNER, SSM_STATE), lambda b, s: (b, 0, 0, 0)))
        out_shape.append(jax.ShapeDtypeStruct((n_seq, 2, SSM_INNER, SSM_STATE), F32))
    return pl.pallas_call(
        functools.partial(_ssd_kernel, nc=nc, has_h0=h0 is not None, emit_state=emit_state),
        grid=(n_seq, 2 * nc),
        in_specs=in_specs,
        out_specs=out_specs,
        out_shape=out_shape,
        scratch_shapes=[pltpu.VMEM((SSM_INNER, SSM_STATE), F32), pltpu.VMEM((seq_len, SSM_INNER), F32),
                        pltpu.VMEM((seq_len, SSM_INNER), F32), pltpu.VMEM((seq_len, 2 * SSM_GN), BF16)],
        compiler_params=_cparams(("arbitrary", "arbitrary")),
        name=f"ssd_{seq_len}",
    )(*args)


def kernel(x_prompt, x_sample, cache_k, cache_v, state_ssm, c, c_ctx, w_mod, b_mod, norm_g, ffn_in, ffn_out,
           gmlp_in, gmlp_ln_g, gmlp_ln_b, gmlp_ws, gmlp_bs, gmlp_out, attn_qkv, attn_sink, attn_out,
           ssm_in, ssm_conv_w, ssm_conv_b, ssm_dt_bias, ssm_a_log, ssm_d, ssm_norm, ssm_out):
    cond = jnp.concatenate([c_ctx[None, :], c, jnp.zeros((MOD_ROWS - 1 - N_LAT_SEQ, D), F32)], axis=0)
    mods = _modulation(cond, w_mod, b_mod).reshape(DEPTH * MOD_ROWS * N_MOD, 1, D)
    gains = norm_g.reshape(DEPTH * 6, 1, D)

    x = (x_prompt.reshape(CTX_TOK, D), x_sample.reshape(LAT_TOK, D))
    new_k = new_v = new_s = None
    for layer in range(DEPTH):
        x = _ffn(x, mods, gains, ffn_in, ffn_out, layer, 0, split_in=layer == 0)
        kind, j = layer % 3, layer // 3
        if kind == 0:
            x = _gmlp(x, mods, gains, gmlp_in, gmlp_ln_g[:, None, :], gmlp_ln_b[:, None, :], gmlp_ws,
                      gmlp_bs[..., None], gmlp_out, j, layer)
        elif kind == 1:
            qkv = _qkv_proj(x, mods, gains, attn_qkv, j, layer)
            kv = qkv[:CTX_TOK, Q_DIM:].reshape(N_CTX_SEQ, CTX_LEN, 2, N_KV_HEADS, HEAD_DIM)
            new_k, new_v = kv[:, :, 0], kv[:, :, 1]
            o_ctx, o_lat = _attention(qkv, attn_sink[j],
                                      cache_k[:, j].reshape(N_LAT_SEQ, CTX_LEN, KV_DIM),
                                      cache_v[:, j].reshape(N_LAT_SEQ, CTX_LEN, KV_DIM))
            x = _out_proj(o_ctx, o_lat, attn_out, j, x, mods, gains, layer)
        else:
            pad = LANES - 2 * SSM_HEADS
            w_dt = jnp.pad(ssm_in[j][:, SSM_MAIN:], ((0, 0), (0, pad)))
            b_dt = jnp.pad(ssm_dt_bias[j].reshape(1, 2 * SSM_HEADS), ((0, 0), (0, pad)))
            zx, dt = _ssm_in(x, mods, gains, ssm_in, j, w_dt, b_dt, layer)
            a_pad = jnp.pad(-jnp.exp(ssm_a_log[j]), ((0, 0), (0, LANES - SSM_HEADS)))
            dsk = jnp.repeat(ssm_d[j], SSM_P)[None, :]
            ng = ssm_norm[j][None, :]
            cw, cb = ssm_conv_w[j], ssm_conv_b[j][None, :]
            r1, r2 = _spread_consts()
            y_ctx, st = _ssd(zx, dt, cw, cb, a_pad, dsk, ng, r1, r2, None,
                             seq0=0, n_seq=N_CTX_SEQ, seq_len=CTX_LEN, emit_state=True)
            (y_lat,) = _ssd(zx, dt, cw, cb, a_pad, dsk, ng, r1, r2,
                            state_ssm[:, j].reshape(N_LAT_SEQ, 2, SSM_INNER, SSM_STATE),
                            seq0=CTX_TOK, n_seq=N_LAT_SEQ, seq_len=LAT_LEN, emit_state=False)
            new_s = st.reshape(N_CTX_SEQ, 2, SSM_HEADS, SSM_P, SSM_STATE)
            x = _out_proj(y_ctx, y_lat, ssm_out, j, x, mods, gains, layer)
        x = _ffn(x, mods, gains, ffn_in, ffn_out, layer, 1, split_out=layer == DEPTH - 1)

    y_prompt = x[0].reshape(N_CTX_SEQ, CTX_LEN, D)
    y_sample = x[1].reshape(N_LAT_SEQ, LAT_LEN, D)
    return (y_prompt, y_sample, new_k[:, None], new_v[:, None], new_s[:, None])
```

```python
import functools
import math

import jax
import jax.numpy as jnp
from jax import lax
from jax.experimental import pallas as pl
from jax.experimental.pallas import tpu as pltpu

F32 = jnp.float32
BF16 = jnp.bfloat16

D = 1024
N_CTX_SEQ, CTX_LEN = 16, 256
N_LAT_SEQ, LAT_LEN = 4, 1024
CTX_TOK = N_CTX_SEQ * CTX_LEN
LAT_TOK = N_LAT_SEQ * LAT_LEN
TOK = CTX_TOK + LAT_TOK
DEPTH = 4
N_MOD = 9
MOD_ROWS = 8
D_FF = 2816
EPS = 1e-6
GRID_W = 64
GMLP_HALF = 3 * D
GMLP_GROUPS = 8
GMLP_GD = GMLP_HALF // GMLP_GROUPS
CHUNK = 128
HEAD_DIM = 64
N_Q_HEADS = 16
N_KV_HEADS = 4
Q_PER_KV = 4
Q_DIM = N_Q_HEADS * HEAD_DIM
KV_DIM = N_KV_HEADS * HEAD_DIM
QKV_DIM = Q_DIM + 2 * KV_DIM
ATTN_SCALE = HEAD_DIM ** -0.5
ROPE_BASE = 10000.0
ROT_PAIRS = HEAD_DIM // 4
NEG_INF = -1e30
SSM_INNER = 2 * D
SSM_HEADS = 32
SSM_P = 64
SSM_GROUPS = 4
SSM_HG = SSM_HEADS // SSM_GROUPS
SSM_STATE = 128
SSM_GN = SSM_GROUPS * SSM_STATE
SSM_CONV_DIM = SSM_INNER + 2 * SSM_GN
SSM_MAIN = SSM_INNER + SSM_CONV_DIM
LANES = 128
HALO = 8

VMEM_LIMIT = 56 * 1024 * 1024
VMEM_LIMIT_BIG = 60 * 1024 * 1024


def _cparams(sem, limit=VMEM_LIMIT):
    return pltpu.CompilerParams(dimension_semantics=sem, vmem_limit_bytes=limit)


def _sigmoid(x):
    return 1.0 / (1.0 + jnp.exp(-x))


def _silu(x):
    return x * _sigmoid(x)


def _rms(x, g):
    return x * lax.rsqrt(jnp.mean(x * x, axis=-1, keepdims=True) + EPS) * g


def _adaln(x, g, shift, scale):
    return _rms(x, g) * (1.0 + scale) + shift


def _mod_row(i, tm):
    t0 = i * tm
    return jnp.where(t0 < CTX_TOK, 0, 1 + (t0 - CTX_TOK) // LAT_LEN)


def _mod_spec(layer, k, tm, grid_rank):
    base = layer * MOD_ROWS * N_MOD + k
    if grid_rank == 1:
        return pl.BlockSpec((None, 1, D), lambda i: (base + _mod_row(i, tm) * N_MOD, 0, 0))
    return pl.BlockSpec((None, 1, D), lambda i, j: (base + _mod_row(i, tm) * N_MOD, 0, 0))


def _gain_spec(layer, k, grid_rank):
    idx = layer * 6 + k
    if grid_rank == 1:
        return pl.BlockSpec((None, 1, D), lambda i: (idx, 0, 0))
    return pl.BlockSpec((None, 1, D), lambda i, j: (idx, 0, 0))


def _split_specs(tm, width, grid_rank):
    na = CTX_TOK // tm
    if grid_rank == 1:
        return (pl.BlockSpec((tm, width), lambda i: (jnp.minimum(i, na - 1), 0)),
                pl.BlockSpec((tm, width), lambda i: (jnp.maximum(i - na, 0), 0)))
    return (pl.BlockSpec((tm, width), lambda i, j: (jnp.minimum(i, na - 1), 0)),
            pl.BlockSpec((tm, width), lambda i, j: (jnp.maximum(i - na, 0), 0)))


def _mod_kernel(cond_ref, w_ref, b_ref, o_ref):
    s = _silu(cond_ref[...]).astype(BF16)
    o_ref[...] = jnp.dot(s, w_ref[...].astype(BF16), preferred_element_type=F32) + b_ref[...]


def _modulation(cond, w_mod, b_mod):
    tn = 2304
    n = N_MOD * D
    return pl.pallas_call(
        _mod_kernel,
        grid=(DEPTH, n // tn),
        in_specs=[
            pl.BlockSpec((MOD_ROWS, D), lambda l, j: (0, 0)),
            pl.BlockSpec((None, D, tn), lambda l, j: (l, 0, j)),
            pl.BlockSpec((None, 1, tn), lambda l, j: (l, 0, j)),
        ],
        out_specs=pl.BlockSpec((None, MOD_ROWS, tn), lambda l, j: (l, 0, j)),
        out_shape=jax.ShapeDtypeStruct((DEPTH, MOD_ROWS, n), F32),
        compiler_params=_cparams(("arbitrary", "arbitrary")),
        name="modulation",
    )(cond, w_mod, b_mod.reshape(DEPTH, 1, n))


FFN_TF = 256
FFN_SLABS = D_FF // FFN_TF
FFN_SPS = 3
FFN_STEPS = -(-FFN_SLABS // FFN_SPS)


def _ffn_kernel(*refs, tm, split_in, split_out):
    n_x = 2 if split_in else 1
    n_o = 2 if split_out else 1
    x_refs = refs[:n_x]
    sh_ref, sc_ref, gt_ref, g0_ref, g1_ref = refs[n_x:n_x + 5]
    w0 = n_x + 5
    wg = refs[w0:w0 + FFN_SPS]
    wu = refs[w0 + FFN_SPS:w0 + 2 * FFN_SPS]
    wo = refs[w0 + 2 * FFN_SPS:w0 + 3 * FFN_SPS]
    o_refs = refs[w0 + 3 * FFN_SPS:w0 + 3 * FFN_SPS + n_o]
    h_ref, acc_ref = refs[w0 + 3 * FFN_SPS + n_o:]
    i = pl.program_id(0)
    j = pl.program_id(1)
    is_ctx = i * tm < CTX_TOK
    last = FFN_STEPS - 1

    def load_x():
        if split_in:
            return jnp.where(is_ctx, x_refs[0][...], x_refs[1][...])
        return x_refs[0][...]

    def swiglu_slabs(h, n):
        acts = []
        for k in range(n):
            g = jnp.dot(h, wg[k][...].astype(BF16), preferred_element_type=F32)
            u = jnp.dot(h, wu[k][...].astype(BF16), preferred_element_type=F32)
            acts.append((_silu(g) * u).astype(BF16))
        w = jnp.concatenate([wo[k][...].astype(BF16) for k in range(n)], axis=0)
        return jnp.dot(jnp.concatenate(acts, axis=1), w, preferred_element_type=F32)

    @pl.when(j == 0)
    def _():
        h = _adaln(load_x(), g0_ref[...], sh_ref[...], sc_ref[...]).astype(BF16)
        h_ref[...] = h
        acc_ref[...] = swiglu_slabs(h, FFN_SPS)

    @pl.when((j > 0) & (j < last))
    def _():
        acc_ref[...] += swiglu_slabs(h_ref[...], FFN_SPS)

    @pl.when(j == last)
    def _():
        f = acc_ref[...] + swiglu_slabs(h_ref[...], FFN_SLABS - last * FFN_SPS)
        res = load_x() + (0.5 * gt_ref[...]) * _rms(f, g1_ref[...])
        if split_out:
            @pl.when(is_ctx)
            def _():
                o_refs[0][...] = res

            @pl.when(jnp.logical_not(is_ctx))
            def _():
                o_refs[1][...] = res
        else:
            o_refs[0][...] = res


def _ffn(x, mods, gains, ffn_in, ffn_out, layer, which, split_in=False, split_out=False):
    tm, tf = 1024, FFN_TF
    k0 = 0 if which == 0 else 6
    gi = 0 if which == 0 else 4
    single = pl.Buffered(1)
    if split_in:
        x_specs = list(_split_specs(tm, D, 2))
    else:
        x_specs = [pl.BlockSpec((tm, D), lambda i, j: (i, 0))]
    x_args = list(x) if split_in else [x]
    if split_out:
        sa, sb = _split_specs(tm, D, 2)
        out_specs = [pl.BlockSpec(sa.block_shape, sa.index_map, pipeline_mode=single),
                     pl.BlockSpec(sb.block_shape, sb.index_map, pipeline_mode=single)]
        out_shape = [jax.ShapeDtypeStruct((CTX_TOK, D), F32), jax.ShapeDtypeStruct((LAT_TOK, D), F32)]
    else:
        out_specs = pl.BlockSpec((tm, D), lambda i, j: (i, 0))
        out_shape = jax.ShapeDtypeStruct((TOK, D), F32)

    def slab(j, k):
        return jnp.minimum(j * FFN_SPS + k, FFN_SLABS - 1)

    def spec_in(k, col0):
        return pl.BlockSpec((None, None, D, tf), lambda i, j: (layer, which, 0, col0 + slab(j, k)))

    def spec_out(k):
        return pl.BlockSpec((None, None, tf, D), lambda i, j: (layer, which, slab(j, k), 0))

    w_specs = ([spec_in(k, 0) for k in range(FFN_SPS)] + [spec_in(k, FFN_SLABS) for k in range(FFN_SPS)]
               + [spec_out(k) for k in range(FFN_SPS)])
    return pl.pallas_call(
        functools.partial(_ffn_kernel, tm=tm, split_in=split_in, split_out=split_out),
        grid=(TOK // tm, FFN_STEPS),
        in_specs=x_specs + [
            _mod_spec(layer, k0 + 0, tm, 2),
            _mod_spec(layer, k0 + 1, tm, 2),
            _mod_spec(layer, k0 + 2, tm, 2),
            _gain_spec(layer, gi, 2),
            _gain_spec(layer, gi + 1, 2),
        ] + w_specs,
        out_specs=out_specs,
        out_shape=out_shape,
        scratch_shapes=[pltpu.VMEM((tm, D), BF16), pltpu.VMEM((tm, D), F32)],
        compiler_params=_cparams(("arbitrary", "arbitrary"), VMEM_LIMIT_BIG),
        name=f"ffn_l{layer}_{which}",
    )(*x_args, mods, mods, mods, gains, gains, *([ffn_in] * (2 * FFN_SPS)), *([ffn_out] * FFN_SPS))


def _out_kernel(aa_ref, ab_ref, w_ref, x_ref, gt_ref, g_ref, o_ref, *, tm):
    a = jnp.where(pl.program_id(0) * tm < CTX_TOK, aa_ref[...], ab_ref[...])
    m = jnp.dot(a, w_ref[...].astype(BF16), preferred_element_type=F32)
    o_ref[...] = x_ref[...] + gt_ref[...] * _rms(m, g_ref[...])


def _out_proj(a_ctx, a_lat, w, widx, x, mods, gains, layer, tm=512):
    k = a_ctx.shape[1]
    sa, sb = _split_specs(tm, k, 1)
    return pl.pallas_call(
        functools.partial(_out_kernel, tm=tm),
        grid=(TOK // tm,),
        in_specs=[
            sa, sb,
            pl.BlockSpec((None, k, D), lambda i: (widx, 0, 0)),
            pl.BlockSpec((tm, D), lambda i: (i, 0)),
            _mod_spec(layer, 5, tm, 1),
            _gain_spec(layer, 3, 1),
        ],
        out_specs=pl.BlockSpec((tm, D), lambda i: (i, 0)),
        out_shape=jax.ShapeDtypeStruct((TOK, D), F32),
        compiler_params=_cparams(("arbitrary",)),
        name=f"outproj_l{layer}",
    )(a_ctx, a_lat, w, x, mods, gains)


GMLP_GPS = 2
GMLP_SLABS = GMLP_GROUPS // GMLP_GPS
GMLP_COLS = GMLP_GPS * GMLP_GD


def _gelu_exact(x):
    return 0.5 * x * (1.0 + lax.erf(x * (1.0 / math.sqrt(2.0))))


def _gmlp_kernel(x_ref, sh_ref, sc_ref, g2_ref, win_ref, lg_ref, lb_ref, ws_ref, bs_ref, wo_ref,
                 gt_ref, g3_ref, o_ref, h_ref, v_ref, mu_ref, rstd_ref, *, tm):
    j = pl.program_id(1)

    @pl.when(j == 0)
    def _():
        h_ref[...] = _adaln(x_ref[...], g2_ref[...], sh_ref[...], sc_ref[...]).astype(BF16)

    y = _gelu_exact(jnp.dot(h_ref[...], win_ref[...].astype(BF16), preferred_element_type=F32))

    @pl.when(j < GMLP_SLABS)
    def _():
        v_ref[j] = y

    @pl.when(j == GMLP_SLABS)
    def _():
        def fold(fn):
            acc = None
            for s in range(GMLP_SLABS):
                v = fn(v_ref[s])
                for cb in range(GMLP_COLS // LANES):
                    blk = v[:, cb * LANES:(cb + 1) * LANES]
                    acc = blk if acc is None else acc + blk
            return jnp.sum(acc, axis=-1, keepdims=True)

        mu = fold(lambda v: v) * (1.0 / GMLP_HALF)
        sq = fold(lambda v: (v - mu) * (v - mu))
        mu_ref[...] = mu
        rstd_ref[...] = lax.rsqrt(sq * (1.0 / GMLP_HALF) + EPS)
        o_ref[...] = jnp.zeros_like(o_ref)

    @pl.when(j >= GMLP_SLABS)
    def _():
        s = j - GMLP_SLABS
        vn = ((v_ref[s] - mu_ref[...]) * rstd_ref[...] * lg_ref[...] + lb_ref[...]).astype(BF16)
        rows = []
        for c in range(tm // CHUNK):
            cols = []
            for gg in range(GMLP_GPS):
                vg = vn[c * CHUNK:(c + 1) * CHUNK, gg * GMLP_GD:(gg + 1) * GMLP_GD]
                cols.append(jnp.dot(ws_ref[gg].astype(BF16), vg, preferred_element_type=F32) + bs_ref[gg])
            rows.append(jnp.concatenate(cols, axis=1))
        sv = jnp.concatenate(rows, axis=0)
        a = (y * sv).astype(BF16)
        o_ref[...] += jnp.dot(a, wo_ref[...].astype(BF16), preferred_element_type=F32)

    @pl.when(j == 2 * GMLP_SLABS - 1)
    def _():
        o_ref[...] = x_ref[...] + gt_ref[...] * _rms(o_ref[...], g3_ref[...])


def _gmlp(x, mods, gains, w_in, ln_g, ln_b, w_s, b_s, w_out, widx, layer, tm=1024):
    ns = GMLP_SLABS

    def u_slab(j):
        return jnp.maximum(j - ns, 0)

    single = pl.Buffered(1)
    return pl.pallas_call(
        functools.partial(_gmlp_kernel, tm=tm),
        grid=(TOK // tm, 2 * ns),
        in_specs=[
            pl.BlockSpec((tm, D), lambda i, j: (i, 0), pipeline_mode=single),
            _mod_spec(layer, 3, tm, 2),
            _mod_spec(layer, 4, tm, 2),
            _gain_spec(layer, 2, 2),
            pl.BlockSpec((None, D, GMLP_COLS), lambda i, j: (widx, 0, jnp.where(j < ns, ns + j, j - ns))),
            pl.BlockSpec((None, 1, GMLP_COLS), lambda i, j: (widx, 0, u_slab(j))),
            pl.BlockSpec((None, 1, GMLP_COLS), lambda i, j: (widx, 0, u_slab(j))),
            pl.BlockSpec((None, GMLP_GPS, CHUNK, CHUNK), lambda i, j: (widx, u_slab(j), 0, 0)),
            pl.BlockSpec((None, GMLP_GPS, CHUNK, 1), lambda i, j: (widx, u_slab(j), 0, 0)),
            pl.BlockSpec((None, GMLP_COLS, D), lambda i, j: (widx, u_slab(j), 0)),
            _mod_spec(layer, 5, tm, 2),
            _gain_spec(layer, 3, 2),
        ],
        out_specs=pl.BlockSpec((tm, D), lambda i, j: (i, 0)),
        out_shape=jax.ShapeDtypeStruct((TOK, D), F32),
        scratch_shapes=[pltpu.VMEM((tm, D), BF16), pltpu.VMEM((ns, tm, GMLP_COLS), F32),
                        pltpu.VMEM((tm, 1), F32), pltpu.VMEM((tm, 1), F32)],
        compiler_params=_cparams(("arbitrary", "arbitrary"), VMEM_LIMIT_BIG),
        name=f"gmlp_l{layer}",
    )(x, mods, mods, gains, w_in, ln_g, ln_b, w_s, b_s, w_out, mods, gains)


def _rope_tables():
    pos = jnp.arange(LAT_LEN)
    pos_r = (pos // GRID_W).astype(F32)
    pos_c = (pos % GRID_W).astype(F32)
    inv = ROPE_BASE ** (-jnp.arange(ROT_PAIRS, dtype=F32) / ROT_PAIRS)
    ang_r = pos_r[:, None] * inv
    ang_c = pos_c[:, None] * inv
    cos = jnp.concatenate([jnp.cos(ang_r)] * 2 + [jnp.cos(ang_c)] * 2, axis=1)
    sin = jnp.concatenate([-jnp.sin(ang_r), jnp.sin(ang_r), -jnp.sin(ang_c), jnp.sin(ang_c)], axis=1)
    reps = LANES // HEAD_DIM
    return jnp.tile(cos, (1, reps)), jnp.tile(sin, (1, reps))


def _qkv_kernel(x_ref, sh_ref, sc_ref, g_ref, w_ref, cos_ref, sin_ref, o_ref, *, tm):
    i = pl.program_id(0)
    h = _adaln(x_ref[...], g_ref[...], sh_ref[...], sc_ref[...])
    y = jnp.dot(h.astype(BF16), w_ref[...].astype(BF16), preferred_element_type=F32)

    @pl.when(i * tm < CTX_TOK)
    def _():
        o_ref[...] = y

    @pl.when(i * tm >= CTX_TOK)
    def _():
        cos = cos_ref[...]
        sin = sin_ref[...]
        lane = lax.broadcasted_iota(jnp.int32, (tm, LANES), 1)
        first = (lane % (2 * ROT_PAIRS)) < ROT_PAIRS
        for cb in range((Q_DIM + KV_DIM) // LANES):
            t = y[:, cb * LANES:(cb + 1) * LANES]
            partner = jnp.where(first, pltpu.roll(t, LANES - ROT_PAIRS, axis=1),
                                pltpu.roll(t, ROT_PAIRS, axis=1))
            o_ref[:, cb * LANES:(cb + 1) * LANES] = t * cos + partner * sin
        o_ref[:, Q_DIM + KV_DIM:] = y[:, Q_DIM + KV_DIM:]


def _qkv_proj(x, mods, gains, w, widx, layer, tm=512):
    cos, sin = _rope_tables()
    per_seq = LAT_LEN // tm
    tab = pl.BlockSpec((tm, LANES), lambda i: (jnp.maximum(i - CTX_TOK // tm, 0) % per_seq, 0))
    return pl.pallas_call(
        functools.partial(_qkv_kernel, tm=tm),
        grid=(TOK // tm,),
        in_specs=[
            pl.BlockSpec((tm, D), lambda i: (i, 0)),
            _mod_spec(layer, 3, tm, 1),
            _mod_spec(layer, 4, tm, 1),
            _gain_spec(layer, 2, 1),
            pl.BlockSpec((None, D, QKV_DIM), lambda i: (widx, 0, 0)),
            tab, tab,
        ],
        out_specs=pl.BlockSpec((tm, QKV_DIM), lambda i: (i, 0)),
        out_shape=jax.ShapeDtypeStruct((TOK, QKV_DIM), F32),
        compiler_params=_cparams(("arbitrary",)),
        name=f"qkv_l{layer}",
    )(x, mods, mods, gains, w, cos, sin)


def _attend(q4, keys, vals, sink_rep, masks):
    logits = []
    for k, mk in zip(keys, masks):
        s = lax.dot_general(q4, k, (((1,), (1,)), ((), ())), preferred_element_type=F32) * ATTN_SCALE
        if mk is not None:
            s = jnp.where(mk, s, NEG_INF)
        logits.append([s[:, cb * LANES:(cb + 1) * LANES] for cb in range(s.shape[1] // LANES)])
    mx = None
    for blocks in logits:
        for blk in blocks:
            mx = blk if mx is None else jnp.maximum(mx, blk)
    m = jnp.maximum(sink_rep, jnp.max(mx, axis=-1, keepdims=True))
    o = rs = None
    for blocks, v in zip(logits, vals):
        p = jnp.concatenate([jnp.exp(blk - m) for blk in blocks], axis=1).astype(BF16)
        pv = jnp.dot(p, v, preferred_element_type=F32)
        ps = jnp.dot(p, jnp.ones((p.shape[1], LANES), BF16), preferred_element_type=F32)
        o = pv if o is None else o + pv
        rs = ps if rs is None else rs + ps
    denom = rs + jnp.exp(sink_rep - m)
    return o / denom[:, :HEAD_DIM]


def _attn_ctx_kernel(sink_ref, q_ref, k_ref, v_ref, o_ref):
    rows = CTX_LEN
    outs = [None] * N_Q_HEADS
    for hk in range(N_KV_HEADS):
        k = k_ref[:, hk * HEAD_DIM:(hk + 1) * HEAD_DIM].astype(BF16)
        v = v_ref[:, hk * HEAD_DIM:(hk + 1) * HEAD_DIM].astype(BF16)
        heads = [hk * Q_PER_KV + g for g in range(Q_PER_KV)]
        q4 = jnp.concatenate([q_ref[:, h * HEAD_DIM:(h + 1) * HEAD_DIM] for h in heads], axis=0).astype(BF16)
        sink_rep = jnp.concatenate([jnp.full((rows, LANES), sink_ref[h], F32) for h in heads], axis=0)
        o4 = _attend(q4, [k], [v], sink_rep, [None])
        for g, h in enumerate(heads):
            outs[h] = o4[g * rows:(g + 1) * rows]
    o_ref[...] = jnp.concatenate(outs, axis=1).astype(o_ref.dtype)


def _attn_lat_kernel(sink_ref, q_ref, kp_ref, kc_ref, kn_ref, vp_ref, vc_ref, vn_ref, ck_ref, cv_ref,
                     o_ref, *, n_blk):
    qi = pl.program_id(1)
    rows = CHUNK
    r4 = Q_PER_KV * rows
    a = lax.broadcasted_iota(jnp.int32, (r4, CHUNK), 0) % rows
    s = lax.broadcasted_iota(jnp.int32, (r4, CHUNK), 1)
    mask_prev = (s >= a) & (qi > 0)
    mask_next = (s <= a) & (qi < n_blk - 1)
    outs = [None] * N_Q_HEADS
    for hk in range(N_KV_HEADS):
        sl = slice(hk * HEAD_DIM, (hk + 1) * HEAD_DIM)
        keys = [kp_ref[:, sl].astype(BF16), kc_ref[:, sl].astype(BF16), kn_ref[:, sl].astype(BF16),
                ck_ref[:, sl].astype(BF16)]
        vals = [vp_ref[:, sl].astype(BF16), vc_ref[:, sl].astype(BF16), vn_ref[:, sl].astype(BF16),
                cv_ref[:, sl].astype(BF16)]
        heads = [hk * Q_PER_KV + g for g in range(Q_PER_KV)]
        q4 = jnp.concatenate([q_ref[:, h * HEAD_DIM:(h + 1) * HEAD_DIM] for h in heads], axis=0).astype(BF16)
        sink_rep = jnp.concatenate([jnp.full((rows, LANES), sink_ref[h], F32) for h in heads], axis=0)
        o4 = _attend(q4, keys, vals, sink_rep, [mask_prev, None, mask_next, None])
        for g, h in enumerate(heads):
            outs[h] = o4[g * rows:(g + 1) * rows]
    o_ref[...] = jnp.concatenate(outs, axis=1).astype(o_ref.dtype)


def _attention(qkv, sinks, ctx_k, ctx_v):
    smem = pl.BlockSpec(memory_space=pltpu.SMEM)
    kcol, vcol = Q_DIM // KV_DIM, Q_DIM // KV_DIM + 1
    ctx_blk = CTX_LEN
    o_ctx = pl.pallas_call(
        _attn_ctx_kernel,
        grid=(N_CTX_SEQ,),
        in_specs=[
            smem,
            pl.BlockSpec((ctx_blk, Q_DIM), lambda b: (b, 0)),
            pl.BlockSpec((ctx_blk, KV_DIM), lambda b: (b, kcol)),
            pl.BlockSpec((ctx_blk, KV_DIM), lambda b: (b, vcol)),
        ],
        out_specs=pl.BlockSpec((ctx_blk, Q_DIM), lambda b: (b, 0)),
        out_shape=jax.ShapeDtypeStruct((CTX_TOK, Q_DIM), BF16),
        compiler_params=_cparams(("arbitrary",)),
        name="attn_ctx",
    )(sinks, qkv, qkv, qkv)

    n_blk = LAT_LEN // CHUNK
    base = CTX_TOK // CHUNK

    def rb(b, qi, off):
        return base + b * n_blk + jnp.clip(qi + off, 0, n_blk - 1)

    def kv_spec(col, off):
        return pl.BlockSpec((CHUNK, KV_DIM), lambda b, qi: (rb(b, qi, off), col))

    o_lat = pl.pallas_call(
        functools.partial(_attn_lat_kernel, n_blk=n_blk),
        grid=(N_LAT_SEQ, n_blk),
        in_specs=[
            smem,
            pl.BlockSpec((CHUNK, Q_DIM), lambda b, qi: (rb(b, qi, 0), 0)),
            kv_spec(kcol, -1), kv_spec(kcol, 0), kv_spec(kcol, 1),
            kv_spec(vcol, -1), kv_spec(vcol, 0), kv_spec(vcol, 1),
            pl.BlockSpec((None, CTX_LEN, KV_DIM), lambda b, qi: (b, 0, 0)),
            pl.BlockSpec((None, CTX_LEN, KV_DIM), lambda b, qi: (b, 0, 0)),
        ],
        out_specs=pl.BlockSpec((CHUNK, Q_DIM), lambda b, qi: (b * n_blk + qi, 0)),
        out_shape=jax.ShapeDtypeStruct((LAT_TOK, Q_DIM), BF16),
        compiler_params=_cparams(("arbitrary", "arbitrary")),
        name="attn_lat",
    )(sinks, qkv, qkv, qkv, qkv, qkv, qkv, qkv, ctx_k, ctx_v)
    return o_ctx, o_lat


def _softplus(x):
    return jnp.maximum(x, 0.0) + jnp.log1p(jnp.exp(-jnp.abs(x)))


SSM_TN = 1024
SSM_Z_STEPS = SSM_INNER // SSM_TN
SSM_C_STEPS = SSM_CONV_DIM // SSM_TN


def _ssm_in_kernel(x_ref, sh_ref, sc_ref, g_ref, w_ref, cw_ref, cb_ref, wdt_ref, bdt_ref,
                   z_ref, xbc_ref, dt_ref, h_ref, *, tm):
    i = pl.program_id(0)
    j = pl.program_id(1)

    @pl.when(j == 0)
    def _():
        h_ref[...] = _adaln(x_ref[...], g_ref[...], sh_ref[...], sc_ref[...]).astype(BF16)

    @pl.when(j < SSM_Z_STEPS)
    def _():
        z_ref[...] = jnp.dot(h_ref[...], w_ref[...].astype(BF16), preferred_element_type=F32).astype(BF16)

    @pl.when((j >= SSM_Z_STEPS) & (j < SSM_Z_STEPS + SSM_C_STEPS))
    def _():
        y = jnp.dot(h_ref[...], w_ref[...].astype(BF16), preferred_element_type=F32)
        seq_len = jnp.where(i * tm < CTX_TOK, CTX_LEN, LAT_LEN)
        pos = lax.broadcasted_iota(jnp.int32, (tm, 1), 0) & (seq_len - 1)
        up = jnp.where(pos == 0, 0.0, pltpu.roll(y, 1, axis=0))
        dn = jnp.where(pos == seq_len - 1, 0.0, pltpu.roll(y, tm - 1, axis=0))
        c = up * cw_ref[0:1, :] + y * cw_ref[1:2, :] + dn * cw_ref[2:3, :] + cb_ref[...]
        xbc_ref[...] = _silu(c).astype(BF16)

    @pl.when(j == SSM_Z_STEPS + SSM_C_STEPS)
    def _():
        y = jnp.dot(h_ref[...], wdt_ref[...].astype(BF16), preferred_element_type=F32)
        dt_ref[...] = _softplus(y + bdt_ref[...])


def _ssm_in(x, mods, gains, w, widx, conv_w, conv_b, w_dt, b_dt, layer, tm=1024):
    tn = SSM_TN
    n_main = SSM_Z_STEPS + SSM_C_STEPS

    def conv_col(j):
        return jnp.clip(j - SSM_Z_STEPS, 0, SSM_C_STEPS - 1)

    return pl.pallas_call(
        functools.partial(_ssm_in_kernel, tm=tm),
        grid=(TOK // tm, n_main + 1),
        in_specs=[
            pl.BlockSpec((tm, D), lambda i, j: (i, 0)),
            _mod_spec(layer, 3, tm, 2),
            _mod_spec(layer, 4, tm, 2),
            _gain_spec(layer, 2, 2),
            pl.BlockSpec((None, D, tn), lambda i, j: (widx, 0, jnp.minimum(j, n_main - 1))),
            pl.BlockSpec((3, tn), lambda i, j: (0, conv_col(j))),
            pl.BlockSpec((1, tn), lambda i, j: (0, conv_col(j))),
            pl.BlockSpec((D, LANES), lambda i, j: (0, 0)),
            pl.BlockSpec((1, LANES), lambda i, j: (0, 0)),
        ],
        out_specs=[
            pl.BlockSpec((tm, tn), lambda i, j: (i, jnp.minimum(j, SSM_Z_STEPS - 1))),
            pl.BlockSpec((tm, tn), lambda i, j: (i, conv_col(j))),
            pl.BlockSpec((tm, LANES), lambda i, j: (i, 0)),
        ],
        out_shape=[jax.ShapeDtypeStruct((TOK, SSM_INNER), BF16), jax.ShapeDtypeStruct((TOK, SSM_CONV_DIM), BF16),
                   jax.ShapeDtypeStruct((TOK, LANES), F32)],
        scratch_shapes=[pltpu.VMEM((tm, D), BF16)],
        compiler_params=_cparams(("arbitrary", "arbitrary")),
        name=f"ssm_in_l{layer}",
    )(x, mods, mods, gains, w, conv_w, conv_b, w_dt, b_dt)


def _split3(q):
    hi = q.astype(BF16)
    r1 = q - hi.astype(F32)
    mid = r1.astype(BF16)
    lo = (r1 - mid.astype(F32)).astype(BF16)
    return hi, mid, lo


def _spread(q, r_ref):
    hi, mid, lo = _split3(q[:, :SSM_HEADS])
    return jnp.dot(jnp.concatenate([hi, mid, lo], axis=1), r_ref[...], preferred_element_type=F32)


def _spread_consts():
    r = jnp.arange(3 * SSM_HEADS) % SSM_HEADS
    r1 = (jnp.arange(SSM_HEADS * SSM_P)[None, :] // SSM_P == r[:, None]).astype(BF16)
    r2 = (jnp.arange(SSM_HEADS * CHUNK)[None, :] // CHUNK == r[:, None]).astype(BF16)
    return r1, r2


def _ssd_chain(xbc, dt_all, a_ref, r1_ref, r2_ref, hs_ref, bwd):
    L = CHUNK
    d = 1 if bwd else 0
    xs = xbc[:, :SSM_INNER].astype(F32)
    bcb = xbc[:, SSM_INNER:]

    dt = pltpu.roll(dt_all, LANES - SSM_HEADS, axis=1) if bwd else dt_all
    a_row = a_ref[d:d + 1, :]
    ii = lax.broadcasted_iota(jnp.int32, (L, L), 0)
    jj = lax.broadcasted_iota(jnp.int32, (L, L), 1)
    tri = (jj >= ii) if bwd else (ii >= jj)
    tri_b = jnp.where(tri, 1.0, 0.0).astype(BF16)
    neg_mask = jnp.where(tri, 0.0, -jnp.inf)
    hi_p, mid_p, lo_p = _split3(dt * a_row)
    acum = (jnp.dot(tri_b, hi_p, preferred_element_type=F32)
            + jnp.dot(tri_b, mid_p, preferred_element_type=F32)
            + jnp.dot(tri_b, lo_p, preferred_element_type=F32))
    end = 0 if bwd else L - 1
    total = acum[end:end + 1, :]
    acum_t = acum.T
    cdec_t = jnp.exp(acum_t[:, end:end + 1])

    e_acum = _spread(acum, r2_ref)
    e_dt = _spread(dt, r1_ref)
    e_end = _spread(dt * jnp.exp(total - acum), r1_ref)
    e_in = _spread(jnp.exp(acum), r1_ref)
    xdt = (xs * e_dt).astype(BF16)
    xw = (xs * e_end).astype(BF16)
    left = lax.broadcasted_iota(jnp.int32, (L, LANES), 1) < SSM_P

    y_parts = []
    for g in range(SSM_GROUPS):
        bm = bcb[:, g * SSM_STATE:(g + 1) * SSM_STATE]
        cm = bcb[:, SSM_GN + g * SSM_STATE:SSM_GN + (g + 1) * SSM_STATE]
        cb = lax.dot_general(cm, bm, (((1,), (1,)), ((), ())), preferred_element_type=F32)
        gsl = slice(g * SSM_HG * SSM_P, (g + 1) * SSM_HG * SSM_P)
        hprev = hs_ref[d, gsl, :]
        y_off = lax.dot_general(cm, hprev.astype(BF16), (((1,), (1,)), ((), ())),
                                preferred_element_type=F32)
        y_pairs = []
        for pr in range(SSM_HG // 2):
            wm = []
            for h in (g * SSM_HG + 2 * pr, g * SSM_HG + 2 * pr + 1):
                seg = e_acum[:, h * L:(h + 1) * L] - acum_t[h:h + 1, :] + neg_mask
                wm.append((cb * jnp.exp(seg)).astype(BF16))
            pair = xdt[:, (g * SSM_HG + 2 * pr) * SSM_P:(g * SSM_HG + 2 * pr + 2) * SSM_P]
            rhs = jnp.concatenate([jnp.where(left, pair, jnp.zeros_like(pair)),
                                   jnp.where(left, jnp.zeros_like(pair), pair)], axis=0)
            y_pairs.append(jnp.dot(jnp.concatenate(wm, axis=1), rhs, preferred_element_type=F32))
        y_parts.append(jnp.concatenate(y_pairs, axis=1) + y_off * e_in[:, gsl])
        st = lax.dot_general(xw[:, gsl], bm, (((0,), (0,)), ((), ())), preferred_element_type=F32)
        dec = jnp.concatenate([jnp.broadcast_to(cdec_t[g * SSM_HG + k:g * SSM_HG + k + 1, :], (SSM_P, 1))
                               for k in range(SSM_HG)], axis=0)
        hs_ref[d, gsl, :] = dec * hprev + st
    return jnp.concatenate(y_parts, axis=1), xs


def _ssd_kernel(*refs, nc, has_h0, emit_state):
    (xa_ref, xb_ref, dta_ref, dtb_ref, za_ref, zb_ref, a_ref, dsk_ref, ng_ref, r1_ref, r2_ref) = refs[:11]
    pos = 11
    h0_ref = None
    if has_h0:
        h0_ref = refs[pos]
        pos += 1
    y_ref = refs[pos]
    pos += 1
    st_ref = None
    if emit_state:
        st_ref = refs[pos]
        pos += 1
    hs_ref, yacc_ref = refs[pos:pos + 2]

    s = pl.program_id(1)
    L = CHUNK
    half = nc // 2
    off_a = pl.multiple_of(s * L, L)
    off_b = pl.multiple_of((nc - 1 - s) * L, L)

    @pl.when(s == 0)
    def _():
        hs_ref[...] = h0_ref[...] if has_h0 else jnp.zeros_like(hs_ref)

    ya, xs_a = _ssd_chain(xa_ref[...], dta_ref[...], a_ref, r1_ref, r2_ref, hs_ref, bwd=False)
    ya = ya + dsk_ref[...] * xs_a
    yb, _ = _ssd_chain(xb_ref[...], dtb_ref[...], a_ref, r1_ref, r2_ref, hs_ref, bwd=True)

    @pl.when(s < half)
    def _():
        yacc_ref[pl.ds(off_a, L), :] = ya
        yacc_ref[pl.ds(off_b, L), :] = yb

    @pl.when(s >= half)
    def _():
        for off, y_new, z_ref in ((off_a, ya, za_ref), (off_b, yb, zb_ref)):
            yt = (yacc_ref[pl.ds(off, L), :] + y_new) * _silu(z_ref[...].astype(F32))
            y_ref[pl.ds(off, L), :] = _rms(yt, ng_ref[...]).astype(y_ref.dtype)

    if emit_state:
        @pl.when(s == nc - 1)
        def _():
            st_ref[...] = hs_ref[...]


def _ssd(z, xbc, dt, a_pad, dsk, norm_g, r1, r2, h0, *, seq0, n_seq, seq_len, emit_state):
    nc = seq_len // CHUNK
    half = nc // 2
    chunk0 = seq0 // CHUNK

    def fwd_chunk(b, s):
        return chunk0 + b * nc + s

    def bwd_chunk(b, s):
        return chunk0 + b * nc + nc - 1 - s

    def late(s):
        return jnp.maximum(s, half)

    const = lambda b, s: (0, 0)
    in_specs = [
        pl.BlockSpec((CHUNK, SSM_CONV_DIM), lambda b, s: (fwd_chunk(b, s), 0)),
        pl.BlockSpec((CHUNK, SSM_CONV_DIM), lambda b, s: (bwd_chunk(b, s), 0)),
        pl.BlockSpec((CHUNK, LANES), lambda b, s: (fwd_chunk(b, s), 0)),
        pl.BlockSpec((CHUNK, LANES), lambda b, s: (bwd_chunk(b, s), 0)),
        pl.BlockSpec((CHUNK, SSM_INNER), lambda b, s: (fwd_chunk(b, late(s)), 0)),
        pl.BlockSpec((CHUNK, SSM_INNER), lambda b, s: (bwd_chunk(b, late(s)), 0)),
        pl.BlockSpec((2, LANES), const),
        pl.BlockSpec((1, SSM_INNER), const),
        pl.BlockSpec((1, SSM_INNER), const),
        pl.BlockSpec(r1.shape, const),
        pl.BlockSpec(r2.shape, const),
    ]
    args = [xbc, xbc, dt, dt, z, z, a_pad, dsk, norm_g, r1, r2]
    if h0 is not None:
        in_specs.append(pl.BlockSpec((None, 2, SSM_INNER, SSM_STATE), lambda b, s: (b, 0, 0, 0)))
        args.append(h0)
    out_specs = [pl.BlockSpec((seq_len, SSM_INNER), lambda b, s: (b, 0))]
    out_shape = [jax.ShapeDtypeStruct((n_seq * seq_len, SSM_INNER), BF16)]
    if emit_state:
        out_specs.append(pl.BlockSpec((None, 2, SSM_INNER, SSM_STATE), lambda b, s: (b, 0, 0, 0)))
        out_shape.append(jax.ShapeDtypeStruct((n_seq, 2, SSM_INNER, SSM_STATE), F32))
    return pl.pallas_call(
        functools.partial(_ssd_kernel, nc=nc, has_h0=h0 is not None, emit_state=emit_state),
        grid=(n_seq, nc),
        in_specs=in_specs,
        out_specs=out_specs,
        out_shape=out_shape,
        scratch_shapes=[pltpu.VMEM((2, SSM_INNER, SSM_STATE), F32), pltpu.VMEM((seq_len, SSM_INNER), F32)],
        compiler_params=_cparams(("arbitrary", "arbitrary")),
        name=f"ssd_{seq_len}",
    )(*args)


def kernel(x_prompt, x_sample, cache_k, cache_v, state_ssm, c, c_ctx, w_mod, b_mod, norm_g, ffn_in, ffn_out,
           gmlp_in, gmlp_ln_g, gmlp_ln_b, gmlp_ws, gmlp_bs, gmlp_out, attn_qkv, attn_sink, attn_out,
           ssm_in, ssm_conv_w, ssm_conv_b, ssm_dt_bias, ssm_a_log, ssm_d, ssm_norm, ssm_out):
    cond = jnp.concatenate([c_ctx[None, :], c, jnp.zeros((MOD_ROWS - 1 - N_LAT_SEQ, D), F32)], axis=0)
    mods = _modulation(cond, w_mod, b_mod).reshape(DEPTH * MOD_ROWS * N_MOD, 1, D)
    gains = norm_g.reshape(DEPTH * 6, 1, D)

    x = (x_prompt.reshape(CTX_TOK, D), x_sample.reshape(LAT_TOK, D))
    new_k = new_v = new_s = None
    for layer in range(DEPTH):
        x = _ffn(x, mods, gains, ffn_in, ffn_out, layer, 0, split_in=layer == 0)
        kind, j = layer % 3, layer // 3
        if kind == 0:
            x = _gmlp(x, mods, gains, gmlp_in, gmlp_ln_g[:, None, :], gmlp_ln_b[:, None, :], gmlp_ws,
                      gmlp_bs[..., None], gmlp_out, j, layer)
        elif kind == 1:
            qkv = _qkv_proj(x, mods, gains, attn_qkv, j, layer)
            kv = qkv[:CTX_TOK, Q_DIM:].reshape(N_CTX_SEQ, CTX_LEN, 2, N_KV_HEADS, HEAD_DIM)
            new_k, new_v = kv[:, :, 0], kv[:, :, 1]
            o_ctx, o_lat = _attention(qkv, attn_sink[j],
                                      cache_k[:, j].reshape(N_LAT_SEQ, CTX_LEN, KV_DIM),
                                      cache_v[:, j].reshape(N_LAT_SEQ, CTX_LEN, KV_DIM))
            x = _out_proj(o_ctx, o_lat, attn_out, j, x, mods, gains, layer)
        else:
            pad = LANES - 2 * SSM_HEADS
            w_dt = jnp.pad(ssm_in[j][:, SSM_MAIN:], ((0, 0), (0, pad)))
            b_dt = jnp.pad(ssm_dt_bias[j].reshape(1, 2 * SSM_HEADS), ((0, 0), (0, pad)))
            z, xbc, dt = _ssm_in(x, mods, gains, ssm_in, j, ssm_conv_w[j], ssm_conv_b[j][None, :],
                                 w_dt, b_dt, layer)
            a_pad = jnp.pad(-jnp.exp(ssm_a_log[j]), ((0, 0), (0, LANES - SSM_HEADS)))
            dsk = jnp.repeat(ssm_d[j], SSM_P)[None, :]
            ng = ssm_norm[j][None, :]
            r1, r2 = _spread_consts()
            y_ctx, st = _ssd(z, xbc, dt, a_pad, dsk, ng, r1, r2, None,
                             seq0=0, n_seq=N_CTX_SEQ, seq_len=CTX_LEN, emit_state=True)
            (y_lat,) = _ssd(z, xbc, dt, a_pad, dsk, ng, r1, r2,
                            state_ssm[:, j].reshape(N_LAT_SEQ, 2, SSM_INNER, SSM_STATE),
                            seq0=CTX_TOK, n_seq=N_LAT_SEQ, seq_len=LAT_LEN, emit_state=False)
            new_s = st.reshape(N_CTX_SEQ, 2, SSM_HEADS, SSM_P, SSM_STATE)
            x = _out_proj(y_ctx, y_lat, ssm_out, j, x, mods, gains, layer)
        x = _ffn(x, mods, gains, ffn_in, ffn_out, layer, 1, split_out=layer == DEPTH - 1)

    y_prompt = x[0].reshape(N_CTX_SEQ, CTX_LEN, D)
    y_sample = x[1].reshape(N_LAT_SEQ, LAT_LEN, D)
    return (y_prompt, y_sample, new_k[:, None], new_v[:, None], new_s[:, None])
```

```python
import functools
import math

import jax
import jax.numpy as jnp
from jax import lax
from jax.experimental import pallas as pl
from jax.experimental.pallas import tpu as pltpu

F32 = jnp.float32
BF16 = jnp.bfloat16

D = 1024
N_CTX_SEQ, CTX_LEN = 16, 256
N_LAT_SEQ, LAT_LEN = 4, 1024
CTX_TOK = N_CTX_SEQ * CTX_LEN
LAT_TOK = N_LAT_SEQ * LAT_LEN
TOK = CTX_TOK + LAT_TOK
DEPTH = 4
N_MOD = 9
MOD_ROWS = 8
D_FF = 2816
EPS = 1e-6
GRID_W = 64
GMLP_HALF = 3 * D
GMLP_GROUPS = 8
GMLP_GD = GMLP_HALF // GMLP_GROUPS
CHUNK = 128
HEAD_DIM = 64
N_Q_HEADS = 16
N_KV_HEADS = 4
Q_PER_KV = 4
Q_DIM = N_Q_HEADS * HEAD_DIM
KV_DIM = N_KV_HEADS * HEAD_DIM
QKV_DIM = Q_DIM + 2 * KV_DIM
ATTN_SCALE = HEAD_DIM ** -0.5
ROPE_BASE = 10000.0
ROT_PAIRS = HEAD_DIM // 4
NEG_INF = -1e30
SSM_INNER = 2 * D
SSM_HEADS = 32
SSM_P = 64
SSM_GROUPS = 4
SSM_HG = SSM_HEADS // SSM_GROUPS
SSM_STATE = 128
SSM_GN = SSM_GROUPS * SSM_STATE
SSM_CONV_DIM = SSM_INNER + 2 * SSM_GN
SSM_MAIN = SSM_INNER + SSM_CONV_DIM
LANES = 128
HALO = 8

VMEM_LIMIT = 56 * 1024 * 1024
VMEM_LIMIT_BIG = 60 * 1024 * 1024


def _cparams(sem, limit=VMEM_LIMIT):
    return pltpu.CompilerParams(dimension_semantics=sem, vmem_limit_bytes=limit)


def _sigmoid(x):
    return 1.0 / (1.0 + jnp.exp(-x))


def _silu(x):
    return x * _sigmoid(x)


def _rms(x, g):
    return x * lax.rsqrt(jnp.mean(x * x, axis=-1, keepdims=True) + EPS) * g


def _adaln(x, g, shift, scale):
    return _rms(x, g) * (1.0 + scale) + shift


def _mod_row(i, tm):
    t0 = i * tm
    return jnp.where(t0 < CTX_TOK, 0, 1 + (t0 - CTX_TOK) // LAT_LEN)


def _mod_spec(layer, k, tm, grid_rank):
    base = layer * MOD_ROWS * N_MOD + k
    if grid_rank == 1:
        return pl.BlockSpec((None, 1, D), lambda i: (base + _mod_row(i, tm) * N_MOD, 0, 0))
    return pl.BlockSpec((None, 1, D), lambda i, j: (base + _mod_row(i, tm) * N_MOD, 0, 0))


def _gain_spec(layer, k, grid_rank):
    idx = layer * 6 + k
    if grid_rank == 1:
        return pl.BlockSpec((None, 1, D), lambda i: (idx, 0, 0))
    return pl.BlockSpec((None, 1, D), lambda i, j: (idx, 0, 0))


def _split_specs(tm, width, grid_rank):
    na = CTX_TOK // tm
    if grid_rank == 1:
        return (pl.BlockSpec((tm, width), lambda i: (jnp.minimum(i, na - 1), 0)),
                pl.BlockSpec((tm, width), lambda i: (jnp.maximum(i - na, 0), 0)))
    return (pl.BlockSpec((tm, width), lambda i, j: (jnp.minimum(i, na - 1), 0)),
            pl.BlockSpec((tm, width), lambda i, j: (jnp.maximum(i - na, 0), 0)))


def _mod_kernel(cond_ref, w_ref, b_ref, o_ref):
    s = _silu(cond_ref[...]).astype(BF16)
    o_ref[...] = jnp.dot(s, w_ref[...].astype(BF16), preferred_element_type=F32) + b_ref[...]


def _modulation(cond, w_mod, b_mod):
    tn = 2304
    n = N_MOD * D
    return pl.pallas_call(
        _mod_kernel,
        grid=(DEPTH, n // tn),
        in_specs=[
            pl.BlockSpec((MOD_ROWS, D), lambda l, j: (0, 0)),
            pl.BlockSpec((None, D, tn), lambda l, j: (l, 0, j)),
            pl.BlockSpec((None, 1, tn), lambda l, j: (l, 0, j)),
        ],
        out_specs=pl.BlockSpec((None, MOD_ROWS, tn), lambda l, j: (l, 0, j)),
        out_shape=jax.ShapeDtypeStruct((DEPTH, MOD_ROWS, n), F32),
        compiler_params=_cparams(("arbitrary", "arbitrary")),
        name="modulation",
    )(cond, w_mod, b_mod.reshape(DEPTH, 1, n))


FFN_TF = 256
FFN_SLABS = D_FF // FFN_TF
FFN_SPS = 3
FFN_STEP_SLABS = (3, 2, 3, 3)
FFN_STEPS = len(FFN_STEP_SLABS)
assert sum(FFN_STEP_SLABS) == FFN_SLABS and max(FFN_STEP_SLABS) == FFN_SPS


def _ffn_kernel(*refs, tm, split_in, split_out):
    n_x = 2 if split_in else 1
    n_o = 2 if split_out else 1
    x_refs = refs[:n_x]
    sh_ref, sc_ref, gt_ref, g0_ref, g1_ref = refs[n_x:n_x + 5]
    w0 = n_x + 5
    wg = refs[w0:w0 + FFN_SPS]
    wu = refs[w0 + FFN_SPS:w0 + 2 * FFN_SPS]
    wo = refs[w0 + 2 * FFN_SPS:w0 + 3 * FFN_SPS]
    o_refs = refs[w0 + 3 * FFN_SPS:w0 + 3 * FFN_SPS + n_o]
    h_ref, acc_ref = refs[w0 + 3 * FFN_SPS + n_o:]
    i = pl.program_id(0)
    j = pl.program_id(1)
    is_ctx = i * tm < CTX_TOK
    last = FFN_STEPS - 1

    def load_x():
        if split_in:
            return jnp.where(is_ctx, x_refs[0][...], x_refs[1][...])
        return x_refs[0][...]

    def swiglu_slabs(h, n):
        acts = []
        for k in range(n):
            g = jnp.dot(h, wg[k][...].astype(BF16), preferred_element_type=F32)
            u = jnp.dot(h, wu[k][...].astype(BF16), preferred_element_type=F32)
            acts.append((_silu(g) * u).astype(BF16))
        w = jnp.concatenate([wo[k][...].astype(BF16) for k in range(n)], axis=0)
        return jnp.dot(jnp.concatenate(acts, axis=1), w, preferred_element_type=F32)

    @pl.when(j == 0)
    def _():
        h = _adaln(load_x(), g0_ref[...], sh_ref[...], sc_ref[...]).astype(BF16)
        h_ref[...] = h
        acc_ref[...] = swiglu_slabs(h, FFN_STEP_SLABS[0])

    for step in range(1, last):
        @pl.when(j == step)
        def _(step=step):
            acc_ref[...] += swiglu_slabs(h_ref[...], FFN_STEP_SLABS[step])

    @pl.when(j == last)
    def _():
        f = acc_ref[...] + swiglu_slabs(h_ref[...], FFN_STEP_SLABS[last])
        res = load_x() + (0.5 * gt_ref[...]) * _rms(f, g1_ref[...])
        if split_out:
            @pl.when(is_ctx)
            def _():
                o_refs[0][...] = res

            @pl.when(jnp.logical_not(is_ctx))
            def _():
                o_refs[1][...] = res
        else:
            o_refs[0][...] = res


def _ffn(x, mods, gains, ffn_in, ffn_out, layer, which, split_in=False, split_out=False):
    tm, tf = 1024, FFN_TF
    k0 = 0 if which == 0 else 6
    gi = 0 if which == 0 else 4
    single = pl.Buffered(1)
    if split_in:
        x_specs = list(_split_specs(tm, D, 2))
    else:
        x_specs = [pl.BlockSpec((tm, D), lambda i, j: (i, 0))]
    x_args = list(x) if split_in else [x]
    if split_out:
        sa, sb = _split_specs(tm, D, 2)
        out_specs = [pl.BlockSpec(sa.block_shape, sa.index_map, pipeline_mode=single),
                     pl.BlockSpec(sb.block_shape, sb.index_map, pipeline_mode=single)]
        out_shape = [jax.ShapeDtypeStruct((CTX_TOK, D), F32), jax.ShapeDtypeStruct((LAT_TOK, D), F32)]
    else:
        out_specs = pl.BlockSpec((tm, D), lambda i, j: (i, 0))
        out_shape = jax.ShapeDtypeStruct((TOK, D), F32)

    starts = [sum(FFN_STEP_SLABS[:s]) for s in range(FFN_STEPS)]
    table = []
    for k in range(FFN_SPS):
        col = [starts[s] + k if k < FFN_STEP_SLABS[s] else None for s in range(FFN_STEPS)]
        for s in reversed(range(FFN_STEPS)):
            if col[s] is None:
                col[s] = col[s + 1] if s + 1 < FFN_STEPS else col[s - 1]
        table.append(col)

    def slab(j, k):
        idx = table[k][FFN_STEPS - 1]
        for s in reversed(range(FFN_STEPS - 1)):
            idx = jnp.where(j == s, table[k][s], idx)
        return idx

    def spec_in(k, col0):
        return pl.BlockSpec((None, None, D, tf), lambda i, j: (layer, which, 0, col0 + slab(j, k)))

    def spec_out(k):
        return pl.BlockSpec((None, None, tf, D), lambda i, j: (layer, which, slab(j, k), 0))

    w_specs = ([spec_in(k, 0) for k in range(FFN_SPS)] + [spec_in(k, FFN_SLABS) for k in range(FFN_SPS)]
               + [spec_out(k) for k in range(FFN_SPS)])
    return pl.pallas_call(
        functools.partial(_ffn_kernel, tm=tm, split_in=split_in, split_out=split_out),
        grid=(TOK // tm, FFN_STEPS),
        in_specs=x_specs + [
            _mod_spec(layer, k0 + 0, tm, 2),
            _mod_spec(layer, k0 + 1, tm, 2),
            _mod_spec(layer, k0 + 2, tm, 2),
            _gain_spec(layer, gi, 2),
            _gain_spec(layer, gi + 1, 2),
        ] + w_specs,
        out_specs=out_specs,
        out_shape=out_shape,
        scratch_shapes=[pltpu.VMEM((tm, D), BF16), pltpu.VMEM((tm, D), F32)],
        compiler_params=_cparams(("arbitrary", "arbitrary"), VMEM_LIMIT_BIG),
        name=f"ffn_l{layer}_{which}",
    )(*x_args, mods, mods, mods, gains, gains, *([ffn_in] * (2 * FFN_SPS)), *([ffn_out] * FFN_SPS))


def _out_kernel(aa_ref, ab_ref, w_ref, x_ref, gt_ref, g_ref, o_ref, *, tm):
    a = jnp.where(pl.program_id(0) * tm < CTX_TOK, aa_ref[...], ab_ref[...])
    m = jnp.dot(a, w_ref[...].astype(BF16), preferred_element_type=F32)
    o_ref[...] = x_ref[...] + gt_ref[...] * _rms(m, g_ref[...])


def _out_proj(a_ctx, a_lat, w, widx, x, mods, gains, layer, tm=512):
    k = a_ctx.shape[1]
    sa, sb = _split_specs(tm, k, 1)
    return pl.pallas_call(
        functools.partial(_out_kernel, tm=tm),
        grid=(TOK // tm,),
        in_specs=[
            sa, sb,
            pl.BlockSpec((None, k, D), lambda i: (widx, 0, 0)),
            pl.BlockSpec((tm, D), lambda i: (i, 0)),
            _mod_spec(layer, 5, tm, 1),
            _gain_spec(layer, 3, 1),
        ],
        out_specs=pl.BlockSpec((tm, D), lambda i: (i, 0)),
        out_shape=jax.ShapeDtypeStruct((TOK, D), F32),
        compiler_params=_cparams(("arbitrary",)),
        name=f"outproj_l{layer}",
    )(a_ctx, a_lat, w, x, mods, gains)


GMLP_GPS = 2
GMLP_SLABS = GMLP_GROUPS // GMLP_GPS
GMLP_COLS = GMLP_GPS * GMLP_GD


def _gelu_exact(x):
    return 0.5 * x * (1.0 + lax.erf(x * (1.0 / math.sqrt(2.0))))


def _fold_lanes(v):
    acc = v[:, :LANES]
    for cb in range(1, v.shape[1] // LANES):
        acc = acc + v[:, cb * LANES:(cb + 1) * LANES]
    return acc


def _gmlp_kernel(x_ref, sh_ref, sc_ref, g2_ref, win_ref, lg_ref, lb_ref, ws_ref, bs_ref, wo_ref,
                 gt_ref, g3_ref, o_ref, h_ref, v_ref, c_ref, s1_ref, s2_ref, mu_ref, rstd_ref, *, tm):
    j = pl.program_id(1)
    last = 2 * GMLP_SLABS - 1

    def slab(h):
        return _gelu_exact(jnp.dot(h, win_ref[...].astype(BF16), preferred_element_type=F32))

    def park(y, first):
        if first:
            c_ref[...] = jnp.sum(_fold_lanes(y), axis=-1, keepdims=True) * (1.0 / GMLP_COLS)
        dv = y - c_ref[...]
        if first:
            s1_ref[...] = _fold_lanes(dv)
            s2_ref[...] = _fold_lanes(dv * dv)
        else:
            s1_ref[...] += _fold_lanes(dv)
            s2_ref[...] += _fold_lanes(dv * dv)
        v_ref[j] = y

    def mix(u, first):
        s = j - GMLP_SLABS
        vn = ((v_ref[s] - mu_ref[...]) * rstd_ref[...] * lg_ref[...] + lb_ref[...]).astype(BF16)
        rows = []
        for c in range(tm // CHUNK):
            cols = []
            for gg in range(GMLP_GPS):
                vg = vn[c * CHUNK:(c + 1) * CHUNK, gg * GMLP_GD:(gg + 1) * GMLP_GD]
                cols.append(jnp.dot(ws_ref[gg].astype(BF16), vg, preferred_element_type=F32) + bs_ref[gg])
            rows.append(jnp.concatenate(cols, axis=1))
        a = (u * jnp.concatenate(rows, axis=0)).astype(BF16)
        t = jnp.dot(a, wo_ref[...].astype(BF16), preferred_element_type=F32)
        return t if first else o_ref[...] + t

    @pl.when(j == 0)
    def _():
        h = _adaln(x_ref[...], g2_ref[...], sh_ref[...], sc_ref[...]).astype(BF16)
        h_ref[...] = h
        park(slab(h), True)

    @pl.when((j > 0) & (j < GMLP_SLABS))
    def _():
        park(slab(h_ref[...]), False)

    @pl.when(j == GMLP_SLABS)
    def _():
        d1 = jnp.sum(s1_ref[...], axis=-1, keepdims=True) * (1.0 / GMLP_HALF)
        d2 = jnp.sum(s2_ref[...], axis=-1, keepdims=True) * (1.0 / GMLP_HALF)
        mu_ref[...] = c_ref[...] + d1
        rstd_ref[...] = lax.rsqrt(d2 - d1 * d1 + EPS)
        o_ref[...] = mix(slab(h_ref[...]), True)

    @pl.when((j > GMLP_SLABS) & (j < last))
    def _():
        o_ref[...] = mix(slab(h_ref[...]), False)

    @pl.when(j == last)
    def _():
        m = mix(slab(h_ref[...]), False)
        o_ref[...] = x_ref[...] + gt_ref[...] * _rms(m, g3_ref[...])


def _gmlp(x, mods, gains, w_in, ln_g, ln_b, w_s, b_s, w_out, widx, layer, tm=1024):
    ns = GMLP_SLABS

    def u_slab(j):
        return jnp.maximum(j - ns, 0)

    single = pl.Buffered(1)
    return pl.pallas_call(
        functools.partial(_gmlp_kernel, tm=tm),
        grid=(TOK // tm, 2 * ns),
        in_specs=[
            pl.BlockSpec((tm, D), lambda i, j: (i, 0), pipeline_mode=single),
            _mod_spec(layer, 3, tm, 2),
            _mod_spec(layer, 4, tm, 2),
            _gain_spec(layer, 2, 2),
            pl.BlockSpec((None, D, GMLP_COLS), lambda i, j: (widx, 0, jnp.where(j < ns, ns + j, j - ns))),
            pl.BlockSpec((None, 1, GMLP_COLS), lambda i, j: (widx, 0, u_slab(j))),
            pl.BlockSpec((None, 1, GMLP_COLS), lambda i, j: (widx, 0, u_slab(j))),
            pl.BlockSpec((None, GMLP_GPS, CHUNK, CHUNK), lambda i, j: (widx, u_slab(j), 0, 0)),
            pl.BlockSpec((None, GMLP_GPS, CHUNK, 1), lambda i, j: (widx, u_slab(j), 0, 0)),
            pl.BlockSpec((None, GMLP_COLS, D), lambda i, j: (widx, u_slab(j), 0)),
            _mod_spec(layer, 5, tm, 2),
            _gain_spec(layer, 3, 2),
        ],
        out_specs=pl.BlockSpec((tm, D), lambda i, j: (i, 0)),
        out_shape=jax.ShapeDtypeStruct((TOK, D), F32),
        scratch_shapes=[pltpu.VMEM((tm, D), BF16), pltpu.VMEM((ns, tm, GMLP_COLS), F32),
                        pltpu.VMEM((tm, 1), F32), pltpu.VMEM((tm, LANES), F32), pltpu.VMEM((tm, LANES), F32),
                        pltpu.VMEM((tm, 1), F32), pltpu.VMEM((tm, 1), F32)],
        compiler_params=_cparams(("arbitrary", "arbitrary"), VMEM_LIMIT_BIG),
        name=f"gmlp_l{layer}",
    )(x, mods, mods, gains, w_in, ln_g, ln_b, w_s, b_s, w_out, mods, gains)


def _rope_tables():
    pos = jnp.arange(LAT_LEN)
    pos_r = (pos // GRID_W).astype(F32)
    pos_c = (pos % GRID_W).astype(F32)
    inv = ROPE_BASE ** (-jnp.arange(ROT_PAIRS, dtype=F32) / ROT_PAIRS)
    ang_r = pos_r[:, None] * inv
    ang_c = pos_c[:, None] * inv
    cos = jnp.concatenate([jnp.cos(ang_r)] * 2 + [jnp.cos(ang_c)] * 2, axis=1)
    sin = jnp.concatenate([-jnp.sin(ang_r), jnp.sin(ang_r), -jnp.sin(ang_c), jnp.sin(ang_c)], axis=1)
    reps = LANES // HEAD_DIM
    return jnp.tile(cos, (1, reps)), jnp.tile(sin, (1, reps))


def _qkv_kernel(x_ref, sh_ref, sc_ref, g_ref, w_ref, cos_ref, sin_ref, o_ref, kc_ref, vc_ref, *, tm):
    i = pl.program_id(0)
    h = _adaln(x_ref[...], g_ref[...], sh_ref[...], sc_ref[...])
    y = jnp.dot(h.astype(BF16), w_ref[...].astype(BF16), preferred_element_type=F32)

    @pl.when(i * tm < CTX_TOK)
    def _():
        o_ref[...] = y
        kc_ref[...] = y[:, Q_DIM:Q_DIM + KV_DIM]
        vc_ref[...] = y[:, Q_DIM + KV_DIM:]

    @pl.when(i * tm >= CTX_TOK)
    def _():
        cos = cos_ref[...]
        sin = sin_ref[...]
        lane = lax.broadcasted_iota(jnp.int32, (tm, LANES), 1)
        first = (lane % (2 * ROT_PAIRS)) < ROT_PAIRS
        for cb in range((Q_DIM + KV_DIM) // LANES):
            t = y[:, cb * LANES:(cb + 1) * LANES]
            partner = jnp.where(first, pltpu.roll(t, LANES - ROT_PAIRS, axis=1),
                                pltpu.roll(t, ROT_PAIRS, axis=1))
            o_ref[:, cb * LANES:(cb + 1) * LANES] = t * cos + partner * sin
        o_ref[:, Q_DIM + KV_DIM:] = y[:, Q_DIM + KV_DIM:]


def _qkv_proj(x, mods, gains, w, widx, layer, tm=512):
    cos, sin = _rope_tables()
    per_seq = LAT_LEN // tm
    tab = pl.BlockSpec((tm, LANES), lambda i: (jnp.maximum(i - CTX_TOK // tm, 0) % per_seq, 0))
    return pl.pallas_call(
        functools.partial(_qkv_kernel, tm=tm),
        grid=(TOK // tm,),
        in_specs=[
            pl.BlockSpec((tm, D), lambda i: (i, 0)),
            _mod_spec(layer, 3, tm, 1),
            _mod_spec(layer, 4, tm, 1),
            _gain_spec(layer, 2, 1),
            pl.BlockSpec((None, D, QKV_DIM), lambda i: (widx, 0, 0)),
            tab, tab,
        ],
        out_specs=[pl.BlockSpec((tm, QKV_DIM), lambda i: (i, 0)),
                   _split_specs(tm, KV_DIM, 1)[0], _split_specs(tm, KV_DIM, 1)[0]],
        out_shape=[jax.ShapeDtypeStruct((TOK, QKV_DIM), F32),
                   jax.ShapeDtypeStruct((CTX_TOK, KV_DIM), F32), jax.ShapeDtypeStruct((CTX_TOK, KV_DIM), F32)],
        compiler_params=_cparams(("arbitrary",)),
        name=f"qkv_l{layer}",
    )(x, mods, mods, gains, w, cos, sin)


def _attend(q4, keys, vals, sink_rep, masks, merge_sums):
    logits = []
    for k, mk in zip(keys, masks):
        s = lax.dot_general(q4, k, (((1,), (1,)), ((), ())), preferred_element_type=F32) * ATTN_SCALE
        if mk is not None:
            s = jnp.where(mk, s, NEG_INF)
        logits.append([s[:, cb * LANES:(cb + 1) * LANES] for cb in range(s.shape[1] // LANES)])
    mx = None
    for blocks in logits:
        for blk in blocks:
            mx = blk if mx is None else jnp.maximum(mx, blk)
    m = jnp.maximum(sink_rep, jnp.max(mx, axis=-1, keepdims=True))
    acc = rs = None
    for blocks, v in zip(logits, vals):
        p = jnp.concatenate([jnp.exp(blk - m) for blk in blocks], axis=1).astype(BF16)
        if merge_sums:
            v = jnp.concatenate([v, jnp.ones_like(v)], axis=1)
        else:
            ps = jnp.dot(p, jnp.ones((p.shape[1], LANES), BF16), preferred_element_type=F32)
            rs = ps if rs is None else rs + ps
        pv = jnp.dot(p, v, preferred_element_type=F32)
        acc = pv if acc is None else acc + pv
    if merge_sums:
        denom = pltpu.roll(acc, HEAD_DIM, axis=1) + jnp.exp(sink_rep - m)
        return (acc / denom)[:, :HEAD_DIM]
    return acc / (rs + jnp.exp(sink_rep - m))[:, :HEAD_DIM]


def _attn_ctx_kernel(sink_ref, q_ref, k_ref, v_ref, o_ref):
    rows = CTX_LEN
    outs = [None] * N_Q_HEADS
    for hk in range(N_KV_HEADS):
        k = k_ref[:, hk * HEAD_DIM:(hk + 1) * HEAD_DIM].astype(BF16)
        v = v_ref[:, hk * HEAD_DIM:(hk + 1) * HEAD_DIM].astype(BF16)
        heads = [hk * Q_PER_KV + g for g in range(Q_PER_KV)]
        q4 = jnp.concatenate([q_ref[:, h * HEAD_DIM:(h + 1) * HEAD_DIM] for h in heads], axis=0).astype(BF16)
        sink_rep = jnp.concatenate([jnp.full((rows, LANES), sink_ref[h], F32) for h in heads], axis=0)
        o4 = _attend(q4, [k], [v], sink_rep, [None], merge_sums=False)
        for g, h in enumerate(heads):
            outs[h] = o4[g * rows:(g + 1) * rows]
    o_ref[...] = jnp.concatenate(outs, axis=1).astype(o_ref.dtype)


def _attn_lat_kernel(sink_ref, q_ref, kp_ref, kc_ref, kn_ref, vp_ref, vc_ref, vn_ref, ck_ref, cv_ref,
                     o_ref, *, n_blk):
    qi = pl.program_id(1)
    rows = CHUNK
    r4 = Q_PER_KV * rows
    a = lax.broadcasted_iota(jnp.int32, (r4, CHUNK), 0) % rows
    s = lax.broadcasted_iota(jnp.int32, (r4, CHUNK), 1)
    mask_prev = (s >= a) & (qi > 0)
    mask_next = (s <= a) & (qi < n_blk - 1)
    outs = [None] * N_Q_HEADS
    for hk in range(N_KV_HEADS):
        sl = slice(hk * HEAD_DIM, (hk + 1) * HEAD_DIM)
        keys = [kp_ref[:, sl].astype(BF16), kc_ref[:, sl].astype(BF16), kn_ref[:, sl].astype(BF16),
                ck_ref[:, sl].astype(BF16)]
        vals = [vp_ref[:, sl].astype(BF16), vc_ref[:, sl].astype(BF16), vn_ref[:, sl].astype(BF16),
                cv_ref[:, sl].astype(BF16)]
        heads = [hk * Q_PER_KV + g for g in range(Q_PER_KV)]
        q4 = jnp.concatenate([q_ref[:, h * HEAD_DIM:(h + 1) * HEAD_DIM] for h in heads], axis=0).astype(BF16)
        sink_rep = jnp.concatenate([jnp.full((rows, LANES), sink_ref[h], F32) for h in heads], axis=0)
        o4 = _attend(q4, keys, vals, sink_rep, [mask_prev, None, mask_next, None], merge_sums=True)
        for g, h in enumerate(heads):
            outs[h] = o4[g * rows:(g + 1) * rows]
    o_ref[...] = jnp.concatenate(outs, axis=1).astype(o_ref.dtype)


def _attention(qkv, sinks, ctx_k, ctx_v):
    smem = pl.BlockSpec(memory_space=pltpu.SMEM)
    kcol, vcol = Q_DIM // KV_DIM, Q_DIM // KV_DIM + 1
    ctx_blk = CTX_LEN
    o_ctx = pl.pallas_call(
        _attn_ctx_kernel,
        grid=(N_CTX_SEQ,),
        in_specs=[
            smem,
            pl.BlockSpec((ctx_blk, Q_DIM), lambda b: (b, 0)),
            pl.BlockSpec((ctx_blk, KV_DIM), lambda b: (b, kcol)),
            pl.BlockSpec((ctx_blk, KV_DIM), lambda b: (b, vcol)),
        ],
        out_specs=pl.BlockSpec((ctx_blk, Q_DIM), lambda b: (b, 0)),
        out_shape=jax.ShapeDtypeStruct((CTX_TOK, Q_DIM), BF16),
        compiler_params=_cparams(("arbitrary",)),
        name="attn_ctx",
    )(sinks, qkv, qkv, qkv)

    n_blk = LAT_LEN // CHUNK
    base = CTX_TOK // CHUNK

    def rb(b, qi, off):
        return base + b * n_blk + jnp.clip(qi + off, 0, n_blk - 1)

    def kv_spec(col, off):
        return pl.BlockSpec((CHUNK, KV_DIM), lambda b, qi: (rb(b, qi, off), col))

    o_lat = pl.pallas_call(
        functools.partial(_attn_lat_kernel, n_blk=n_blk),
        grid=(N_LAT_SEQ, n_blk),
        in_specs=[
            smem,
            pl.BlockSpec((CHUNK, Q_DIM), lambda b, qi: (rb(b, qi, 0), 0)),
            kv_spec(kcol, -1), kv_spec(kcol, 0), kv_spec(kcol, 1),
            kv_spec(vcol, -1), kv_spec(vcol, 0), kv_spec(vcol, 1),
            pl.BlockSpec((None, CTX_LEN, KV_DIM), lambda b, qi: (b, 0, 0)),
            pl.BlockSpec((None, CTX_LEN, KV_DIM), lambda b, qi: (b, 0, 0)),
        ],
        out_specs=pl.BlockSpec((CHUNK, Q_DIM), lambda b, qi: (b * n_blk + qi, 0)),
        out_shape=jax.ShapeDtypeStruct((LAT_TOK, Q_DIM), BF16),
        compiler_params=_cparams(("arbitrary", "arbitrary")),
        name="attn_lat",
    )(sinks, qkv, qkv, qkv, qkv, qkv, qkv, qkv, ctx_k, ctx_v)
    return o_ctx, o_lat


def _softplus(x):
    return jnp.maximum(x, 0.0) + jnp.log1p(jnp.exp(-jnp.abs(x)))


SSM_TN = 1024
SSM_Z_STEPS = SSM_INNER // SSM_TN
SSM_C_STEPS = SSM_CONV_DIM // SSM_TN


def _ssm_in_kernel(x_ref, sh_ref, sc_ref, g_ref, w_ref, cw_ref, cb_ref, wdt_ref, bdt_ref,
                   z_ref, xbc_ref, dt_ref, h_ref, *, tm):
    i = pl.program_id(0)
    j = pl.program_id(1)

    @pl.when(j == 0)
    def _():
        h_ref[...] = _adaln(x_ref[...], g_ref[...], sh_ref[...], sc_ref[...]).astype(BF16)

    @pl.when(j < SSM_Z_STEPS)
    def _():
        z_ref[...] = jnp.dot(h_ref[...], w_ref[...].astype(BF16), preferred_element_type=F32).astype(BF16)

    @pl.when((j >= SSM_Z_STEPS) & (j < SSM_Z_STEPS + SSM_C_STEPS))
    def _():
        y = jnp.dot(h_ref[...], w_ref[...].astype(BF16), preferred_element_type=F32)
        seq_len = jnp.where(i * tm < CTX_TOK, CTX_LEN, LAT_LEN)
        pos = lax.broadcasted_iota(jnp.int32, (tm, 1), 0) & (seq_len - 1)
        up = jnp.where(pos == 0, 0.0, pltpu.roll(y, 1, axis=0))
        dn = jnp.where(pos == seq_len - 1, 0.0, pltpu.roll(y, tm - 1, axis=0))
        c = up * cw_ref[0:1, :] + y * cw_ref[1:2, :] + dn * cw_ref[2:3, :] + cb_ref[...]
        xbc_ref[...] = _silu(c).astype(BF16)

    @pl.when(j == SSM_Z_STEPS + SSM_C_STEPS)
    def _():
        y = jnp.dot(h_ref[...], wdt_ref[...].astype(BF16), preferred_element_type=F32)
        dt_ref[...] = _softplus(y + bdt_ref[...])


def _ssm_in(x, mods, gains, w, widx, conv_w, conv_b, w_dt, b_dt, layer, tm=1024):
    tn = SSM_TN
    n_main = SSM_Z_STEPS + SSM_C_STEPS

    def conv_col(j):
        return jnp.clip(j - SSM_Z_STEPS, 0, SSM_C_STEPS - 1)

    return pl.pallas_call(
        functools.partial(_ssm_in_kernel, tm=tm),
        grid=(TOK // tm, n_main + 1),
        in_specs=[
            pl.BlockSpec((tm, D), lambda i, j: (i, 0)),
            _mod_spec(layer, 3, tm, 2),
            _mod_spec(layer, 4, tm, 2),
            _gain_spec(layer, 2, 2),
            pl.BlockSpec((None, D, tn), lambda i, j: (widx, 0, jnp.minimum(j, n_main - 1))),
            pl.BlockSpec((3, tn), lambda i, j: (0, conv_col(j))),
            pl.BlockSpec((1, tn), lambda i, j: (0, conv_col(j))),
            pl.BlockSpec((D, LANES), lambda i, j: (0, 0)),
            pl.BlockSpec((1, LANES), lambda i, j: (0, 0)),
        ],
        out_specs=[
            pl.BlockSpec((tm, tn), lambda i, j: (i, jnp.minimum(j, SSM_Z_STEPS - 1))),
            pl.BlockSpec((tm, tn), lambda i, j: (i, conv_col(j))),
            pl.BlockSpec((tm, LANES), lambda i, j: (i, 0)),
        ],
        out_shape=[jax.ShapeDtypeStruct((TOK, SSM_INNER), BF16), jax.ShapeDtypeStruct((TOK, SSM_CONV_DIM), BF16),
                   jax.ShapeDtypeStruct((TOK, LANES), F32)],
        scratch_shapes=[pltpu.VMEM((tm, D), BF16)],
        compiler_params=_cparams(("arbitrary", "arbitrary")),
        name=f"ssm_in_l{layer}",
    )(x, mods, mods, gains, w, conv_w, conv_b, w_dt, b_dt)


def _split3(q):
    hi = q.astype(BF16)
    r1 = q - hi.astype(F32)
    mid = r1.astype(BF16)
    lo = (r1 - mid.astype(F32)).astype(BF16)
    return hi, mid, lo


def _spread(q, r_ref):
    hi, mid, lo = _split3(q[:, :SSM_HEADS])
    return jnp.dot(jnp.concatenate([hi, mid, lo], axis=1), r_ref[...], preferred_element_type=F32)


def _spread_consts():
    r = jnp.arange(3 * SSM_HEADS) % SSM_HEADS
    r1 = (jnp.arange(SSM_HEADS * SSM_P)[None, :] // SSM_P == r[:, None]).astype(BF16)
    r2 = (jnp.arange(SSM_HEADS * CHUNK)[None, :] // CHUNK == r[:, None]).astype(BF16)
    return r1, r2


def _ssd_chain(xbc, dt_all, a_ref, r1_ref, r2_ref, hs_ref, bwd):
    L = CHUNK
    d = 1 if bwd else 0
    xs = xbc[:, :SSM_INNER].astype(F32)
    bcb = xbc[:, SSM_INNER:]

    dt = pltpu.roll(dt_all, LANES - SSM_HEADS, axis=1) if bwd else dt_all
    a_row = a_ref[d:d + 1, :]
    ii = lax.broadcasted_iota(jnp.int32, (L, L), 0)
    jj = lax.broadcasted_iota(jnp.int32, (L, L), 1)
    tri = (jj >= ii) if bwd else (ii >= jj)
    tri_b = jnp.where(tri, 1.0, 0.0).astype(BF16)
    neg_mask = jnp.where(tri, 0.0, -jnp.inf)
    hi_p, mid_p, lo_p = _split3(dt * a_row)
    acum = (jnp.dot(tri_b, hi_p, preferred_element_type=F32)
            + jnp.dot(tri_b, mid_p, preferred_element_type=F32)
            + jnp.dot(tri_b, lo_p, preferred_element_type=F32))
    end = 0 if bwd else L - 1
    total = acum[end:end + 1, :]
    acum_t = acum.T
    cdec_t = jnp.exp(acum_t[:, end:end + 1])

    e_acum = _spread(acum, r2_ref)
    e_dt = _spread(dt, r1_ref)
    e_end = _spread(dt * jnp.exp(total - acum), r1_ref)
    e_in = _spread(jnp.exp(acum), r1_ref)
    xdt = (xs * e_dt).astype(BF16)
    xw = (xs * e_end).astype(BF16)
    left = lax.broadcasted_iota(jnp.int32, (L, LANES), 1) < SSM_P

    y_parts = []
    for g in range(SSM_GROUPS):
        bm = bcb[:, g * SSM_STATE:(g + 1) * SSM_STATE]
        cm = bcb[:, SSM_GN + g * SSM_STATE:SSM_GN + (g + 1) * SSM_STATE]
        cb = lax.dot_general(cm, bm, (((1,), (1,)), ((), ())), preferred_element_type=F32)
        gsl = slice(g * SSM_HG * SSM_P, (g + 1) * SSM_HG * SSM_P)
        hprev = hs_ref[d, gsl, :]
        y_off = lax.dot_general(cm, hprev.astype(BF16), (((1,), (1,)), ((), ())),
                                preferred_element_type=F32)
        y_pairs = []
        for pr in range(SSM_HG // 2):
            wm = []
            for h in (g * SSM_HG + 2 * pr, g * SSM_HG + 2 * pr + 1):
                seg = e_acum[:, h * L:(h + 1) * L] - acum_t[h:h + 1, :] + neg_mask
                wm.append((cb * jnp.exp(seg)).astype(BF16))
            pair = xdt[:, (g * SSM_HG + 2 * pr) * SSM_P:(g * SSM_HG + 2 * pr + 2) * SSM_P]
            rhs = jnp.concatenate([jnp.where(left, pair, jnp.zeros_like(pair)),
                                   jnp.where(left, jnp.zeros_like(pair), pair)], axis=0)
            y_pairs.append(jnp.dot(jnp.concatenate(wm, axis=1), rhs, preferred_element_type=F32))
        y_parts.append(jnp.concatenate(y_pairs, axis=1) + y_off * e_in[:, gsl])
        st = lax.dot_general(xw[:, gsl], bm, (((0,), (0,)), ((), ())), preferred_element_type=F32)
        dec = jnp.concatenate([jnp.broadcast_to(cdec_t[g * SSM_HG + k:g * SSM_HG + k + 1, :], (SSM_P, 1))
                               for k in range(SSM_HG)], axis=0)
        hs_ref[d, gsl, :] = dec * hprev + st
    return jnp.concatenate(y_parts, axis=1), xs


def _ssd_kernel(*refs, nc, has_h0, emit_state):
    (xa_ref, xb_ref, dta_ref, dtb_ref, za_ref, zb_ref, a_ref, dsk_ref, ng_ref, r1_ref, r2_ref) = refs[:11]
    pos = 11
    h0_ref = None
    if has_h0:
        h0_ref = refs[pos]
        pos += 1
    y_ref = refs[pos]
    pos += 1
    st_ref = None
    if emit_state:
        st_ref = refs[pos]
        pos += 1
    hs_ref, yacc_ref = refs[pos:pos + 2]

    s = pl.program_id(1)
    L = CHUNK
    half = nc // 2
    off_a = pl.multiple_of(s * L, L)
    off_b = pl.multiple_of((nc - 1 - s) * L, L)

    @pl.when(s == 0)
    def _():
        hs_ref[...] = h0_ref[...] if has_h0 else jnp.zeros_like(hs_ref)

    ya, xs_a = _ssd_chain(xa_ref[...], dta_ref[...], a_ref, r1_ref, r2_ref, hs_ref, bwd=False)
    ya = ya + dsk_ref[...] * xs_a
    yb, _ = _ssd_chain(xb_ref[...], dtb_ref[...], a_ref, r1_ref, r2_ref, hs_ref, bwd=True)

    @pl.when(s < half)
    def _():
        yacc_ref[pl.ds(off_a, L), :] = ya
        yacc_ref[pl.ds(off_b, L), :] = yb

    @pl.when(s >= half)
    def _():
        for off, y_new, z_ref in ((off_a, ya, za_ref), (off_b, yb, zb_ref)):
            yt = (yacc_ref[pl.ds(off, L), :] + y_new) * _silu(z_ref[...].astype(F32))
            y_ref[pl.ds(off, L), :] = _rms(yt, ng_ref[...]).astype(y_ref.dtype)

    if emit_state:
        @pl.when(s == nc - 1)
        def _():
            st_ref[...] = hs_ref[...]


def _ssd(z, xbc, dt, a_pad, dsk, norm_g, r1, r2, h0, *, seq0, n_seq, seq_len, emit_state):
    nc = seq_len // CHUNK
    half = nc // 2
    chunk0 = seq0 // CHUNK

    def fwd_chunk(b, s):
        return chunk0 + b * nc + s

    def bwd_chunk(b, s):
        return chunk0 + b * nc + nc - 1 - s

    def late(s):
        return jnp.maximum(s, half)

    const = lambda b, s: (0, 0)
    in_specs = [
        pl.BlockSpec((CHUNK, SSM_CONV_DIM), lambda b, s: (fwd_chunk(b, s), 0)),
        pl.BlockSpec((CHUNK, SSM_CONV_DIM), lambda b, s: (bwd_chunk(b, s), 0)),
        pl.BlockSpec((CHUNK, LANES), lambda b, s: (fwd_chunk(b, s), 0)),
        pl.BlockSpec((CHUNK, LANES), lambda b, s: (bwd_chunk(b, s), 0)),
        pl.BlockSpec((CHUNK, SSM_INNER), lambda b, s: (fwd_chunk(b, late(s)), 0)),
        pl.BlockSpec((CHUNK, SSM_INNER), lambda b, s: (bwd_chunk(b, late(s)), 0)),
        pl.BlockSpec((2, LANES), const),
        pl.BlockSpec((1, SSM_INNER), const),
        pl.BlockSpec((1, SSM_INNER), const),
        pl.BlockSpec(r1.shape, const),
        pl.BlockSpec(r2.shape, const),
    ]
    args = [xbc, xbc, dt, dt, z, z, a_pad, dsk, norm_g, r1, r2]
    if h0 is not None:
        in_specs.append(pl.BlockSpec((None, 2, SSM_INNER, SSM_STATE), lambda b, s: (b, 0, 0, 0)))
        args.append(h0)
    out_specs = [pl.BlockSpec((seq_len, SSM_INNER), lambda b, s: (b, 0))]
    out_shape = [jax.ShapeDtypeStruct((n_seq * seq_len, SSM_INNER), BF16)]
    if emit_state:
        out_specs.append(pl.BlockSpec((None, 2, SSM_INNER, SSM_STATE), lambda b, s: (b, 0, 0, 0)))
        out_shape.append(jax.ShapeDtypeStruct((n_seq, 2, SSM_INNER, SSM_STATE), F32))
    return pl.pallas_call(
        functools.partial(_ssd_kernel, nc=nc, has_h0=h0 is not None, emit_state=emit_state),
        grid=(n_seq, nc),
        in_specs=in_specs,
        out_specs=out_specs,
        out_shape=out_shape,
        scratch_shapes=[pltpu.VMEM((2, SSM_INNER, SSM_STATE), F32), pltpu.VMEM((seq_len, SSM_INNER), F32)],
        compiler_params=_cparams(("arbitrary", "arbitrary")),
        name=f"ssd_{seq_len}",
    )(*args)


def kernel(x_prompt, x_sample, cache_k, cache_v, state_ssm, c, c_ctx, w_mod, b_mod, norm_g, ffn_in, ffn_out,
           gmlp_in, gmlp_ln_g, gmlp_ln_b, gmlp_ws, gmlp_bs, gmlp_out, attn_qkv, attn_sink, attn_out,
           ssm_in, ssm_conv_w, ssm_conv_b, ssm_dt_bias, ssm_a_log, ssm_d, ssm_norm, ssm_out):
    cond = jnp.concatenate([c_ctx[None, :], c, jnp.zeros((MOD_ROWS - 1 - N_LAT_SEQ, D), F32)], axis=0)
    mods = _modulation(cond, w_mod, b_mod).reshape(DEPTH * MOD_ROWS * N_MOD, 1, D)
    gains = norm_g.reshape(DEPTH * 6, 1, D)

    x = (x_prompt.reshape(CTX_TOK, D), x_sample.reshape(LAT_TOK, D))
    new_k = new_v = new_s = None
    for layer in range(DEPTH):
        x = _ffn(x, mods, gains, ffn_in, ffn_out, layer, 0, split_in=layer == 0)
        kind, j = layer % 3, layer // 3
        if kind == 0:
            x = _gmlp(x, mods, gains, gmlp_in, gmlp_ln_g[:, None, :], gmlp_ln_b[:, None, :], gmlp_ws,
                      gmlp_bs[..., None], gmlp_out, j, layer)
        elif kind == 1:
            qkv, kc, vc = _qkv_proj(x, mods, gains, attn_qkv, j, layer)
            new_k = kc.reshape(N_CTX_SEQ, CTX_LEN, N_KV_HEADS, HEAD_DIM)
            new_v = vc.reshape(N_CTX_SEQ, CTX_LEN, N_KV_HEADS, HEAD_DIM)
            o_ctx, o_lat = _attention(qkv, attn_sink[j],
                                      cache_k[:, j].reshape(N_LAT_SEQ, CTX_LEN, KV_DIM),
                                      cache_v[:, j].reshape(N_LAT_SEQ, CTX_LEN, KV_DIM))
            x = _out_proj(o_ctx, o_lat, attn_out, j, x, mods, gains, layer)
        else:
            pad = LANES - 2 * SSM_HEADS
            w_dt = jnp.pad(ssm_in[j][:, SSM_MAIN:], ((0, 0), (0, pad)))
            b_dt = jnp.pad(ssm_dt_bias[j].reshape(1, 2 * SSM_HEADS), ((0, 0), (0, pad)))
            z, xbc, dt = _ssm_in(x, mods, gains, ssm_in, j, ssm_conv_w[j], ssm_conv_b[j][None, :],
                                 w_dt, b_dt, layer)
            a_pad = jnp.pad(-jnp.exp(ssm_a_log[j]), ((0, 0), (0, LANES - SSM_HEADS)))
            dsk = jnp.repeat(ssm_d[j], SSM_P)[None, :]
            ng = ssm_norm[j][None, :]
            r1, r2 = _spread_consts()
            y_ctx, st = _ssd(z, xbc, dt, a_pad, dsk, ng, r1, r2, None,
                             seq0=0, n_seq=N_CTX_SEQ, seq_len=CTX_LEN, emit_state=True)
            (y_lat,) = _ssd(z, xbc, dt, a_pad, dsk, ng, r1, r2,
                            state_ssm[:, j].reshape(N_LAT_SEQ, 2, SSM_INNER, SSM_STATE),
                            seq0=CTX_TOK, n_seq=N_LAT_SEQ, seq_len=LAT_LEN, emit_state=False)
            new_s = st.reshape(N_CTX_SEQ, 2, SSM_HEADS, SSM_P, SSM_STATE)
            x = _out_proj(y_ctx, y_lat, ssm_out, j, x, mods, gains, layer)
        x = _ffn(x, mods, gains, ffn_in, ffn_out, layer, 1, split_out=layer == DEPTH - 1)

    y_prompt = x[0].reshape(N_CTX_SEQ, CTX_LEN, D)
    y_sample = x[1].reshape(N_LAT_SEQ, LAT_LEN, D)
    return (y_prompt, y_sample, new_k[:, None], new_v[:, None], new_s[:, None])
```

```python
import functools
import math

import jax
import jax.numpy as jnp
from jax import lax
from jax.experimental import pallas as pl
from jax.experimental.pallas import tpu as pltpu

F32 = jnp.float32
BF16 = jnp.bfloat16

D = 1024
N_CTX_SEQ, CTX_LEN = 16, 256
N_LAT_SEQ, LAT_LEN = 4, 1024
CTX_TOK = N_CTX_SEQ * CTX_LEN
LAT_TOK = N_LAT_SEQ * LAT_LEN
TOK = CTX_TOK + LAT_TOK
DEPTH = 4
N_MOD = 9
MOD_ROWS = 8
D_FF = 2816
EPS = 1e-6
GRID_W = 64
GMLP_HALF = 3 * D
GMLP_GROUPS = 8
GMLP_GD = GMLP_HALF // GMLP_GROUPS
CHUNK = 128
HEAD_DIM = 64
N_Q_HEADS = 16
N_KV_HEADS = 4
Q_PER_KV = 4
Q_DIM = N_Q_HEADS * HEAD_DIM
KV_DIM = N_KV_HEADS * HEAD_DIM
QKV_DIM = Q_DIM + 2 * KV_DIM
ATTN_SCALE = HEAD_DIM ** -0.5
ROPE_BASE = 10000.0
ROT_PAIRS = HEAD_DIM // 4
NEG_INF = -1e30
SSM_INNER = 2 * D
SSM_HEADS = 32
SSM_P = 64
SSM_GROUPS = 4
SSM_HG = SSM_HEADS // SSM_GROUPS
SSM_STATE = 128
SSM_GN = SSM_GROUPS * SSM_STATE
SSM_CONV_DIM = SSM_INNER + 2 * SSM_GN
SSM_MAIN = SSM_INNER + SSM_CONV_DIM
LANES = 128
HALO = 8

VMEM_LIMIT = 56 * 1024 * 1024
VMEM_LIMIT_BIG = 60 * 1024 * 1024


def _cparams(sem, limit=VMEM_LIMIT):
    return pltpu.CompilerParams(dimension_semantics=sem, vmem_limit_bytes=limit)


def _sigmoid(x):
    return 1.0 / (1.0 + jnp.exp(-x))


def _silu(x):
    return x * _sigmoid(x)


def _rms(x, g):
    return x * lax.rsqrt(jnp.mean(x * x, axis=-1, keepdims=True) + EPS) * g


def _adaln(x, g, shift, scale):
    return _rms(x, g) * (1.0 + scale) + shift


def _mod_row(i, tm):
    t0 = i * tm
    return jnp.where(t0 < CTX_TOK, 0, 1 + (t0 - CTX_TOK) // LAT_LEN)


def _mod_spec(layer, k, tm, grid_rank):
    base = layer * MOD_ROWS * N_MOD + k
    if grid_rank == 1:
        return pl.BlockSpec((None, 1, D), lambda i: (base + _mod_row(i, tm) * N_MOD, 0, 0))
    return pl.BlockSpec((None, 1, D), lambda i, j: (base + _mod_row(i, tm) * N_MOD, 0, 0))


def _gain_spec(layer, k, grid_rank):
    idx = layer * 6 + k
    if grid_rank == 1:
        return pl.BlockSpec((None, 1, D), lambda i: (idx, 0, 0))
    return pl.BlockSpec((None, 1, D), lambda i, j: (idx, 0, 0))


def _split_specs(tm, width, grid_rank):
    na = CTX_TOK // tm
    if grid_rank == 1:
        return (pl.BlockSpec((tm, width), lambda i: (jnp.minimum(i, na - 1), 0)),
                pl.BlockSpec((tm, width), lambda i: (jnp.maximum(i - na, 0), 0)))
    return (pl.BlockSpec((tm, width), lambda i, j: (jnp.minimum(i, na - 1), 0)),
            pl.BlockSpec((tm, width), lambda i, j: (jnp.maximum(i - na, 0), 0)))


def _mod_kernel(cond_ref, w_ref, b_ref, o_ref):
    s = _silu(cond_ref[...]).astype(BF16)
    o_ref[...] = jnp.dot(s, w_ref[...].astype(BF16), preferred_element_type=F32) + b_ref[...]


def _modulation(cond, w_mod, b_mod):
    tn = 2304
    n = N_MOD * D
    return pl.pallas_call(
        _mod_kernel,
        grid=(DEPTH, n // tn),
        in_specs=[
            pl.BlockSpec((MOD_ROWS, D), lambda l, j: (0, 0)),
            pl.BlockSpec((None, D, tn), lambda l, j: (l, 0, j)),
            pl.BlockSpec((None, 1, tn), lambda l, j: (l, 0, j)),
        ],
        out_specs=pl.BlockSpec((None, MOD_ROWS, tn), lambda l, j: (l, 0, j)),
        out_shape=jax.ShapeDtypeStruct((DEPTH, MOD_ROWS, n), F32),
        compiler_params=_cparams(("arbitrary", "arbitrary")),
        name="modulation",
    )(cond, w_mod, b_mod.reshape(DEPTH, 1, n))


FFN_TF = 256
FFN_SLABS = D_FF // FFN_TF
FFN_SPS = 3
FFN_STEP_SLABS = (3, 2, 3, 3)
FFN_STEPS = len(FFN_STEP_SLABS)
assert sum(FFN_STEP_SLABS) == FFN_SLABS and max(FFN_STEP_SLABS) == FFN_SPS


def _ffn_kernel(*refs, tm, split_in, split_out):
    n_x = 2 if split_in else 1
    n_o = 2 if split_out else 1
    x_refs = refs[:n_x]
    sh_ref, sc_ref, gt_ref, g0_ref, g1_ref = refs[n_x:n_x + 5]
    w0 = n_x + 5
    wg = refs[w0:w0 + FFN_SPS]
    wu = refs[w0 + FFN_SPS:w0 + 2 * FFN_SPS]
    wo = refs[w0 + 2 * FFN_SPS:w0 + 3 * FFN_SPS]
    o_refs = refs[w0 + 3 * FFN_SPS:w0 + 3 * FFN_SPS + n_o]
    h_ref, acc_ref = refs[w0 + 3 * FFN_SPS + n_o:]
    i = pl.program_id(0)
    j = pl.program_id(1)
    is_ctx = i * tm < CTX_TOK
    last = FFN_STEPS - 1

    def load_x():
        if split_in:
            return jnp.where(is_ctx, x_refs[0][...], x_refs[1][...])
        return x_refs[0][...]

    def swiglu_slabs(h, n):
        acts = []
        for k in range(n):
            g = jnp.dot(h, wg[k][...].astype(BF16), preferred_element_type=F32)
            u = jnp.dot(h, wu[k][...].astype(BF16), preferred_element_type=F32)
            acts.append((_silu(g) * u).astype(BF16))
        w = jnp.concatenate([wo[k][...].astype(BF16) for k in range(n)], axis=0)
        return jnp.dot(jnp.concatenate(acts, axis=1), w, preferred_element_type=F32)

    @pl.when(j == 0)
    def _():
        h = _adaln(load_x(), g0_ref[...], sh_ref[...], sc_ref[...]).astype(BF16)
        h_ref[...] = h
        acc_ref[...] = swiglu_slabs(h, FFN_STEP_SLABS[0])

    for step in range(1, last):
        @pl.when(j == step)
        def _(step=step):
            acc_ref[...] += swiglu_slabs(h_ref[...], FFN_STEP_SLABS[step])

    @pl.when(j == last)
    def _():
        f = acc_ref[...] + swiglu_slabs(h_ref[...], FFN_STEP_SLABS[last])
        res = load_x() + (0.5 * gt_ref[...]) * _rms(f, g1_ref[...])
        if split_out:
            @pl.when(is_ctx)
            def _():
                o_refs[0][...] = res

            @pl.when(jnp.logical_not(is_ctx))
            def _():
                o_refs[1][...] = res
        else:
            o_refs[0][...] = res


def _ffn(x, mods, gains, ffn_in, ffn_out, layer, which, split_in=False, split_out=False):
    tm, tf = 1024, FFN_TF
    k0 = 0 if which == 0 else 6
    gi = 0 if which == 0 else 4
    if split_in:
        x_specs = list(_split_specs(tm, D, 2))
    else:
        x_specs = [pl.BlockSpec((tm, D), lambda i, j: (i, 0))]
    x_args = list(x) if split_in else [x]
    if split_out:
        out_specs = list(_split_specs(tm, D, 2))
        out_shape = [jax.ShapeDtypeStruct((CTX_TOK, D), F32), jax.ShapeDtypeStruct((LAT_TOK, D), F32)]
    else:
        out_specs = pl.BlockSpec((tm, D), lambda i, j: (i, 0))
        out_shape = jax.ShapeDtypeStruct((TOK, D), F32)

    starts = [sum(FFN_STEP_SLABS[:s]) for s in range(FFN_STEPS)]
    table = []
    for k in range(FFN_SPS):
        col = [starts[s] + k if k < FFN_STEP_SLABS[s] else None for s in range(FFN_STEPS)]
        for s in reversed(range(FFN_STEPS)):
            if col[s] is None:
                col[s] = col[s + 1] if s + 1 < FFN_STEPS else col[s - 1]
        table.append(col)

    def slab(j, k):
        idx = table[k][FFN_STEPS - 1]
        for s in reversed(range(FFN_STEPS - 1)):
            idx = jnp.where(j == s, table[k][s], idx)
        return idx

    def spec_in(k, col0):
        return pl.BlockSpec((None, None, D, tf), lambda i, j: (layer, which, 0, col0 + slab(j, k)))

    def spec_out(k):
        return pl.BlockSpec((None, None, tf, D), lambda i, j: (layer, which, slab(j, k), 0))

    w_specs = ([spec_in(k, 0) for k in range(FFN_SPS)] + [spec_in(k, FFN_SLABS) for k in range(FFN_SPS)]
               + [spec_out(k) for k in range(FFN_SPS)])
    return pl.pallas_call(
        functools.partial(_ffn_kernel, tm=tm, split_in=split_in, split_out=split_out),
        grid=(TOK // tm, FFN_STEPS),
        in_specs=x_specs + [
            _mod_spec(layer, k0 + 0, tm, 2),
            _mod_spec(layer, k0 + 1, tm, 2),
            _mod_spec(layer, k0 + 2, tm, 2),
            _gain_spec(layer, gi, 2),
            _gain_spec(layer, gi + 1, 2),
        ] + w_specs,
        out_specs=out_specs,
        out_shape=out_shape,
        scratch_shapes=[pltpu.VMEM((tm, D), BF16), pltpu.VMEM((tm, D), F32)],
        compiler_params=_cparams(("arbitrary", "arbitrary"), VMEM_LIMIT_BIG),
        name=f"ffn_l{layer}_{which}",
    )(*x_args, mods, mods, mods, gains, gains, *([ffn_in] * (2 * FFN_SPS)), *([ffn_out] * FFN_SPS))


def _out_kernel(aa_ref, ab_ref, w_ref, x_ref, gt_ref, g_ref, o_ref, *, tm):
    a = jnp.where(pl.program_id(0) * tm < CTX_TOK, aa_ref[...], ab_ref[...])
    m = jnp.dot(a, w_ref[...].astype(BF16), preferred_element_type=F32)
    o_ref[...] = x_ref[...] + gt_ref[...] * _rms(m, g_ref[...])


def _out_proj(a_ctx, a_lat, w, widx, x, mods, gains, layer, tm=512):
    k = a_ctx.shape[1]
    sa, sb = _split_specs(tm, k, 1)
    return pl.pallas_call(
        functools.partial(_out_kernel, tm=tm),
        grid=(TOK // tm,),
        in_specs=[
            sa, sb,
            pl.BlockSpec((None, k, D), lambda i: (widx, 0, 0)),
            pl.BlockSpec((tm, D), lambda i: (i, 0)),
            _mod_spec(layer, 5, tm, 1),
            _gain_spec(layer, 3, 1),
        ],
        out_specs=pl.BlockSpec((tm, D), lambda i: (i, 0)),
        out_shape=jax.ShapeDtypeStruct((TOK, D), F32),
        compiler_params=_cparams(("arbitrary",)),
        name=f"outproj_l{layer}",
    )(a_ctx, a_lat, w, x, mods, gains)


GMLP_GPS = 2
GMLP_SLABS = GMLP_GROUPS // GMLP_GPS
GMLP_COLS = GMLP_GPS * GMLP_GD


def _gelu_exact(x):
    return 0.5 * x * (1.0 + lax.erf(x * (1.0 / math.sqrt(2.0))))


def _fold_lanes(v):
    acc = v[:, :LANES]
    for cb in range(1, v.shape[1] // LANES):
        acc = acc + v[:, cb * LANES:(cb + 1) * LANES]
    return acc


def _gmlp_kernel(x_ref, sh_ref, sc_ref, g2_ref, win_ref, lg_ref, lb_ref, ws_ref, bs_ref, wo_ref,
                 gt_ref, g3_ref, o_ref, h_ref, v_ref, c_ref, s1_ref, s2_ref, mu_ref, rstd_ref, *, tm):
    j = pl.program_id(1)
    last = 2 * GMLP_SLABS - 1

    def slab(h):
        return _gelu_exact(jnp.dot(h, win_ref[...].astype(BF16), preferred_element_type=F32))

    def park(y, first):
        if first:
            c_ref[...] = jnp.sum(_fold_lanes(y), axis=-1, keepdims=True) * (1.0 / GMLP_COLS)
        dv = y - c_ref[...]
        if first:
            s1_ref[...] = _fold_lanes(dv)
            s2_ref[...] = _fold_lanes(dv * dv)
        else:
            s1_ref[...] += _fold_lanes(dv)
            s2_ref[...] += _fold_lanes(dv * dv)
        v_ref[j] = y

    def mix(u, first):
        s = j - GMLP_SLABS
        vn = ((v_ref[s] - mu_ref[...]) * rstd_ref[...] * lg_ref[...] + lb_ref[...]).astype(BF16)
        rows = []
        for c in range(tm // CHUNK):
            cols = []
            for gg in range(GMLP_GPS):
                vg = vn[c * CHUNK:(c + 1) * CHUNK, gg * GMLP_GD:(gg + 1) * GMLP_GD]
                cols.append(jnp.dot(ws_ref[gg].astype(BF16), vg, preferred_element_type=F32) + bs_ref[gg])
            rows.append(jnp.concatenate(cols, axis=1))
        a = (u * jnp.concatenate(rows, axis=0)).astype(BF16)
        t = jnp.dot(a, wo_ref[...].astype(BF16), preferred_element_type=F32)
        return t if first else o_ref[...] + t

    @pl.when(j == 0)
    def _():
        h = _adaln(x_ref[...], g2_ref[...], sh_ref[...], sc_ref[...]).astype(BF16)
        h_ref[...] = h
        park(slab(h), True)

    @pl.when((j > 0) & (j < GMLP_SLABS))
    def _():
        park(slab(h_ref[...]), False)

    @pl.when(j == GMLP_SLABS)
    def _():
        d1 = jnp.sum(s1_ref[...], axis=-1, keepdims=True) * (1.0 / GMLP_HALF)
        d2 = jnp.sum(s2_ref[...], axis=-1, keepdims=True) * (1.0 / GMLP_HALF)
        mu_ref[...] = c_ref[...] + d1
        rstd_ref[...] = lax.rsqrt(d2 - d1 * d1 + EPS)
        o_ref[...] = mix(slab(h_ref[...]), True)

    @pl.when((j > GMLP_SLABS) & (j < last))
    def _():
        o_ref[...] = mix(slab(h_ref[...]), False)

    @pl.when(j == last)
    def _():
        m = mix(slab(h_ref[...]), False)
        o_ref[...] = x_ref[...] + gt_ref[...] * _rms(m, g3_ref[...])


def _gmlp(x, mods, gains, w_in, ln_g, ln_b, w_s, b_s, w_out, widx, layer, tm=1024):
    ns = GMLP_SLABS

    def u_slab(j):
        return jnp.maximum(j - ns, 0)

    return pl.pallas_call(
        functools.partial(_gmlp_kernel, tm=tm),
        grid=(TOK // tm, 2 * ns),
        in_specs=[
            pl.BlockSpec((tm, D), lambda i, j: (i, 0)),
            _mod_spec(layer, 3, tm, 2),
            _mod_spec(layer, 4, tm, 2),
            _gain_spec(layer, 2, 2),
            pl.BlockSpec((None, D, GMLP_COLS), lambda i, j: (widx, 0, jnp.where(j < ns, ns + j, j - ns))),
            pl.BlockSpec((None, 1, GMLP_COLS), lambda i, j: (widx, 0, u_slab(j))),
            pl.BlockSpec((None, 1, GMLP_COLS), lambda i, j: (widx, 0, u_slab(j))),
            pl.BlockSpec((None, GMLP_GPS, CHUNK, CHUNK), lambda i, j: (widx, u_slab(j), 0, 0)),
            pl.BlockSpec((None, GMLP_GPS, CHUNK, 1), lambda i, j: (widx, u_slab(j), 0, 0)),
            pl.BlockSpec((None, GMLP_COLS, D), lambda i, j: (widx, u_slab(j), 0)),
            _mod_spec(layer, 5, tm, 2),
            _gain_spec(layer, 3, 2),
        ],
        out_specs=pl.BlockSpec((tm, D), lambda i, j: (i, 0)),
        out_shape=jax.ShapeDtypeStruct((TOK, D), F32),
        scratch_shapes=[pltpu.VMEM((tm, D), BF16), pltpu.VMEM((ns, tm, GMLP_COLS), F32),
                        pltpu.VMEM((tm, 1), F32), pltpu.VMEM((tm, LANES), F32), pltpu.VMEM((tm, LANES), F32),
                        pltpu.VMEM((tm, 1), F32), pltpu.VMEM((tm, 1), F32)],
        compiler_params=_cparams(("arbitrary", "arbitrary"), VMEM_LIMIT_BIG),
        name=f"gmlp_l{layer}",
    )(x, mods, mods, gains, w_in, ln_g, ln_b, w_s, b_s, w_out, mods, gains)


def _rope_tables():
    pos = jnp.arange(LAT_LEN)
    pos_r = (pos // GRID_W).astype(F32)
    pos_c = (pos % GRID_W).astype(F32)
    inv = ROPE_BASE ** (-jnp.arange(ROT_PAIRS, dtype=F32) / ROT_PAIRS)
    ang_r = pos_r[:, None] * inv
    ang_c = pos_c[:, None] * inv
    cos = jnp.concatenate([jnp.cos(ang_r)] * 2 + [jnp.cos(ang_c)] * 2, axis=1)
    sin = jnp.concatenate([-jnp.sin(ang_r), jnp.sin(ang_r), -jnp.sin(ang_c), jnp.sin(ang_c)], axis=1)
    reps = LANES // HEAD_DIM
    return jnp.tile(cos, (1, reps)), jnp.tile(sin, (1, reps))


def _qkv_kernel(x_ref, sh_ref, sc_ref, g_ref, w_ref, cos_ref, sin_ref, o_ref, kc_ref, vc_ref, *, tm):
    i = pl.program_id(0)
    h = _adaln(x_ref[...], g_ref[...], sh_ref[...], sc_ref[...])
    y = jnp.dot(h.astype(BF16), w_ref[...].astype(BF16), preferred_element_type=F32)

    @pl.when(i * tm < CTX_TOK)
    def _():
        o_ref[...] = y
        kc_ref[...] = y[:, Q_DIM:Q_DIM + KV_DIM]
        vc_ref[...] = y[:, Q_DIM + KV_DIM:]

    @pl.when(i * tm >= CTX_TOK)
    def _():
        cos = cos_ref[...]
        sin = sin_ref[...]
        lane = lax.broadcasted_iota(jnp.int32, (tm, LANES), 1)
        first = (lane % (2 * ROT_PAIRS)) < ROT_PAIRS
        for cb in range((Q_DIM + KV_DIM) // LANES):
            t = y[:, cb * LANES:(cb + 1) * LANES]
            partner = jnp.where(first, pltpu.roll(t, LANES - ROT_PAIRS, axis=1),
                                pltpu.roll(t, ROT_PAIRS, axis=1))
            o_ref[:, cb * LANES:(cb + 1) * LANES] = t * cos + partner * sin
        o_ref[:, Q_DIM + KV_DIM:] = y[:, Q_DIM + KV_DIM:]


def _qkv_proj(x, mods, gains, w, widx, layer, tm=512):
    cos, sin = _rope_tables()
    per_seq = LAT_LEN // tm
    tab = pl.BlockSpec((tm, LANES), lambda i: (jnp.maximum(i - CTX_TOK // tm, 0) % per_seq, 0))
    return pl.pallas_call(
        functools.partial(_qkv_kernel, tm=tm),
        grid=(TOK // tm,),
        in_specs=[
            pl.BlockSpec((tm, D), lambda i: (i, 0)),
            _mod_spec(layer, 3, tm, 1),
            _mod_spec(layer, 4, tm, 1),
            _gain_spec(layer, 2, 1),
            pl.BlockSpec((None, D, QKV_DIM), lambda i: (widx, 0, 0)),
            tab, tab,
        ],
        out_specs=[pl.BlockSpec((tm, QKV_DIM), lambda i: (i, 0)),
                   _split_specs(tm, KV_DIM, 1)[0], _split_specs(tm, KV_DIM, 1)[0]],
        out_shape=[jax.ShapeDtypeStruct((TOK, QKV_DIM), F32),
                   jax.ShapeDtypeStruct((CTX_TOK, KV_DIM), F32), jax.ShapeDtypeStruct((CTX_TOK, KV_DIM), F32)],
        compiler_params=_cparams(("arbitrary",)),
        name=f"qkv_l{layer}",
    )(x, mods, mods, gains, w, cos, sin)


def _attend(q4, keys, vals, sink_rep, masks, merge_sums):
    logits = []
    for k, mk in zip(keys, masks):
        s = lax.dot_general(q4, k, (((1,), (1,)), ((), ())), preferred_element_type=F32) * ATTN_SCALE
        if mk is not None:
            s = jnp.where(mk, s, NEG_INF)
        logits.append([s[:, cb * LANES:(cb + 1) * LANES] for cb in range(s.shape[1] // LANES)])
    mx = None
    for blocks in logits:
        for blk in blocks:
            mx = blk if mx is None else jnp.maximum(mx, blk)
    m = jnp.maximum(sink_rep, jnp.max(mx, axis=-1, keepdims=True))
    acc = rs = None
    for blocks, v in zip(logits, vals):
        p = jnp.concatenate([jnp.exp(blk - m) for blk in blocks], axis=1).astype(BF16)
        if merge_sums:
            v = jnp.concatenate([v, jnp.ones_like(v)], axis=1)
        else:
            ps = jnp.dot(p, jnp.ones((p.shape[1], LANES), BF16), preferred_element_type=F32)
            rs = ps if rs is None else rs + ps
        pv = jnp.dot(p, v, preferred_element_type=F32)
        acc = pv if acc is None else acc + pv
    if merge_sums:
        denom = pltpu.roll(acc, HEAD_DIM, axis=1) + jnp.exp(sink_rep - m)
        return (acc / denom)[:, :HEAD_DIM]
    return acc / (rs + jnp.exp(sink_rep - m))[:, :HEAD_DIM]


def _attn_ctx_kernel(sink_ref, q_ref, k_ref, v_ref, o_ref):
    rows = CTX_LEN
    outs = [None] * N_Q_HEADS
    for hk in range(N_KV_HEADS):
        k = k_ref[:, hk * HEAD_DIM:(hk + 1) * HEAD_DIM].astype(BF16)
        v = v_ref[:, hk * HEAD_DIM:(hk + 1) * HEAD_DIM].astype(BF16)
        heads = [hk * Q_PER_KV + g for g in range(Q_PER_KV)]
        q4 = jnp.concatenate([q_ref[:, h * HEAD_DIM:(h + 1) * HEAD_DIM] for h in heads], axis=0).astype(BF16)
        sink_rep = jnp.concatenate([jnp.full((rows, LANES), sink_ref[h], F32) for h in heads], axis=0)
        o4 = _attend(q4, [k], [v], sink_rep, [None], merge_sums=False)
        for g, h in enumerate(heads):
            outs[h] = o4[g * rows:(g + 1) * rows]
    o_ref[...] = jnp.concatenate(outs, axis=1).astype(o_ref.dtype)


def _attn_lat_kernel(sink_ref, q_ref, kp_ref, kc_ref, kn_ref, vp_ref, vc_ref, vn_ref, ck_ref, cv_ref,
                     o_ref, *, n_blk):
    qi = pl.program_id(1)
    rows = CHUNK
    r4 = Q_PER_KV * rows
    a = lax.broadcasted_iota(jnp.int32, (r4, CHUNK), 0) % rows
    s = lax.broadcasted_iota(jnp.int32, (r4, CHUNK), 1)
    mask_prev = (s >= a) & (qi > 0)
    mask_next = (s <= a) & (qi < n_blk - 1)
    outs = [None] * N_Q_HEADS
    for hk in range(N_KV_HEADS):
        sl = slice(hk * HEAD_DIM, (hk + 1) * HEAD_DIM)
        keys = [kp_ref[:, sl].astype(BF16), kc_ref[:, sl].astype(BF16), kn_ref[:, sl].astype(BF16),
                ck_ref[:, sl].astype(BF16)]
        vals = [vp_ref[:, sl].astype(BF16), vc_ref[:, sl].astype(BF16), vn_ref[:, sl].astype(BF16),
                cv_ref[:, sl].astype(BF16)]
        heads = [hk * Q_PER_KV + g for g in range(Q_PER_KV)]
        q4 = jnp.concatenate([q_ref[:, h * HEAD_DIM:(h + 1) * HEAD_DIM] for h in heads], axis=0).astype(BF16)
        sink_rep = jnp.concatenate([jnp.full((rows, LANES), sink_ref[h], F32) for h in heads], axis=0)
        o4 = _attend(q4, keys, vals, sink_rep, [mask_prev, None, mask_next, None], merge_sums=True)
        for g, h in enumerate(heads):
            outs[h] = o4[g * rows:(g + 1) * rows]
    o_ref[...] = jnp.concatenate(outs, axis=1).astype(o_ref.dtype)


def _attention(qkv, sinks, ctx_k, ctx_v):
    smem = pl.BlockSpec(memory_space=pltpu.SMEM)
    kcol, vcol = Q_DIM // KV_DIM, Q_DIM // KV_DIM + 1
    ctx_blk = CTX_LEN
    o_ctx = pl.pallas_call(
        _attn_ctx_kernel,
        grid=(N_CTX_SEQ,),
        in_specs=[
            smem,
            pl.BlockSpec((ctx_blk, Q_DIM), lambda b: (b, 0)),
            pl.BlockSpec((ctx_blk, KV_DIM), lambda b: (b, kcol)),
            pl.BlockSpec((ctx_blk, KV_DIM), lambda b: (b, vcol)),
        ],
        out_specs=pl.BlockSpec((ctx_blk, Q_DIM), lambda b: (b, 0)),
        out_shape=jax.ShapeDtypeStruct((CTX_TOK, Q_DIM), BF16),
        compiler_params=_cparams(("arbitrary",)),
        name="attn_ctx",
    )(sinks, qkv, qkv, qkv)

    n_blk = LAT_LEN // CHUNK
    base = CTX_TOK // CHUNK

    def rb(b, qi, off):
        return base + b * n_blk + jnp.clip(qi + off, 0, n_blk - 1)

    def kv_spec(col, off):
        return pl.BlockSpec((CHUNK, KV_DIM), lambda b, qi: (rb(b, qi, off), col))

    o_lat = pl.pallas_call(
        functools.partial(_attn_lat_kernel, n_blk=n_blk),
        grid=(N_LAT_SEQ, n_blk),
        in_specs=[
            smem,
            pl.BlockSpec((CHUNK, Q_DIM), lambda b, qi: (rb(b, qi, 0), 0)),
            kv_spec(kcol, -1), kv_spec(kcol, 0), kv_spec(kcol, 1),
            kv_spec(vcol, -1), kv_spec(vcol, 0), kv_spec(vcol, 1),
            pl.BlockSpec((None, CTX_LEN, KV_DIM), lambda b, qi: (b, 0, 0)),
            pl.BlockSpec((None, CTX_LEN, KV_DIM), lambda b, qi: (b, 0, 0)),
        ],
        out_specs=pl.BlockSpec((CHUNK, Q_DIM), lambda b, qi: (b * n_blk + qi, 0)),
        out_shape=jax.ShapeDtypeStruct((LAT_TOK, Q_DIM), BF16),
        compiler_params=_cparams(("arbitrary", "arbitrary")),
        name="attn_lat",
    )(sinks, qkv, qkv, qkv, qkv, qkv, qkv, qkv, ctx_k, ctx_v)
    return o_ctx, o_lat


def _softplus(x):
    return jnp.maximum(x, 0.0) + jnp.log1p(jnp.exp(-jnp.abs(x)))


SSM_TN = 1024
SSM_Z_STEPS = SSM_INNER // SSM_TN
SSM_C_STEPS = SSM_CONV_DIM // SSM_TN


def _ssm_in_kernel(x_ref, sh_ref, sc_ref, g_ref, w_ref, cw_ref, cb_ref, wdt_ref, bdt_ref,
                   z_ref, xbc_ref, dt_ref, h_ref, *, tm):
    i = pl.program_id(0)
    j = pl.program_id(1)

    @pl.when(j == 0)
    def _():
        h_ref[...] = _adaln(x_ref[...], g_ref[...], sh_ref[...], sc_ref[...]).astype(BF16)

    @pl.when(j < SSM_Z_STEPS)
    def _():
        z_ref[...] = jnp.dot(h_ref[...], w_ref[...].astype(BF16), preferred_element_type=F32).astype(BF16)

    @pl.when((j >= SSM_Z_STEPS) & (j < SSM_Z_STEPS + SSM_C_STEPS))
    def _():
        y = jnp.dot(h_ref[...], w_ref[...].astype(BF16), preferred_element_type=F32)
        seq_len = jnp.where(i * tm < CTX_TOK, CTX_LEN, LAT_LEN)
        pos = lax.broadcasted_iota(jnp.int32, (tm, 1), 0) & (seq_len - 1)
        up = jnp.where(pos == 0, 0.0, pltpu.roll(y, 1, axis=0))
        dn = jnp.where(pos == seq_len - 1, 0.0, pltpu.roll(y, tm - 1, axis=0))
        c = up * cw_ref[0:1, :] + y * cw_ref[1:2, :] + dn * cw_ref[2:3, :] + cb_ref[...]
        xbc_ref[...] = _silu(c).astype(BF16)

    @pl.when(j == SSM_Z_STEPS + SSM_C_STEPS)
    def _():
        y = jnp.dot(h_ref[...], wdt_ref[...].astype(BF16), preferred_element_type=F32)
        dt_ref[...] = _softplus(y + bdt_ref[...])


def _ssm_in(x, mods, gains, w, widx, conv_w, conv_b, w_dt, b_dt, layer, tm=1024):
    tn = SSM_TN
    n_main = SSM_Z_STEPS + SSM_C_STEPS

    def conv_col(j):
        return jnp.clip(j - SSM_Z_STEPS, 0, SSM_C_STEPS - 1)

    return pl.pallas_call(
        functools.partial(_ssm_in_kernel, tm=tm),
        grid=(TOK // tm, n_main + 1),
        in_specs=[
            pl.BlockSpec((tm, D), lambda i, j: (i, 0)),
            _mod_spec(layer, 3, tm, 2),
            _mod_spec(layer, 4, tm, 2),
            _gain_spec(layer, 2, 2),
            pl.BlockSpec((None, D, tn), lambda i, j: (widx, 0, jnp.minimum(j, n_main - 1))),
            pl.BlockSpec((3, tn), lambda i, j: (0, conv_col(j))),
            pl.BlockSpec((1, tn), lambda i, j: (0, conv_col(j))),
            pl.BlockSpec((D, LANES), lambda i, j: (0, 0)),
            pl.BlockSpec((1, LANES), lambda i, j: (0, 0)),
        ],
        out_specs=[
            pl.BlockSpec((tm, tn), lambda i, j: (i, jnp.minimum(j, SSM_Z_STEPS - 1))),
            pl.BlockSpec((tm, tn), lambda i, j: (i, conv_col(j))),
            pl.BlockSpec((tm, LANES), lambda i, j: (i, 0)),
        ],
        out_shape=[jax.ShapeDtypeStruct((TOK, SSM_INNER), BF16), jax.ShapeDtypeStruct((TOK, SSM_CONV_DIM), BF16),
                   jax.ShapeDtypeStruct((TOK, LANES), F32)],
        scratch_shapes=[pltpu.VMEM((tm, D), BF16)],
        compiler_params=_cparams(("arbitrary", "arbitrary")),
        name=f"ssm_in_l{layer}",
    )(x, mods, mods, gains, w, conv_w, conv_b, w_dt, b_dt)


def _split3(q):
    hi = q.astype(BF16)
    r1 = q - hi.astype(F32)
    mid = r1.astype(BF16)
    lo = (r1 - mid.astype(F32)).astype(BF16)
    return hi, mid, lo


def _spread(q, r_ref):
    hi, mid, lo = _split3(q[:, :SSM_HEADS])
    return jnp.dot(jnp.concatenate([hi, mid, lo], axis=1), r_ref[...], preferred_element_type=F32)


def _spread_consts():
    r = jnp.arange(3 * SSM_HEADS) % SSM_HEADS
    r1 = (jnp.arange(SSM_HEADS * SSM_P)[None, :] // SSM_P == r[:, None]).astype(BF16)
    r2 = (jnp.arange(SSM_HEADS * CHUNK)[None, :] // CHUNK == r[:, None]).astype(BF16)
    return r1, r2


def _ssd_chain(xbc, dt_all, a_ref, r1_ref, r2_ref, hs_ref, bwd):
    L = CHUNK
    d = 1 if bwd else 0
    xs = xbc[:, :SSM_INNER].astype(F32)
    bcb = xbc[:, SSM_INNER:]

    dt = pltpu.roll(dt_all, LANES - SSM_HEADS, axis=1) if bwd else dt_all
    a_row = a_ref[d:d + 1, :]
    ii = lax.broadcasted_iota(jnp.int32, (L, L), 0)
    jj = lax.broadcasted_iota(jnp.int32, (L, L), 1)
    tri = (jj >= ii) if bwd else (ii >= jj)
    tri_b = jnp.where(tri, 1.0, 0.0).astype(BF16)
    neg_mask = jnp.where(tri, 0.0, -jnp.inf)
    hi_p, mid_p, lo_p = _split3(dt * a_row)
    acum = (jnp.dot(tri_b, hi_p, preferred_element_type=F32)
            + jnp.dot(tri_b, mid_p, preferred_element_type=F32)
            + jnp.dot(tri_b, lo_p, preferred_element_type=F32))
    end = 0 if bwd else L - 1
    total = acum[end:end + 1, :]
    acum_t = acum.T
    cdec_t = jnp.exp(acum_t[:, end:end + 1])

    e_acum = _spread(acum, r2_ref)
    e_dt = _spread(dt, r1_ref)
    e_end = _spread(dt * jnp.exp(total - acum), r1_ref)
    e_in = _spread(jnp.exp(acum), r1_ref)
    xdt = (xs * e_dt).astype(BF16)
    xw = (xs * e_end).astype(BF16)
    left = lax.broadcasted_iota(jnp.int32, (L, LANES), 1) < SSM_P

    y_parts = []
    for g in range(SSM_GROUPS):
        bm = bcb[:, g * SSM_STATE:(g + 1) * SSM_STATE]
        cm = bcb[:, SSM_GN + g * SSM_STATE:SSM_GN + (g + 1) * SSM_STATE]
        cb = lax.dot_general(cm, bm, (((1,), (1,)), ((), ())), preferred_element_type=F32)
        gsl = slice(g * SSM_HG * SSM_P, (g + 1) * SSM_HG * SSM_P)
        hprev = hs_ref[d, gsl, :]
        y_off = lax.dot_general(cm, hprev.astype(BF16), (((1,), (1,)), ((), ())),
                                preferred_element_type=F32)
        y_pairs = []
        for pr in range(SSM_HG // 2):
            wm = []
            for h in (g * SSM_HG + 2 * pr, g * SSM_HG + 2 * pr + 1):
                seg = e_acum[:, h * L:(h + 1) * L] - acum_t[h:h + 1, :] + neg_mask
                wm.append((cb * jnp.exp(seg)).astype(BF16))
            pair = xdt[:, (g * SSM_HG + 2 * pr) * SSM_P:(g * SSM_HG + 2 * pr + 2) * SSM_P]
            rhs = jnp.concatenate([jnp.where(left, pair, jnp.zeros_like(pair)),
                                   jnp.where(left, jnp.zeros_like(pair), pair)], axis=0)
            y_pairs.append(jnp.dot(jnp.concatenate(wm, axis=1), rhs, preferred_element_type=F32))
        y_parts.append(jnp.concatenate(y_pairs, axis=1) + y_off * e_in[:, gsl])
        st = lax.dot_general(xw[:, gsl], bm, (((0,), (0,)), ((), ())), preferred_element_type=F32)
        dec = jnp.concatenate([jnp.broadcast_to(cdec_t[g * SSM_HG + k:g * SSM_HG + k + 1, :], (SSM_P, 1))
                               for k in range(SSM_HG)], axis=0)
        hs_ref[d, gsl, :] = dec * hprev + st
    return jnp.concatenate(y_parts, axis=1), xs


def _ssd_kernel(*refs, nc, has_h0, emit_state):
    (xa_ref, xb_ref, dta_ref, dtb_ref, za_ref, zb_ref, a_ref, dsk_ref, ng_ref, r1_ref, r2_ref,
     wo_ref, xres_ref, gt_ref, g3_ref) = refs[:15]
    pos = 15
    h0_ref = None
    if has_h0:
        h0_ref = refs[pos]
        pos += 1
    y_ref = refs[pos]
    pos += 1
    st_ref = None
    if emit_state:
        st_ref = refs[pos]
        pos += 1
    hs_ref, yacc_ref = refs[pos:pos + 2]

    s = pl.program_id(1)
    L = CHUNK
    half = nc // 2
    off_a = pl.multiple_of(s * L, L)
    off_b = pl.multiple_of((nc - 1 - s) * L, L)

    @pl.when(s == 0)
    def _():
        hs_ref[...] = h0_ref[...] if has_h0 else jnp.zeros_like(hs_ref)

    ya, xs_a = _ssd_chain(xa_ref[...], dta_ref[...], a_ref, r1_ref, r2_ref, hs_ref, bwd=False)
    ya = ya + dsk_ref[...] * xs_a
    yb, _ = _ssd_chain(xb_ref[...], dtb_ref[...], a_ref, r1_ref, r2_ref, hs_ref, bwd=True)

    @pl.when(s < half)
    def _():
        yacc_ref[pl.ds(off_a, L), :] = ya
        yacc_ref[pl.ds(off_b, L), :] = yb

    @pl.when(s >= half)
    def _():
        for off, y_new, z_ref in ((off_a, ya, za_ref), (off_b, yb, zb_ref)):
            yt = (yacc_ref[pl.ds(off, L), :] + y_new) * _silu(z_ref[...].astype(F32))
            yn = _rms(yt, ng_ref[...]).astype(BF16)
            m = jnp.dot(yn, wo_ref[...], preferred_element_type=F32)
            y_ref[pl.ds(off, L), :] = xres_ref[pl.ds(off, L), :] + gt_ref[...] * _rms(m, g3_ref[...])

    if emit_state:
        @pl.when(s == nc - 1)
        def _():
            st_ref[...] = hs_ref[...]


def _ssd(z, xbc, dt, a_pad, dsk, norm_g, r1, r2, w_out, x, mods, gains, layer, h0, *,
         seq0, n_seq, seq_len, emit_state):
    nc = seq_len // CHUNK
    half = nc // 2
    chunk0 = seq0 // CHUNK
    seq_blk0 = seq0 // seq_len
    mod_base = layer * MOD_ROWS * N_MOD + 5
    mod_row0 = 0 if seq0 < CTX_TOK else 1
    mod_step = 0 if seq0 < CTX_TOK else N_MOD

    def fwd_chunk(b, s):
        return chunk0 + b * nc + s

    def bwd_chunk(b, s):
        return chunk0 + b * nc + nc - 1 - s

    def late(s):
        return jnp.maximum(s, half)

    const = lambda b, s: (0, 0)
    in_specs = [
        pl.BlockSpec((CHUNK, SSM_CONV_DIM), lambda b, s: (fwd_chunk(b, s), 0)),
        pl.BlockSpec((CHUNK, SSM_CONV_DIM), lambda b, s: (bwd_chunk(b, s), 0)),
        pl.BlockSpec((CHUNK, LANES), lambda b, s: (fwd_chunk(b, s), 0)),
        pl.BlockSpec((CHUNK, LANES), lambda b, s: (bwd_chunk(b, s), 0)),
        pl.BlockSpec((CHUNK, SSM_INNER), lambda b, s: (fwd_chunk(b, late(s)), 0)),
        pl.BlockSpec((CHUNK, SSM_INNER), lambda b, s: (bwd_chunk(b, late(s)), 0)),
        pl.BlockSpec((2, LANES), const),
        pl.BlockSpec((1, SSM_INNER), const),
        pl.BlockSpec((1, SSM_INNER), const),
        pl.BlockSpec(r1.shape, const),
        pl.BlockSpec(r2.shape, const),
        pl.BlockSpec((SSM_INNER, D), const, pipeline_mode=pl.Buffered(1)),
        pl.BlockSpec((seq_len, D), lambda b, s: (seq_blk0 + b, 0)),
        pl.BlockSpec((None, 1, D), lambda b, s: (mod_base + mod_row0 * N_MOD + b * mod_step, 0, 0)),
        pl.BlockSpec((None, 1, D), lambda b, s: (layer * 6 + 3, 0, 0)),
    ]
    args = [xbc, xbc, dt, dt, z, z, a_pad, dsk, norm_g, r1, r2, w_out, x, mods, gains]
    if h0 is not None:
        in_specs.append(pl.BlockSpec((None, 2, SSM_INNER, SSM_STATE), lambda b, s: (b, 0, 0, 0)))
        args.append(h0)
    out_specs = [pl.BlockSpec((seq_len, D), lambda b, s: (b, 0))]
    out_shape = [jax.ShapeDtypeStruct((n_seq * seq_len, D), F32)]
    if emit_state:
        out_specs.append(pl.BlockSpec((None, 2, SSM_INNER, SSM_STATE), lambda b, s: (b, 0, 0, 0)))
        out_shape.append(jax.ShapeDtypeStruct((n_seq, 2, SSM_INNER, SSM_STATE), F32))
    return pl.pallas_call(
        functools.partial(_ssd_kernel, nc=nc, has_h0=h0 is not None, emit_state=emit_state),
        grid=(n_seq, nc),
        in_specs=in_specs,
        out_specs=out_specs,
        out_shape=out_shape,
        scratch_shapes=[pltpu.VMEM((2, SSM_INNER, SSM_STATE), F32), pltpu.VMEM((seq_len, SSM_INNER), F32)],
        compiler_params=_cparams(("arbitrary", "arbitrary")),
        name=f"ssd_{seq_len}",
    )(*args)


def kernel(x_prompt, x_sample, cache_k, cache_v, state_ssm, c, c_ctx, w_mod, b_mod, norm_g, ffn_in, ffn_out,
           gmlp_in, gmlp_ln_g, gmlp_ln_b, gmlp_ws, gmlp_bs, gmlp_out, attn_qkv, attn_sink, attn_out,
           ssm_in, ssm_conv_w, ssm_conv_b, ssm_dt_bias, ssm_a_log, ssm_d, ssm_norm, ssm_out):
    cond = jnp.concatenate([c_ctx[None, :], c, jnp.zeros((MOD_ROWS - 1 - N_LAT_SEQ, D), F32)], axis=0)
    mods = _modulation(cond, w_mod, b_mod).reshape(DEPTH * MOD_ROWS * N_MOD, 1, D)
    gains = norm_g.reshape(DEPTH * 6, 1, D)

    x = (x_prompt.reshape(CTX_TOK, D), x_sample.reshape(LAT_TOK, D))
    new_k = new_v = new_s = None
    for layer in range(DEPTH):
        x = _ffn(x, mods, gains, ffn_in, ffn_out, layer, 0, split_in=layer == 0)
        kind, j = layer % 3, layer // 3
        if kind == 0:
            x = _gmlp(x, mods, gains, gmlp_in, gmlp_ln_g[:, None, :], gmlp_ln_b[:, None, :], gmlp_ws,
                      gmlp_bs[..., None], gmlp_out, j, layer)
        elif kind == 1:
            qkv, kc, vc = _qkv_proj(x, mods, gains, attn_qkv, j, layer)
            new_k = kc.reshape(N_CTX_SEQ, CTX_LEN, N_KV_HEADS, HEAD_DIM)
            new_v = vc.reshape(N_CTX_SEQ, CTX_LEN, N_KV_HEADS, HEAD_DIM)
            o_ctx, o_lat = _attention(qkv, attn_sink[j],
                                      cache_k[:, j].reshape(N_LAT_SEQ, CTX_LEN, KV_DIM),
                                      cache_v[:, j].reshape(N_LAT_SEQ, CTX_LEN, KV_DIM))
            x = _out_proj(o_ctx, o_lat, attn_out, j, x, mods, gains, layer)
        else:
            pad = LANES - 2 * SSM_HEADS
            w_dt = jnp.pad(ssm_in[j][:, SSM_MAIN:], ((0, 0), (0, pad)))
            b_dt = jnp.pad(ssm_dt_bias[j].reshape(1, 2 * SSM_HEADS), ((0, 0), (0, pad)))
            z, xbc, dt = _ssm_in(x, mods, gains, ssm_in, j, ssm_conv_w[j], ssm_conv_b[j][None, :],
                                 w_dt, b_dt, layer)
            a_pad = jnp.pad(-jnp.exp(ssm_a_log[j]), ((0, 0), (0, LANES - SSM_HEADS)))
            dsk = jnp.repeat(ssm_d[j], SSM_P)[None, :]
            ng = ssm_norm[j][None, :]
            r1, r2 = _spread_consts()
            w_o = ssm_out[j].astype(BF16)
            x_ctx, st = _ssd(z, xbc, dt, a_pad, dsk, ng, r1, r2, w_o, x, mods, gains, layer, None,
                             seq0=0, n_seq=N_CTX_SEQ, seq_len=CTX_LEN, emit_state=True)
            (x_lat,) = _ssd(z, xbc, dt, a_pad, dsk, ng, r1, r2, w_o, x, mods, gains, layer,
                            state_ssm[:, j].reshape(N_LAT_SEQ, 2, SSM_INNER, SSM_STATE),
                            seq0=CTX_TOK, n_seq=N_LAT_SEQ, seq_len=LAT_LEN, emit_state=False)
            new_s = st.reshape(N_CTX_SEQ, 2, SSM_HEADS, SSM_P, SSM_STATE)
            x = (x_ctx, x_lat)
        x = _ffn(x, mods, gains, ffn_in, ffn_out, layer, 1, split_in=isinstance(x, tuple),
                 split_out=layer == DEPTH - 1)

    y_prompt = x[0].reshape(N_CTX_SEQ, CTX_LEN, D)
    y_sample = x[1].reshape(N_LAT_SEQ, LAT_LEN, D)
    return (y_prompt, y_sample, new_k[:, None], new_v[:, None], new_s[:, None])
```

```python
import functools
import math

import jax
import jax.numpy as jnp
from jax import lax
from jax.experimental import pallas as pl
from jax.experimental.pallas import tpu as pltpu

F32 = jnp.float32
BF16 = jnp.bfloat16

D = 1024
N_CTX_SEQ, CTX_LEN = 16, 256
N_LAT_SEQ, LAT_LEN = 4, 1024
CTX_TOK = N_CTX_SEQ * CTX_LEN
LAT_TOK = N_LAT_SEQ * LAT_LEN
TOK = CTX_TOK + LAT_TOK
DEPTH = 4
N_MOD = 9
MOD_ROWS = 8
D_FF = 2816
EPS = 1e-6
GRID_W = 64
GMLP_HALF = 3 * D
GMLP_GROUPS = 8
GMLP_GD = GMLP_HALF // GMLP_GROUPS
CHUNK = 128
HEAD_DIM = 64
N_Q_HEADS = 16
N_KV_HEADS = 4
Q_PER_KV = 4
Q_DIM = N_Q_HEADS * HEAD_DIM
KV_DIM = N_KV_HEADS * HEAD_DIM
QKV_DIM = Q_DIM + 2 * KV_DIM
ATTN_SCALE = HEAD_DIM ** -0.5
ROPE_BASE = 10000.0
ROT_PAIRS = HEAD_DIM // 4
NEG_INF = -1e30
SSM_INNER = 2 * D
SSM_HEADS = 32
SSM_P = 64
SSM_GROUPS = 4
SSM_HG = SSM_HEADS // SSM_GROUPS
SSM_STATE = 128
SSM_GN = SSM_GROUPS * SSM_STATE
SSM_CONV_DIM = SSM_INNER + 2 * SSM_GN
SSM_MAIN = SSM_INNER + SSM_CONV_DIM
LANES = 128
LOG2E = 1.0 / math.log(2.0)

VMEM_LIMIT = 56 * 1024 * 1024
VMEM_LIMIT_BIG = 60 * 1024 * 1024


def _cparams(sem, limit=VMEM_LIMIT):
    return pltpu.CompilerParams(dimension_semantics=sem, vmem_limit_bytes=limit)


def _sigmoid(x):
    return 1.0 / (1.0 + jnp.exp(-x))


def _silu(x):
    return x * _sigmoid(x)


def _rms(x, g):
    return x * lax.rsqrt(jnp.mean(x * x, axis=-1, keepdims=True) + EPS) * g


def _adaln(x, g, shift, scale):
    return _rms(x, g) * (1.0 + scale) + shift


def _mod_row(i, tm):
    t0 = i * tm
    return jnp.where(t0 < CTX_TOK, 0, 1 + (t0 - CTX_TOK) // LAT_LEN)


def _mod_spec(layer, k, tm, grid_rank):
    base = layer * MOD_ROWS * N_MOD + k
    if grid_rank == 1:
        return pl.BlockSpec((None, 1, D), lambda i: (base + _mod_row(i, tm) * N_MOD, 0, 0))
    return pl.BlockSpec((None, 1, D), lambda i, j: (base + _mod_row(i, tm) * N_MOD, 0, 0))


def _gain_spec(layer, k, grid_rank):
    idx = layer * 6 + k
    if grid_rank == 1:
        return pl.BlockSpec((None, 1, D), lambda i: (idx, 0, 0))
    return pl.BlockSpec((None, 1, D), lambda i, j: (idx, 0, 0))


def _split_specs(tm, width, grid_rank):
    na = CTX_TOK // tm
    if grid_rank == 1:
        return (pl.BlockSpec((tm, width), lambda i: (jnp.minimum(i, na - 1), 0)),
                pl.BlockSpec((tm, width), lambda i: (jnp.maximum(i - na, 0), 0)))
    return (pl.BlockSpec((tm, width), lambda i, j: (jnp.minimum(i, na - 1), 0)),
            pl.BlockSpec((tm, width), lambda i, j: (jnp.maximum(i - na, 0), 0)))


def _mod_kernel(cond_ref, w_ref, b_ref, o_ref):
    s = _silu(cond_ref[...]).astype(BF16)
    o_ref[...] = jnp.dot(s, w_ref[...].astype(BF16), preferred_element_type=F32) + b_ref[...]


def _modulation(cond, w_mod, b_mod):
    tn = 2304
    n = N_MOD * D
    return pl.pallas_call(
        _mod_kernel,
        grid=(DEPTH, n // tn),
        in_specs=[
            pl.BlockSpec((MOD_ROWS, D), lambda l, j: (0, 0)),
            pl.BlockSpec((None, D, tn), lambda l, j: (l, 0, j)),
            pl.BlockSpec((None, 1, tn), lambda l, j: (l, 0, j)),
        ],
        out_specs=pl.BlockSpec((None, MOD_ROWS, tn), lambda l, j: (l, 0, j)),
        out_shape=jax.ShapeDtypeStruct((DEPTH, MOD_ROWS, n), F32),
        compiler_params=_cparams(("arbitrary", "arbitrary")),
        name="modulation",
    )(cond, w_mod, b_mod.reshape(DEPTH, 1, n))


FFN_TF = 256
FFN_SLABS = D_FF // FFN_TF
FFN_SPS = 4
FFN_STEP_SLABS = (3, 4, 4)
FFN_STEPS = len(FFN_STEP_SLABS)
assert sum(FFN_STEP_SLABS) == FFN_SLABS and max(FFN_STEP_SLABS) == FFN_SPS


def _ffn_kernel(*refs, tm, split_in, split_out):
    n_x = 2 if split_in else 1
    n_o = 2 if split_out else 1
    x_refs = refs[:n_x]
    sh_ref, sc_ref, gt_ref, g0_ref, g1_ref = refs[n_x:n_x + 5]
    w0 = n_x + 5
    wg = refs[w0:w0 + FFN_SPS]
    wu = refs[w0 + FFN_SPS:w0 + 2 * FFN_SPS]
    wo = refs[w0 + 2 * FFN_SPS:w0 + 3 * FFN_SPS]
    o_refs = refs[w0 + 3 * FFN_SPS:w0 + 3 * FFN_SPS + n_o]
    h_ref, acc_ref = refs[w0 + 3 * FFN_SPS + n_o:]
    i = pl.program_id(0)
    j = pl.program_id(1)
    is_ctx = i * tm < CTX_TOK
    last = FFN_STEPS - 1

    def load_x():
        if split_in:
            return jnp.where(is_ctx, x_refs[0][...], x_refs[1][...])
        return x_refs[0][...]

    def swiglu_slabs(h, n):
        acts = []
        for k in range(n):
            g = jnp.dot(h, wg[k][...].astype(BF16), preferred_element_type=F32)
            u = jnp.dot(h, wu[k][...].astype(BF16), preferred_element_type=F32)
            acts.append((_silu(g) * u).astype(BF16))
        w = jnp.concatenate([wo[k][...].astype(BF16) for k in range(n)], axis=0)
        return jnp.dot(jnp.concatenate(acts, axis=1), w, preferred_element_type=F32)

    @pl.when(j == 0)
    def _():
        h = _adaln(load_x(), g0_ref[...], sh_ref[...], sc_ref[...]).astype(BF16)
        h_ref[...] = h
        acc_ref[...] = swiglu_slabs(h, FFN_STEP_SLABS[0])

    for step in range(1, last):
        @pl.when(j == step)
        def _(step=step):
            acc_ref[...] += swiglu_slabs(h_ref[...], FFN_STEP_SLABS[step])

    @pl.when(j == last)
    def _():
        f = acc_ref[...] + swiglu_slabs(h_ref[...], FFN_STEP_SLABS[last])
        res = load_x() + (0.5 * gt_ref[...]) * _rms(f, g1_ref[...])
        if split_out:
            @pl.when(is_ctx)
            def _():
                o_refs[0][...] = res

            @pl.when(jnp.logical_not(is_ctx))
            def _():
                o_refs[1][...] = res
        else:
            o_refs[0][...] = res


def _ffn(x, mods, gains, ffn_in, ffn_out, layer, which, split_in=False, split_out=False):
    tm, tf = 1024, FFN_TF
    k0 = 0 if which == 0 else 6
    gi = 0 if which == 0 else 4
    if split_in:
        x_specs = list(_split_specs(tm, D, 2))
    else:
        x_specs = [pl.BlockSpec((tm, D), lambda i, j: (i, 0))]
    x_args = list(x) if split_in else [x]
    if split_out:
        out_specs = list(_split_specs(tm, D, 2))
        out_shape = [jax.ShapeDtypeStruct((CTX_TOK, D), F32), jax.ShapeDtypeStruct((LAT_TOK, D), F32)]
    else:
        out_specs = pl.BlockSpec((tm, D), lambda i, j: (i, 0))
        out_shape = jax.ShapeDtypeStruct((TOK, D), F32)

    starts = [sum(FFN_STEP_SLABS[:s]) for s in range(FFN_STEPS)]
    table = []
    for k in range(FFN_SPS):
        col = [starts[s] + k if k < FFN_STEP_SLABS[s] else None for s in range(FFN_STEPS)]
        for s in reversed(range(FFN_STEPS)):
            if col[s] is None:
                col[s] = col[s + 1] if s + 1 < FFN_STEPS else col[s - 1]
        table.append(col)

    def slab(j, k):
        idx = table[k][FFN_STEPS - 1]
        for s in reversed(range(FFN_STEPS - 1)):
            idx = jnp.where(j == s, table[k][s], idx)
        return idx

    def spec_in(k, col0):
        return pl.BlockSpec((None, None, D, tf), lambda i, j: (layer, which, 0, col0 + slab(j, k)))

    def spec_out(k):
        return pl.BlockSpec((None, None, tf, D), lambda i, j: (layer, which, slab(j, k), 0))

    w_specs = ([spec_in(k, 0) for k in range(FFN_SPS)] + [spec_in(k, FFN_SLABS) for k in range(FFN_SPS)]
               + [spec_out(k) for k in range(FFN_SPS)])
    return pl.pallas_call(
        functools.partial(_ffn_kernel, tm=tm, split_in=split_in, split_out=split_out),
        grid=(TOK // tm, FFN_STEPS),
        in_specs=x_specs + [
            _mod_spec(layer, k0 + 0, tm, 2),
            _mod_spec(layer, k0 + 1, tm, 2),
            _mod_spec(layer, k0 + 2, tm, 2),
            _gain_spec(layer, gi, 2),
            _gain_spec(layer, gi + 1, 2),
        ] + w_specs,
        out_specs=out_specs,
        out_shape=out_shape,
        scratch_shapes=[pltpu.VMEM((tm, D), BF16), pltpu.VMEM((tm, D), F32)],
        compiler_params=_cparams(("arbitrary", "arbitrary"), VMEM_LIMIT_BIG),
        name=f"ffn_l{layer}_{which}",
    )(*x_args, mods, mods, mods, gains, gains, *([ffn_in] * (2 * FFN_SPS)), *([ffn_out] * FFN_SPS))


def _out_kernel(aa_ref, ab_ref, w_ref, x_ref, gt_ref, g_ref, o_ref, *, tm):
    a = jnp.where(pl.program_id(0) * tm < CTX_TOK, aa_ref[...], ab_ref[...])
    m = jnp.dot(a, w_ref[...].astype(BF16), preferred_element_type=F32)
    o_ref[...] = x_ref[...] + gt_ref[...] * _rms(m, g_ref[...])


def _out_proj(a_ctx, a_lat, w, widx, x, mods, gains, layer, tm=512):
    k = a_ctx.shape[1]
    sa, sb = _split_specs(tm, k, 1)
    return pl.pallas_call(
        functools.partial(_out_kernel, tm=tm),
        grid=(TOK // tm,),
        in_specs=[
            sa, sb,
            pl.BlockSpec((None, k, D), lambda i: (widx, 0, 0)),
            pl.BlockSpec((tm, D), lambda i: (i, 0)),
            _mod_spec(layer, 5, tm, 1),
            _gain_spec(layer, 3, 1),
        ],
        out_specs=pl.BlockSpec((tm, D), lambda i: (i, 0)),
        out_shape=jax.ShapeDtypeStruct((TOK, D), F32),
        compiler_params=_cparams(("arbitrary",)),
        name=f"outproj_l{layer}",
    )(a_ctx, a_lat, w, x, mods, gains)


GMLP_GPS = 2
GMLP_SLABS = GMLP_GROUPS // GMLP_GPS
GMLP_COLS = GMLP_GPS * GMLP_GD


def _gelu_exact(x):
    return 0.5 * x * (1.0 + lax.erf(x * (1.0 / math.sqrt(2.0))))


def _fold_lanes(v):
    acc = v[:, :LANES]
    for cb in range(1, v.shape[1] // LANES):
        acc = acc + v[:, cb * LANES:(cb + 1) * LANES]
    return acc


GMLP_P1 = GMLP_SLABS // 2


def _gmlp_kernel(x_ref, sh_ref, sc_ref, g2_ref, wina_ref, winb_ref, lg_ref, lb_ref, ws_ref, bs_ref, wo_ref,
                 gt_ref, g3_ref, o_ref, h_ref, v_ref, c_ref, s1_ref, s2_ref, mu_ref, rstd_ref, *, tm):
    j = pl.program_id(1)
    last = GMLP_P1 + GMLP_SLABS - 1

    def slab(h, w_ref=wina_ref):
        return _gelu_exact(jnp.dot(h, w_ref[...].astype(BF16), preferred_element_type=F32))

    def park(y, idx):
        if idx == 0:
            c_ref[...] = jnp.sum(_fold_lanes(y), axis=-1, keepdims=True) * (1.0 / GMLP_COLS)
        dv = y - c_ref[...]
        if idx == 0:
            s1_ref[...] = _fold_lanes(dv)
            s2_ref[...] = _fold_lanes(dv * dv)
        else:
            s1_ref[...] += _fold_lanes(dv)
            s2_ref[...] += _fold_lanes(dv * dv)
        v_ref[idx] = y

    def mix(u, first):
        s = j - GMLP_P1
        vn = ((v_ref[s] - mu_ref[...]) * rstd_ref[...] * lg_ref[...] + lb_ref[...]).astype(BF16)
        rows = []
        for c in range(tm // CHUNK):
            cols = []
            for gg in range(GMLP_GPS):
                vg = vn[c * CHUNK:(c + 1) * CHUNK, gg * GMLP_GD:(gg + 1) * GMLP_GD]
                cols.append(jnp.dot(ws_ref[gg].astype(BF16), vg, preferred_element_type=F32) + bs_ref[gg])
            rows.append(jnp.concatenate(cols, axis=1))
        a = (u * jnp.concatenate(rows, axis=0)).astype(BF16)
        t = jnp.dot(a, wo_ref[...].astype(BF16), preferred_element_type=F32)
        return t if first else o_ref[...] + t

    @pl.when(j == 0)
    def _():
        h = _adaln(x_ref[...], g2_ref[...], sh_ref[...], sc_ref[...]).astype(BF16)
        h_ref[...] = h
        park(slab(h), 0)
        park(slab(h, winb_ref), 1)

    for step in range(1, GMLP_P1):
        @pl.when(j == step)
        def _(step=step):
            park(slab(h_ref[...]), 2 * step)
            park(slab(h_ref[...], winb_ref), 2 * step + 1)

    @pl.when(j == GMLP_P1)
    def _():
        d1 = jnp.sum(s1_ref[...], axis=-1, keepdims=True) * (1.0 / GMLP_HALF)
        d2 = jnp.sum(s2_ref[...], axis=-1, keepdims=True) * (1.0 / GMLP_HALF)
        mu_ref[...] = c_ref[...] + d1
        rstd_ref[...] = lax.rsqrt(d2 - d1 * d1 + EPS)
        o_ref[...] = mix(slab(h_ref[...]), True)

    @pl.when((j > GMLP_P1) & (j < last))
    def _():
        o_ref[...] = mix(slab(h_ref[...]), False)

    @pl.when(j == last)
    def _():
        m = mix(slab(h_ref[...]), False)
        o_ref[...] = x_ref[...] + gt_ref[...] * _rms(m, g3_ref[...])


def _gmlp(x, mods, gains, w_in, ln_g, ln_b, w_s, b_s, w_out, widx, layer, tm=1024):
    ns, p1 = GMLP_SLABS, GMLP_P1

    def u_slab(j):
        return jnp.maximum(j - p1, 0)

    def win_a(j):
        return jnp.where(j < p1, ns + 2 * j, j - p1)

    def win_b(j):
        return jnp.where(j < p1, ns + 2 * j + 1, ns + 1)

    return pl.pallas_call(
        functools.partial(_gmlp_kernel, tm=tm),
        grid=(TOK // tm, p1 + ns),
        in_specs=[
            pl.BlockSpec((tm, D), lambda i, j: (i, 0)),
            _mod_spec(layer, 3, tm, 2),
            _mod_spec(layer, 4, tm, 2),
            _gain_spec(layer, 2, 2),
            pl.BlockSpec((None, D, GMLP_COLS), lambda i, j: (widx, 0, win_a(j))),
            pl.BlockSpec((None, D, GMLP_COLS), lambda i, j: (widx, 0, win_b(j))),
            pl.BlockSpec((None, 1, GMLP_COLS), lambda i, j: (widx, 0, u_slab(j))),
            pl.BlockSpec((None, 1, GMLP_COLS), lambda i, j: (widx, 0, u_slab(j))),
            pl.BlockSpec((None, GMLP_GPS, CHUNK, CHUNK), lambda i, j: (widx, u_slab(j), 0, 0)),
            pl.BlockSpec((None, GMLP_GPS, CHUNK, 1), lambda i, j: (widx, u_slab(j), 0, 0)),
            pl.BlockSpec((None, GMLP_COLS, D), lambda i, j: (widx, u_slab(j), 0)),
            _mod_spec(layer, 5, tm, 2),
            _gain_spec(layer, 3, 2),
        ],
        out_specs=pl.BlockSpec((tm, D), lambda i, j: (i, 0)),
        out_shape=jax.ShapeDtypeStruct((TOK, D), F32),
        scratch_shapes=[pltpu.VMEM((tm, D), BF16), pltpu.VMEM((ns, tm, GMLP_COLS), F32),
                        pltpu.VMEM((tm, 1), F32), pltpu.VMEM((tm, LANES), F32), pltpu.VMEM((tm, LANES), F32),
                        pltpu.VMEM((tm, 1), F32), pltpu.VMEM((tm, 1), F32)],
        compiler_params=_cparams(("arbitrary", "arbitrary"), VMEM_LIMIT_BIG),
        name=f"gmlp_l{layer}",
    )(x, mods, mods, gains, w_in, w_in, ln_g, ln_b, w_s, b_s, w_out, mods, gains)


def _rope_tables():
    pos = jnp.arange(LAT_LEN)
    pos_r = (pos // GRID_W).astype(F32)
    pos_c = (pos % GRID_W).astype(F32)
    inv = ROPE_BASE ** (-jnp.arange(ROT_PAIRS, dtype=F32) / ROT_PAIRS)
    ang_r = pos_r[:, None] * inv
    ang_c = pos_c[:, None] * inv
    cos = jnp.concatenate([jnp.cos(ang_r)] * 2 + [jnp.cos(ang_c)] * 2, axis=1)
    sin = jnp.concatenate([-jnp.sin(ang_r), jnp.sin(ang_r), -jnp.sin(ang_c), jnp.sin(ang_c)], axis=1)
    reps = LANES // HEAD_DIM
    return jnp.tile(cos, (1, reps)), jnp.tile(sin, (1, reps))


def _qkv_kernel(x_ref, sh_ref, sc_ref, g_ref, w_ref, cos_ref, sin_ref, o_ref, kc_ref, vc_ref, *, tm):
    i = pl.program_id(0)
    h = _adaln(x_ref[...], g_ref[...], sh_ref[...], sc_ref[...])
    y = jnp.dot(h.astype(BF16), w_ref[...].astype(BF16), preferred_element_type=F32)

    @pl.when(i * tm < CTX_TOK)
    def _():
        o_ref[...] = y
        kc_ref[...] = y[:, Q_DIM:Q_DIM + KV_DIM]
        vc_ref[...] = y[:, Q_DIM + KV_DIM:]

    @pl.when(i * tm >= CTX_TOK)
    def _():
        cos = cos_ref[...]
        sin = sin_ref[...]
        lane = lax.broadcasted_iota(jnp.int32, (tm, LANES), 1)
        first = (lane % (2 * ROT_PAIRS)) < ROT_PAIRS
        for cb in range((Q_DIM + KV_DIM) // LANES):
            t = y[:, cb * LANES:(cb + 1) * LANES]
            partner = jnp.where(first, pltpu.roll(t, LANES - ROT_PAIRS, axis=1),
                                pltpu.roll(t, ROT_PAIRS, axis=1))
            o_ref[:, cb * LANES:(cb + 1) * LANES] = t * cos + partner * sin
        o_ref[:, Q_DIM + KV_DIM:] = y[:, Q_DIM + KV_DIM:]


def _qkv_proj(x, mods, gains, w, widx, layer, tm=512):
    cos, sin = _rope_tables()
    per_seq = LAT_LEN // tm
    tab = pl.BlockSpec((tm, LANES), lambda i: (jnp.maximum(i - CTX_TOK // tm, 0) % per_seq, 0))
    return pl.pallas_call(
        functools.partial(_qkv_kernel, tm=tm),
        grid=(TOK // tm,),
        in_specs=[
            pl.BlockSpec((tm, D), lambda i: (i, 0)),
            _mod_spec(layer, 3, tm, 1),
            _mod_spec(layer, 4, tm, 1),
            _gain_spec(layer, 2, 1),
            pl.BlockSpec((None, D, QKV_DIM), lambda i: (widx, 0, 0)),
            tab, tab,
        ],
        out_specs=[pl.BlockSpec((tm, QKV_DIM), lambda i: (i, 0)),
                   _split_specs(tm, KV_DIM, 1)[0], _split_specs(tm, KV_DIM, 1)[0]],
        out_shape=[jax.ShapeDtypeStruct((TOK, QKV_DIM), F32),
                   jax.ShapeDtypeStruct((CTX_TOK, KV_DIM), F32), jax.ShapeDtypeStruct((CTX_TOK, KV_DIM), F32)],
        compiler_params=_cparams(("arbitrary",)),
        name=f"qkv_l{layer}",
    )(x, mods, mods, gains, w, cos, sin)


def _attend(q4, keys, vals, sink_rep, masks, merge_sums):
    logits = []
    for k, mk in zip(keys, masks):
        s = lax.dot_general(q4, k, (((1,), (1,)), ((), ())), preferred_element_type=F32) * ATTN_SCALE
        if mk is not None:
            s = jnp.where(mk, s, NEG_INF)
        logits.append([s[:, cb * LANES:(cb + 1) * LANES] for cb in range(s.shape[1] // LANES)])
    mx = None
    for blocks in logits:
        for blk in blocks:
            mx = blk if mx is None else jnp.maximum(mx, blk)
    m = jnp.maximum(sink_rep, jnp.max(mx, axis=-1, keepdims=True))
    acc = rs = None
    for blocks, v in zip(logits, vals):
        p = jnp.concatenate([jnp.exp(blk - m) for blk in blocks], axis=1).astype(BF16)
        if merge_sums:
            v = jnp.concatenate([v, jnp.ones_like(v)], axis=1)
        else:
            ps = jnp.dot(p, jnp.ones((p.shape[1], LANES), BF16), preferred_element_type=F32)
            rs = ps if rs is None else rs + ps
        pv = jnp.dot(p, v, preferred_element_type=F32)
        acc = pv if acc is None else acc + pv
    if merge_sums:
        denom = pltpu.roll(acc, HEAD_DIM, axis=1) + jnp.exp(sink_rep - m)
        return (acc / denom)[:, :HEAD_DIM]
    return acc / (rs + jnp.exp(sink_rep - m))[:, :HEAD_DIM]


def _attn_ctx_kernel(sink_ref, q_ref, k_ref, v_ref, o_ref):
    rows = CTX_LEN
    outs = [None] * N_Q_HEADS
    for hk in range(N_KV_HEADS):
        k = k_ref[:, hk * HEAD_DIM:(hk + 1) * HEAD_DIM].astype(BF16)
        v = v_ref[:, hk * HEAD_DIM:(hk + 1) * HEAD_DIM].astype(BF16)
        heads = [hk * Q_PER_KV + g for g in range(Q_PER_KV)]
        q4 = jnp.concatenate([q_ref[:, h * HEAD_DIM:(h + 1) * HEAD_DIM] for h in heads], axis=0).astype(BF16)
        sink_rep = jnp.concatenate([jnp.full((rows, LANES), sink_ref[h], F32) for h in heads], axis=0)
        o4 = _attend(q4, [k], [v], sink_rep, [None], merge_sums=False)
        for g, h in enumerate(heads):
            outs[h] = o4[g * rows:(g + 1) * rows]
    o_ref[...] = jnp.concatenate(outs, axis=1).astype(o_ref.dtype)


def _attn_lat_kernel(sink_ref, q_ref, kp_ref, kc_ref, kn_ref, vp_ref, vc_ref, vn_ref, ck_ref, cv_ref,
                     o_ref, *, n_blk):
    qi = pl.program_id(1)
    rows = CHUNK
    r4 = Q_PER_KV * rows
    a = lax.broadcasted_iota(jnp.int32, (r4, CHUNK), 0) % rows
    s = lax.broadcasted_iota(jnp.int32, (r4, CHUNK), 1)
    mask_prev = (s >= a) & (qi > 0)
    mask_next = (s <= a) & (qi < n_blk - 1)
    outs = [None] * N_Q_HEADS
    for hk in range(N_KV_HEADS):
        sl = slice(hk * HEAD_DIM, (hk + 1) * HEAD_DIM)
        keys = [kp_ref[:, sl].astype(BF16), kc_ref[:, sl].astype(BF16), kn_ref[:, sl].astype(BF16),
                ck_ref[:, sl].astype(BF16)]
        vals = [vp_ref[:, sl].astype(BF16), vc_ref[:, sl].astype(BF16), vn_ref[:, sl].astype(BF16),
                cv_ref[:, sl].astype(BF16)]
        heads = [hk * Q_PER_KV + g for g in range(Q_PER_KV)]
        q4 = jnp.concatenate([q_ref[:, h * HEAD_DIM:(h + 1) * HEAD_DIM] for h in heads], axis=0).astype(BF16)
        sink_rep = jnp.concatenate([jnp.full((rows, LANES), sink_ref[h], F32) for h in heads], axis=0)
        o4 = _attend(q4, keys, vals, sink_rep, [mask_prev, None, mask_next, None], merge_sums=True)
        for g, h in enumerate(heads):
            outs[h] = o4[g * rows:(g + 1) * rows]
    o_ref[...] = jnp.concatenate(outs, axis=1).astype(o_ref.dtype)


def _attention(qkv, sinks, ctx_k, ctx_v):
    smem = pl.BlockSpec(memory_space=pltpu.SMEM)
    kcol, vcol = Q_DIM // KV_DIM, Q_DIM // KV_DIM + 1
    ctx_blk = CTX_LEN
    o_ctx = pl.pallas_call(
        _attn_ctx_kernel,
        grid=(N_CTX_SEQ,),
        in_specs=[
            smem,
            pl.BlockSpec((ctx_blk, Q_DIM), lambda b: (b, 0)),
            pl.BlockSpec((ctx_blk, KV_DIM), lambda b: (b, kcol)),
            pl.BlockSpec((ctx_blk, KV_DIM), lambda b: (b, vcol)),
        ],
        out_specs=pl.BlockSpec((ctx_blk, Q_DIM), lambda b: (b, 0)),
        out_shape=jax.ShapeDtypeStruct((CTX_TOK, Q_DIM), BF16),
        compiler_params=_cparams(("arbitrary",)),
        name="attn_ctx",
    )(sinks, qkv, qkv, qkv)

    n_blk = LAT_LEN // CHUNK
    base = CTX_TOK // CHUNK

    def rb(b, qi, off):
        return base + b * n_blk + jnp.clip(qi + off, 0, n_blk - 1)

    def kv_spec(col, off):
        return pl.BlockSpec((CHUNK, KV_DIM), lambda b, qi: (rb(b, qi, off), col))

    o_lat = pl.pallas_call(
        functools.partial(_attn_lat_kernel, n_blk=n_blk),
        grid=(N_LAT_SEQ, n_blk),
        in_specs=[
            smem,
            pl.BlockSpec((CHUNK, Q_DIM), lambda b, qi: (rb(b, qi, 0), 0)),
            kv_spec(kcol, -1), kv_spec(kcol, 0), kv_spec(kcol, 1),
            kv_spec(vcol, -1), kv_spec(vcol, 0), kv_spec(vcol, 1),
            pl.BlockSpec((None, CTX_LEN, KV_DIM), lambda b, qi: (b, 0, 0)),
            pl.BlockSpec((None, CTX_LEN, KV_DIM), lambda b, qi: (b, 0, 0)),
        ],
        out_specs=pl.BlockSpec((CHUNK, Q_DIM), lambda b, qi: (b * n_blk + qi, 0)),
        out_shape=jax.ShapeDtypeStruct((LAT_TOK, Q_DIM), BF16),
        compiler_params=_cparams(("arbitrary", "arbitrary")),
        name="attn_lat",
    )(sinks, qkv, qkv, qkv, qkv, qkv, qkv, qkv, ctx_k, ctx_v)
    return o_ctx, o_lat


def _softplus(x):
    return jnp.maximum(x, 0.0) + jnp.log1p(jnp.exp(-jnp.abs(x)))


SSM_TN = 1024
SSM_Z_STEPS = SSM_INNER // SSM_TN
SSM_C_STEPS = SSM_CONV_DIM // SSM_TN


def _ssm_in_kernel(x_ref, sh_ref, sc_ref, g_ref, w_ref, cw_ref, cb_ref, wdt_ref, bdt_ref,
                   z_ref, xbc_ref, dt_ref, h_ref, *, tm):
    i = pl.program_id(0)
    j = pl.program_id(1)

    @pl.when(j == 0)
    def _():
        h = _adaln(x_ref[...], g_ref[...], sh_ref[...], sc_ref[...]).astype(BF16)
        h_ref[...] = h
        y = jnp.dot(h, wdt_ref[...].astype(BF16), preferred_element_type=F32)
        dt_ref[...] = _softplus(y + bdt_ref[...])
        z_ref[...] = jnp.dot(h, w_ref[...].astype(BF16), preferred_element_type=F32).astype(BF16)

    @pl.when((j > 0) & (j < SSM_Z_STEPS))
    def _():
        z_ref[...] = jnp.dot(h_ref[...], w_ref[...].astype(BF16), preferred_element_type=F32).astype(BF16)

    @pl.when(j >= SSM_Z_STEPS)
    def _():
        y = jnp.dot(h_ref[...], w_ref[...].astype(BF16), preferred_element_type=F32)
        seq_len = jnp.where(i * tm < CTX_TOK, CTX_LEN, LAT_LEN)
        pos = lax.broadcasted_iota(jnp.int32, (tm, 1), 0) & (seq_len - 1)
        up = jnp.where(pos == 0, 0.0, pltpu.roll(y, 1, axis=0))
        dn = jnp.where(pos == seq_len - 1, 0.0, pltpu.roll(y, tm - 1, axis=0))
        c = up * cw_ref[0:1, :] + y * cw_ref[1:2, :] + dn * cw_ref[2:3, :] + cb_ref[...]
        xbc_ref[...] = _silu(c).astype(BF16)


def _ssm_in(x, mods, gains, w, widx, conv_w, conv_b, w_dt, b_dt, layer, tm=1024):
    tn = SSM_TN
    n_main = SSM_Z_STEPS + SSM_C_STEPS

    def conv_col(j):
        return jnp.clip(j - SSM_Z_STEPS, 0, SSM_C_STEPS - 1)

    return pl.pallas_call(
        functools.partial(_ssm_in_kernel, tm=tm),
        grid=(TOK // tm, n_main),
        in_specs=[
            pl.BlockSpec((tm, D), lambda i, j: (i, 0)),
            _mod_spec(layer, 3, tm, 2),
            _mod_spec(layer, 4, tm, 2),
            _gain_spec(layer, 2, 2),
            pl.BlockSpec((None, D, tn), lambda i, j: (widx, 0, j)),
            pl.BlockSpec((3, tn), lambda i, j: (0, conv_col(j))),
            pl.BlockSpec((1, tn), lambda i, j: (0, conv_col(j))),
            pl.BlockSpec((D, LANES), lambda i, j: (0, 0)),
            pl.BlockSpec((1, LANES), lambda i, j: (0, 0)),
        ],
        out_specs=[
            pl.BlockSpec((tm, tn), lambda i, j: (i, jnp.minimum(j, SSM_Z_STEPS - 1))),
            pl.BlockSpec((tm, tn), lambda i, j: (i, conv_col(j))),
            pl.BlockSpec((tm, LANES), lambda i, j: (i, 0)),
        ],
        out_shape=[jax.ShapeDtypeStruct((TOK, SSM_INNER), BF16), jax.ShapeDtypeStruct((TOK, SSM_CONV_DIM), BF16),
                   jax.ShapeDtypeStruct((TOK, LANES), F32)],
        scratch_shapes=[pltpu.VMEM((tm, D), BF16)],
        compiler_params=_cparams(("arbitrary", "arbitrary")),
        name=f"ssm_in_l{layer}",
    )(x, mods, mods, gains, w, conv_w, conv_b, w_dt, b_dt)


def _split3(q):
    hi = q.astype(BF16)
    r1 = q - hi.astype(F32)
    mid = r1.astype(BF16)
    lo = (r1 - mid.astype(F32)).astype(BF16)
    return hi, mid, lo


def _spread(q, r_ref):
    hi, mid, lo = _split3(q[:, :SSM_HEADS])
    return jnp.dot(jnp.concatenate([hi, mid, lo], axis=1), r_ref[...], preferred_element_type=F32)


def _spread_consts():
    r = jnp.arange(3 * SSM_HEADS) % SSM_HEADS
    r1 = (jnp.arange(SSM_HEADS * SSM_P)[None, :] // SSM_P == r[:, None]).astype(BF16)
    r2 = (jnp.arange(SSM_HEADS * CHUNK)[None, :] // CHUNK == r[:, None]).astype(BF16)
    return r1, r2


def _ssd_chain(xbc, dt_all, a_ref, r1_ref, r2_ref, hs_ref, bwd):
    L = CHUNK
    d = 1 if bwd else 0
    xs = xbc[:, :SSM_INNER].astype(F32)
    bcb = xbc[:, SSM_INNER:]

    dt = pltpu.roll(dt_all, LANES - SSM_HEADS, axis=1) if bwd else dt_all
    a_row = a_ref[d:d + 1, :]
    ii = lax.broadcasted_iota(jnp.int32, (L, L), 0)
    jj = lax.broadcasted_iota(jnp.int32, (L, L), 1)
    tri = (jj >= ii) if bwd else (ii >= jj)
    tri_b = jnp.where(tri, 1.0, 0.0).astype(BF16)
    neg_mask = jnp.where(tri, 0.0, -jnp.inf)
    hi_p, mid_p, lo_p = _split3(dt * a_row)
    acum = (jnp.dot(tri_b, hi_p, preferred_element_type=F32)
            + jnp.dot(tri_b, mid_p, preferred_element_type=F32)
            + jnp.dot(tri_b, lo_p, preferred_element_type=F32))
    end = 0 if bwd else L - 1
    total = acum[end:end + 1, :]
    acum2 = acum * LOG2E
    acum_t = acum2.T
    cdec_t = jnp.exp2(acum_t[:, end:end + 1])

    e_acum = _spread(acum2, r2_ref)
    e_dt = _spread(dt, r1_ref)
    e_end = _spread(dt * jnp.exp(total - acum), r1_ref)
    e_in = _spread(jnp.exp(acum), r1_ref)
    xdt = (xs * e_dt).astype(BF16)
    xw = (xs * e_end).astype(BF16)
    left = lax.broadcasted_iota(jnp.int32, (L, LANES), 1) < SSM_P

    y_parts = []
    for g in range(SSM_GROUPS):
        bm = bcb[:, g * SSM_STATE:(g + 1) * SSM_STATE]
        cm = bcb[:, SSM_GN + g * SSM_STATE:SSM_GN + (g + 1) * SSM_STATE]
        cb = lax.dot_general(cm, bm, (((1,), (1,)), ((), ())), preferred_element_type=F32)
        gsl = slice(g * SSM_HG * SSM_P, (g + 1) * SSM_HG * SSM_P)
        hprev = hs_ref[d, gsl, :]
        y_off = lax.dot_general(cm, hprev.astype(BF16), (((1,), (1,)), ((), ())),
                                preferred_element_type=F32)
        y_pairs = []
        for pr in range(SSM_HG // 2):
            wm = []
            for h in (g * SSM_HG + 2 * pr, g * SSM_HG + 2 * pr + 1):
                seg = e_acum[:, h * L:(h + 1) * L] - acum_t[h:h + 1, :] + neg_mask
                wm.append((cb * jnp.exp2(seg)).astype(BF16))
            pair = xdt[:, (g * SSM_HG + 2 * pr) * SSM_P:(g * SSM_HG + 2 * pr + 2) * SSM_P]
            rhs = jnp.concatenate([jnp.where(left, pair, jnp.zeros_like(pair)),
                                   jnp.where(left, jnp.zeros_like(pair), pair)], axis=0)
            y_pairs.append(jnp.dot(jnp.concatenate(wm, axis=1), rhs, preferred_element_type=F32))
        y_parts.append(jnp.concatenate(y_pairs, axis=1) + y_off * e_in[:, gsl])
        st = lax.dot_general(xw[:, gsl], bm, (((0,), (0,)), ((), ())), preferred_element_type=F32)
        dec = jnp.concatenate([jnp.broadcast_to(cdec_t[g * SSM_HG + k:g * SSM_HG + k + 1, :], (SSM_P, 1))
                               for k in range(SSM_HG)], axis=0)
        hs_ref[d, gsl, :] = dec * hprev + st
    return jnp.concatenate(y_parts, axis=1), xs


def _ssd_kernel(*refs, nc, has_h0, emit_state):
    (xa_ref, xb_ref, dta_ref, dtb_ref, za_ref, zb_ref, a_ref, dsk_ref, ng_ref, r1_ref, r2_ref,
     wo_ref, xres_ref, gt_ref, g3_ref) = refs[:15]
    pos = 15
    h0_ref = None
    if has_h0:
        h0_ref = refs[pos]
        pos += 1
    y_ref = refs[pos]
    pos += 1
    st_ref = None
    if emit_state:
        st_ref = refs[pos]
        pos += 1
    hs_ref, yacc_ref = refs[pos:pos + 2]

    s = pl.program_id(1)
    L = CHUNK
    half = nc // 2
    off_a = pl.multiple_of(s * L, L)
    off_b = pl.multiple_of((nc - 1 - s) * L, L)

    @pl.when(s == 0)
    def _():
        hs_ref[...] = h0_ref[...] if has_h0 else jnp.zeros_like(hs_ref)

    ya, xs_a = _ssd_chain(xa_ref[...], dta_ref[...], a_ref, r1_ref, r2_ref, hs_ref, bwd=False)
    ya = ya + dsk_ref[...] * xs_a
    yb, _ = _ssd_chain(xb_ref[...], dtb_ref[...], a_ref, r1_ref, r2_ref, hs_ref, bwd=True)

    @pl.when(s < half)
    def _():
        yacc_ref[pl.ds(off_a, L), :] = ya
        yacc_ref[pl.ds(off_b, L), :] = yb

    @pl.when(s >= half)
    def _():
        for off, y_new, z_ref in ((off_a, ya, za_ref), (off_b, yb, zb_ref)):
            yt = (yacc_ref[pl.ds(off, L), :] + y_new) * _silu(z_ref[...].astype(F32))
            yn = _rms(yt, ng_ref[...]).astype(BF16)
            m = jnp.dot(yn, wo_ref[...], preferred_element_type=F32)
            y_ref[pl.ds(off, L), :] = xres_ref[pl.ds(off, L), :] + gt_ref[...] * _rms(m, g3_ref[...])

    if emit_state:
        @pl.when(s == nc - 1)
        def _():
            st_ref[...] = hs_ref[...]


def _ssd(z, xbc, dt, a_pad, dsk, norm_g, r1, r2, w_out, x, mods, gains, layer, h0, *,
         seq0, n_seq, seq_len, emit_state):
    nc = seq_len // CHUNK
    half = nc // 2
    chunk0 = seq0 // CHUNK
    seq_blk0 = seq0 // seq_len
    mod_base = layer * MOD_ROWS * N_MOD + 5
    mod_row0 = 0 if seq0 < CTX_TOK else 1
    mod_step = 0 if seq0 < CTX_TOK else N_MOD

    def fwd_chunk(b, s):
        return chunk0 + b * nc + s

    def bwd_chunk(b, s):
        return chunk0 + b * nc + nc - 1 - s

    def late(s):
        return jnp.maximum(s, half)

    const = lambda b, s: (0, 0)
    in_specs = [
        pl.BlockSpec((CHUNK, SSM_CONV_DIM), lambda b, s: (fwd_chunk(b, s), 0)),
        pl.BlockSpec((CHUNK, SSM_CONV_DIM), lambda b, s: (bwd_chunk(b, s), 0)),
        pl.BlockSpec((CHUNK, LANES), lambda b, s: (fwd_chunk(b, s), 0)),
        pl.BlockSpec((CHUNK, LANES), lambda b, s: (bwd_chunk(b, s), 0)),
        pl.BlockSpec((CHUNK, SSM_INNER), lambda b, s: (fwd_chunk(b, late(s)), 0)),
        pl.BlockSpec((CHUNK, SSM_INNER), lambda b, s: (bwd_chunk(b, late(s)), 0)),
        pl.BlockSpec((2, LANES), const),
        pl.BlockSpec((1, SSM_INNER), const),
        pl.BlockSpec((1, SSM_INNER), const),
        pl.BlockSpec(r1.shape, const),
        pl.BlockSpec(r2.shape, const),
        pl.BlockSpec((SSM_INNER, D), const, pipeline_mode=pl.Buffered(1)),
        pl.BlockSpec((seq_len, D), lambda b, s: (seq_blk0 + b, 0)),
        pl.BlockSpec((None, 1, D), lambda b, s: (mod_base + mod_row0 * N_MOD + b * mod_step, 0, 0)),
        pl.BlockSpec((None, 1, D), lambda b, s: (layer * 6 + 3, 0, 0)),
    ]
    args = [xbc, xbc, dt, dt, z, z, a_pad, dsk, norm_g, r1, r2, w_out, x, mods, gains]
    if h0 is not None:
        in_specs.append(pl.BlockSpec((None, 2, SSM_INNER, SSM_STATE), lambda b, s: (b, 0, 0, 0)))
        args.append(h0)
    out_specs = [pl.BlockSpec((seq_len, D), lambda b, s: (b, 0))]
    out_shape = [jax.ShapeDtypeStruct((n_seq * seq_len, D), F32)]
    if emit_state:
        out_specs.append(pl.BlockSpec((None, 2, SSM_INNER, SSM_STATE), lambda b, s: (b, 0, 0, 0)))
        out_shape.append(jax.ShapeDtypeStruct((n_seq, 2, SSM_INNER, SSM_STATE), F32))
    return pl.pallas_call(
        functools.partial(_ssd_kernel, nc=nc, has_h0=h0 is not None, emit_state=emit_state),
        grid=(n_seq, nc),
        in_specs=in_specs,
        out_specs=out_specs,
        out_shape=out_shape,
        scratch_shapes=[pltpu.VMEM((2, SSM_INNER, SSM_STATE), F32), pltpu.VMEM((seq_len, SSM_INNER), F32)],
        compiler_params=_cparams(("arbitrary", "arbitrary")),
        name=f"ssd_{seq_len}",
    )(*args)


def kernel(x_prompt, x_sample, cache_k, cache_v, state_ssm, c, c_ctx, w_mod, b_mod, norm_g, ffn_in, ffn_out,
           gmlp_in, gmlp_ln_g, gmlp_ln_b, gmlp_ws, gmlp_bs, gmlp_out, attn_qkv, attn_sink, attn_out,
           ssm_in, ssm_conv_w, ssm_conv_b, ssm_dt_bias, ssm_a_log, ssm_d, ssm_norm, ssm_out):
    cond = jnp.concatenate([c_ctx[None, :], c, jnp.zeros((MOD_ROWS - 1 - N_LAT_SEQ, D), F32)], axis=0)
    mods = _modulation(cond, w_mod, b_mod).reshape(DEPTH * MOD_ROWS * N_MOD, 1, D)
    gains = norm_g.reshape(DEPTH * 6, 1, D)

    x = (x_prompt.reshape(CTX_TOK, D), x_sample.reshape(LAT_TOK, D))
    new_k = new_v = new_s = None
    for layer in range(DEPTH):
        x = _ffn(x, mods, gains, ffn_in, ffn_out, layer, 0, split_in=layer == 0)
        kind, j = layer % 3, layer // 3
        if kind == 0:
            x = _gmlp(x, mods, gains, gmlp_in, gmlp_ln_g[:, None, :], gmlp_ln_b[:, None, :], gmlp_ws,
                      gmlp_bs[..., None], gmlp_out, j, layer)
        elif kind == 1:
            qkv, kc, vc = _qkv_proj(x, mods, gains, attn_qkv, j, layer)
            new_k = kc.reshape(N_CTX_SEQ, CTX_LEN, N_KV_HEADS, HEAD_DIM)
            new_v = vc.reshape(N_CTX_SEQ, CTX_LEN, N_KV_HEADS, HEAD_DIM)
            o_ctx, o_lat = _attention(qkv, attn_sink[j],
                                      cache_k[:, j].reshape(N_LAT_SEQ, CTX_LEN, KV_DIM),
                                      cache_v[:, j].reshape(N_LAT_SEQ, CTX_LEN, KV_DIM))
            x = _out_proj(o_ctx, o_lat, attn_out, j, x, mods, gains, layer)
        else:
            pad = LANES - 2 * SSM_HEADS
            w_dt = jnp.pad(ssm_in[j][:, SSM_MAIN:], ((0, 0), (0, pad)))
            b_dt = jnp.pad(ssm_dt_bias[j].reshape(1, 2 * SSM_HEADS), ((0, 0), (0, pad)))
            z, xbc, dt = _ssm_in(x, mods, gains, ssm_in, j, ssm_conv_w[j], ssm_conv_b[j][None, :],
                                 w_dt, b_dt, layer)
            a_pad = jnp.pad(-jnp.exp(ssm_a_log[j]), ((0, 0), (0, LANES - SSM_HEADS)))
            dsk = jnp.repeat(ssm_d[j], SSM_P)[None, :]
            ng = ssm_norm[j][None, :]
            r1, r2 = _spread_consts()
            w_o = ssm_out[j].astype(BF16)
            x_ctx, st = _ssd(z, xbc, dt, a_pad, dsk, ng, r1, r2, w_o, x, mods, gains, layer, None,
                             seq0=0, n_seq=N_CTX_SEQ, seq_len=CTX_LEN, emit_state=True)
            (x_lat,) = _ssd(z, xbc, dt, a_pad, dsk, ng, r1, r2, w_o, x, mods, gains, layer,
                            state_ssm[:, j].reshape(N_LAT_SEQ, 2, SSM_INNER, SSM_STATE),
                            seq0=CTX_TOK, n_seq=N_LAT_SEQ, seq_len=LAT_LEN, emit_state=False)
            new_s = st.reshape(N_CTX_SEQ, 2, SSM_HEADS, SSM_P, SSM_STATE)
            x = (x_ctx, x_lat)
        x = _ffn(x, mods, gains, ffn_in, ffn_out, layer, 1, split_in=isinstance(x, tuple),
                 split_out=layer == DEPTH - 1)

    y_prompt = x[0].reshape(N_CTX_SEQ, CTX_LEN, D)
    y_sample = x[1].reshape(N_LAT_SEQ, LAT_LEN, D)
    return (y_prompt, y_sample, new_k[:, None], new_v[:, None], new_s[:, None])
```

```python
import functools
import math

import jax
import jax.numpy as jnp
from jax import lax
from jax.experimental import pallas as pl
from jax.experimental.pallas import tpu as pltpu

F32 = jnp.float32
BF16 = jnp.bfloat16

D = 1024
N_CTX_SEQ, CTX_LEN = 16, 256
N_LAT_SEQ, LAT_LEN = 4, 1024
CTX_TOK = N_CTX_SEQ * CTX_LEN
LAT_TOK = N_LAT_SEQ * LAT_LEN
TOK = CTX_TOK + LAT_TOK
DEPTH = 4
N_MOD = 9
MOD_ROWS = 8
D_FF = 2816
EPS = 1e-6
GRID_W = 64
GMLP_HALF = 3 * D
GMLP_GROUPS = 8
GMLP_GD = GMLP_HALF // GMLP_GROUPS
CHUNK = 128
HEAD_DIM = 64
N_Q_HEADS = 16
N_KV_HEADS = 4
Q_PER_KV = 4
Q_DIM = N_Q_HEADS * HEAD_DIM
KV_DIM = N_KV_HEADS * HEAD_DIM
QKV_DIM = Q_DIM + 2 * KV_DIM
ATTN_SCALE = HEAD_DIM ** -0.5
ROPE_BASE = 10000.0
ROT_PAIRS = HEAD_DIM // 4
NEG_INF = -1e30
SSM_INNER = 2 * D
SSM_HEADS = 32
SSM_P = 64
SSM_GROUPS = 4
SSM_HG = SSM_HEADS // SSM_GROUPS
SSM_STATE = 128
SSM_GN = SSM_GROUPS * SSM_STATE
SSM_CONV_DIM = SSM_INNER + 2 * SSM_GN
SSM_MAIN = SSM_INNER + SSM_CONV_DIM
LANES = 128
LOG2E = 1.0 / math.log(2.0)

VMEM_LIMIT = 56 * 1024 * 1024
VMEM_LIMIT_BIG = 60 * 1024 * 1024


def _cparams(sem, limit=VMEM_LIMIT):
    return pltpu.CompilerParams(dimension_semantics=sem, vmem_limit_bytes=limit)


def _sigmoid(x):
    return 1.0 / (1.0 + jnp.exp(-x))


def _silu(x):
    return x * _sigmoid(x)


def _rms(x, g):
    return x * lax.rsqrt(jnp.mean(x * x, axis=-1, keepdims=True) + EPS) * g


def _adaln(x, g, shift, scale):
    return _rms(x, g) * (1.0 + scale) + shift


def _mod_row(i, tm):
    t0 = i * tm
    return jnp.where(t0 < CTX_TOK, 0, 1 + (t0 - CTX_TOK) // LAT_LEN)


def _mod_spec(layer, k, tm, grid_rank):
    base = layer * MOD_ROWS * N_MOD + k
    if grid_rank == 1:
        return pl.BlockSpec((None, 1, D), lambda i: (base + _mod_row(i, tm) * N_MOD, 0, 0))
    return pl.BlockSpec((None, 1, D), lambda i, j: (base + _mod_row(i, tm) * N_MOD, 0, 0))


def _gain_spec(layer, k, grid_rank):
    idx = layer * 6 + k
    if grid_rank == 1:
        return pl.BlockSpec((None, 1, D), lambda i: (idx, 0, 0))
    return pl.BlockSpec((None, 1, D), lambda i, j: (idx, 0, 0))


def _split_specs(tm, width, grid_rank):
    na = CTX_TOK // tm
    if grid_rank == 1:
        return (pl.BlockSpec((tm, width), lambda i: (jnp.minimum(i, na - 1), 0)),
                pl.BlockSpec((tm, width), lambda i: (jnp.maximum(i - na, 0), 0)))
    return (pl.BlockSpec((tm, width), lambda i, j: (jnp.minimum(i, na - 1), 0)),
            pl.BlockSpec((tm, width), lambda i, j: (jnp.maximum(i - na, 0), 0)))


def _mod_kernel(cond_ref, w_ref, b_ref, o_ref):
    s = _silu(cond_ref[...]).astype(BF16)
    o_ref[...] = jnp.dot(s, w_ref[...].astype(BF16), preferred_element_type=F32) + b_ref[...]


def _modulation(cond, w_mod, b_mod):
    tn = 2304
    n = N_MOD * D
    return pl.pallas_call(
        _mod_kernel,
        grid=(DEPTH, n // tn),
        in_specs=[
            pl.BlockSpec((MOD_ROWS, D), lambda l, j: (0, 0)),
            pl.BlockSpec((None, D, tn), lambda l, j: (l, 0, j)),
            pl.BlockSpec((None, 1, tn), lambda l, j: (l, 0, j)),
        ],
        out_specs=pl.BlockSpec((None, MOD_ROWS, tn), lambda l, j: (l, 0, j)),
        out_shape=jax.ShapeDtypeStruct((DEPTH, MOD_ROWS, n), F32),
        compiler_params=_cparams(("arbitrary", "arbitrary")),
        name="modulation",
    )(cond, w_mod, b_mod.reshape(DEPTH, 1, n))


FFN_TF = 256
FFN_SLABS = D_FF // FFN_TF
FFN_SPS = 4
FFN_STEP_SLABS = (3, 4, 4)
FFN_STEPS = len(FFN_STEP_SLABS)
assert sum(FFN_STEP_SLABS) == FFN_SLABS and max(FFN_STEP_SLABS) == FFN_SPS


def _ffn_kernel(*refs, tm, split_in, split_out):
    n_x = 2 if split_in else 1
    n_o = 2 if split_out else 1
    x_refs = refs[:n_x]
    sh_ref, sc_ref, gt_ref, g0_ref, g1_ref = refs[n_x:n_x + 5]
    w0 = n_x + 5
    wg = refs[w0:w0 + FFN_SPS]
    wu = refs[w0 + FFN_SPS:w0 + 2 * FFN_SPS]
    wo = refs[w0 + 2 * FFN_SPS:w0 + 3 * FFN_SPS]
    o_refs = refs[w0 + 3 * FFN_SPS:w0 + 3 * FFN_SPS + n_o]
    h_ref, acc_ref = refs[w0 + 3 * FFN_SPS + n_o:]
    i = pl.program_id(0)
    j = pl.program_id(1)
    is_ctx = i * tm < CTX_TOK
    last = FFN_STEPS - 1

    def load_x():
        if split_in:
            return jnp.where(is_ctx, x_refs[0][...], x_refs[1][...])
        return x_refs[0][...]

    def swiglu_slabs(h, n):
        acts = []
        for k in range(n):
            g = jnp.dot(h, wg[k][...].astype(BF16), preferred_element_type=F32)
            u = jnp.dot(h, wu[k][...].astype(BF16), preferred_element_type=F32)
            acts.append((_silu(g) * u).astype(BF16))
        w = jnp.concatenate([wo[k][...].astype(BF16) for k in range(n)], axis=0)
        return jnp.dot(jnp.concatenate(acts, axis=1), w, preferred_element_type=F32)

    @pl.when(j == 0)
    def _():
        h = _adaln(load_x(), g0_ref[...], sh_ref[...], sc_ref[...]).astype(BF16)
        h_ref[...] = h
        acc_ref[...] = swiglu_slabs(h, FFN_STEP_SLABS[0])

    for step in range(1, last):
        @pl.when(j == step)
        def _(step=step):
            acc_ref[...] += swiglu_slabs(h_ref[...], FFN_STEP_SLABS[step])

    @pl.when(j == last)
    def _():
        f = acc_ref[...] + swiglu_slabs(h_ref[...], FFN_STEP_SLABS[last])
        res = load_x() + (0.5 * gt_ref[...]) * _rms(f, g1_ref[...])
        if split_out:
            @pl.when(is_ctx)
            def _():
                o_refs[0][...] = res

            @pl.when(jnp.logical_not(is_ctx))
            def _():
                o_refs[1][...] = res
        else:
            o_refs[0][...] = res


def _ffn(x, mods, gains, ffn_in, ffn_out, layer, which, split_in=False, split_out=False):
    tm, tf = 1024, FFN_TF
    k0 = 0 if which == 0 else 6
    gi = 0 if which == 0 else 4
    if split_in:
        x_specs = list(_split_specs(tm, D, 2))
    else:
        x_specs = [pl.BlockSpec((tm, D), lambda i, j: (i, 0))]
    x_args = list(x) if split_in else [x]
    if split_out:
        out_specs = list(_split_specs(tm, D, 2))
        out_shape = [jax.ShapeDtypeStruct((CTX_TOK, D), F32), jax.ShapeDtypeStruct((LAT_TOK, D), F32)]
    else:
        out_specs = pl.BlockSpec((tm, D), lambda i, j: (i, 0))
        out_shape = jax.ShapeDtypeStruct((TOK, D), F32)

    starts = [sum(FFN_STEP_SLABS[:s]) for s in range(FFN_STEPS)]
    table = []
    for k in range(FFN_SPS):
        col = [starts[s] + k if k < FFN_STEP_SLABS[s] else None for s in range(FFN_STEPS)]
        for s in reversed(range(FFN_STEPS)):
            if col[s] is None:
                col[s] = col[s + 1] if s + 1 < FFN_STEPS else col[s - 1]
        table.append(col)

    def slab(j, k):
        idx = table[k][FFN_STEPS - 1]
        for s in reversed(range(FFN_STEPS - 1)):
            idx = jnp.where(j == s, table[k][s], idx)
        return idx

    def spec_in(k, col0):
        return pl.BlockSpec((None, None, D, tf), lambda i, j: (layer, which, 0, col0 + slab(j, k)))

    def spec_out(k):
        return pl.BlockSpec((None, None, tf, D), lambda i, j: (layer, which, slab(j, k), 0))

    w_specs = ([spec_in(k, 0) for k in range(FFN_SPS)] + [spec_in(k, FFN_SLABS) for k in range(FFN_SPS)]
               + [spec_out(k) for k in range(FFN_SPS)])
    return pl.pallas_call(
        functools.partial(_ffn_kernel, tm=tm, split_in=split_in, split_out=split_out),
        grid=(TOK // tm, FFN_STEPS),
        in_specs=x_specs + [
            _mod_spec(layer, k0 + 0, tm, 2),
            _mod_spec(layer, k0 + 1, tm, 2),
            _mod_spec(layer, k0 + 2, tm, 2),
            _gain_spec(layer, gi, 2),
            _gain_spec(layer, gi + 1, 2),
        ] + w_specs,
        out_specs=out_specs,
        out_shape=out_shape,
        scratch_shapes=[pltpu.VMEM((tm, D), BF16), pltpu.VMEM((tm, D), F32)],
        compiler_params=_cparams(("arbitrary", "arbitrary"), VMEM_LIMIT_BIG),
        name=f"ffn_l{layer}_{which}",
    )(*x_args, mods, mods, mods, gains, gains, *([ffn_in] * (2 * FFN_SPS)), *([ffn_out] * FFN_SPS))


def _out_kernel(aa_ref, ab_ref, w_ref, x_ref, gt_ref, g_ref, o_ref, *, tm):
    is_ctx = pl.program_id(0) * tm < CTX_TOK
    wb = w_ref[...].astype(BF16)
    for r in range(2):
        rows = slice(r * (tm // 2), (r + 1) * (tm // 2))
        a = jnp.where(is_ctx, aa_ref[rows, :], ab_ref[rows, :])
        m = jnp.dot(a, wb, preferred_element_type=F32)
        o_ref[rows, :] = x_ref[rows, :] + gt_ref[...] * _rms(m, g_ref[...])


def _out_proj(a_ctx, a_lat, w, widx, x, mods, gains, layer, tm=1024):
    k = a_ctx.shape[1]
    sa, sb = _split_specs(tm, k, 1)
    return pl.pallas_call(
        functools.partial(_out_kernel, tm=tm),
        grid=(TOK // tm,),
        in_specs=[
            sa, sb,
            pl.BlockSpec((None, k, D), lambda i: (widx, 0, 0)),
            pl.BlockSpec((tm, D), lambda i: (i, 0)),
            _mod_spec(layer, 5, tm, 1),
            _gain_spec(layer, 3, 1),
        ],
        out_specs=pl.BlockSpec((tm, D), lambda i: (i, 0)),
        out_shape=jax.ShapeDtypeStruct((TOK, D), F32),
        compiler_params=_cparams(("arbitrary",)),
        name=f"outproj_l{layer}",
    )(a_ctx, a_lat, w, x, mods, gains)


GMLP_GPS = 2
GMLP_SLABS = GMLP_GROUPS // GMLP_GPS
GMLP_COLS = GMLP_GPS * GMLP_GD


def _gelu_exact(x):
    return 0.5 * x * (1.0 + lax.erf(x * (1.0 / math.sqrt(2.0))))


def _fold_lanes(v):
    acc = v[:, :LANES]
    for cb in range(1, v.shape[1] // LANES):
        acc = acc + v[:, cb * LANES:(cb + 1) * LANES]
    return acc


GMLP_P1 = GMLP_SLABS // 2


def _gmlp_kernel(x_ref, sh_ref, sc_ref, g2_ref, wina_ref, winb_ref, lg_ref, lb_ref, ws_ref, bs_ref, wo_ref,
                 gt_ref, g3_ref, o_ref, h_ref, v_ref, c_ref, s1_ref, s2_ref, mu_ref, rstd_ref, *, tm):
    j = pl.program_id(1)
    last = GMLP_P1 + GMLP_SLABS - 1

    def slab(h, w_ref=wina_ref):
        return _gelu_exact(jnp.dot(h, w_ref[...].astype(BF16), preferred_element_type=F32))

    def park(y, idx):
        if idx == 0:
            c_ref[...] = jnp.sum(_fold_lanes(y), axis=-1, keepdims=True) * (1.0 / GMLP_COLS)
        dv = y - c_ref[...]
        if idx == 0:
            s1_ref[...] = _fold_lanes(dv)
            s2_ref[...] = _fold_lanes(dv * dv)
        else:
            s1_ref[...] += _fold_lanes(dv)
            s2_ref[...] += _fold_lanes(dv * dv)
        v_ref[idx] = y

    def mix(u, first):
        s = j - GMLP_P1
        vn = ((v_ref[s] - mu_ref[...]) * rstd_ref[...] * lg_ref[...] + lb_ref[...]).astype(BF16)
        rows = []
        for c in range(tm // CHUNK):
            cols = []
            for gg in range(GMLP_GPS):
                vg = vn[c * CHUNK:(c + 1) * CHUNK, gg * GMLP_GD:(gg + 1) * GMLP_GD]
                cols.append(jnp.dot(ws_ref[gg].astype(BF16), vg, preferred_element_type=F32) + bs_ref[gg])
            rows.append(jnp.concatenate(cols, axis=1))
        a = (u * jnp.concatenate(rows, axis=0)).astype(BF16)
        t = jnp.dot(a, wo_ref[...].astype(BF16), preferred_element_type=F32)
        return t if first else o_ref[...] + t

    @pl.when(j == 0)
    def _():
        h = _adaln(x_ref[...], g2_ref[...], sh_ref[...], sc_ref[...]).astype(BF16)
        h_ref[...] = h
        park(slab(h), 0)
        park(slab(h, winb_ref), 1)

    for step in range(1, GMLP_P1):
        @pl.when(j == step)
        def _(step=step):
            park(slab(h_ref[...]), 2 * step)
            park(slab(h_ref[...], winb_ref), 2 * step + 1)

    @pl.when(j == GMLP_P1)
    def _():
        d1 = jnp.sum(s1_ref[...], axis=-1, keepdims=True) * (1.0 / GMLP_HALF)
        d2 = jnp.sum(s2_ref[...], axis=-1, keepdims=True) * (1.0 / GMLP_HALF)
        mu_ref[...] = c_ref[...] + d1
        rstd_ref[...] = lax.rsqrt(d2 - d1 * d1 + EPS)
        o_ref[...] = mix(slab(h_ref[...]), True)

    @pl.when((j > GMLP_P1) & (j < last))
    def _():
        o_ref[...] = mix(slab(h_ref[...]), False)

    @pl.when(j == last)
    def _():
        m = mix(slab(h_ref[...]), False)
        o_ref[...] = x_ref[...] + gt_ref[...] * _rms(m, g3_ref[...])


def _gmlp(x, mods, gains, w_in, ln_g, ln_b, w_s, b_s, w_out, widx, layer, tm=1024):
    ns, p1 = GMLP_SLABS, GMLP_P1

    def u_slab(j):
        return jnp.maximum(j - p1, 0)

    def win_a(j):
        return jnp.where(j < p1, ns + 2 * j, j - p1)

    def win_b(j):
        return jnp.where(j < p1, ns + 2 * j + 1, ns + 1)

    return pl.pallas_call(
        functools.partial(_gmlp_kernel, tm=tm),
        grid=(TOK // tm, p1 + ns),
        in_specs=[
            pl.BlockSpec((tm, D), lambda i, j: (i, 0)),
            _mod_spec(layer, 3, tm, 2),
            _mod_spec(layer, 4, tm, 2),
            _gain_spec(layer, 2, 2),
            pl.BlockSpec((None, D, GMLP_COLS), lambda i, j: (widx, 0, win_a(j))),
            pl.BlockSpec((None, D, GMLP_COLS), lambda i, j: (widx, 0, win_b(j))),
            pl.BlockSpec((None, 1, GMLP_COLS), lambda i, j: (widx, 0, u_slab(j))),
            pl.BlockSpec((None, 1, GMLP_COLS), lambda i, j: (widx, 0, u_slab(j))),
            pl.BlockSpec((None, GMLP_GPS, CHUNK, CHUNK), lambda i, j: (widx, u_slab(j), 0, 0)),
            pl.BlockSpec((None, GMLP_GPS, CHUNK, 1), lambda i, j: (widx, u_slab(j), 0, 0)),
            pl.BlockSpec((None, GMLP_COLS, D), lambda i, j: (widx, u_slab(j), 0)),
            _mod_spec(layer, 5, tm, 2),
            _gain_spec(layer, 3, 2),
        ],
        out_specs=pl.BlockSpec((tm, D), lambda i, j: (i, 0)),
        out_shape=jax.ShapeDtypeStruct((TOK, D), F32),
        scratch_shapes=[pltpu.VMEM((tm, D), BF16), pltpu.VMEM((ns, tm, GMLP_COLS), F32),
                        pltpu.VMEM((tm, 1), F32), pltpu.VMEM((tm, LANES), F32), pltpu.VMEM((tm, LANES), F32),
                        pltpu.VMEM((tm, 1), F32), pltpu.VMEM((tm, 1), F32)],
        compiler_params=_cparams(("arbitrary", "arbitrary"), VMEM_LIMIT_BIG),
        name=f"gmlp_l{layer}",
    )(x, mods, mods, gains, w_in, w_in, ln_g, ln_b, w_s, b_s, w_out, mods, gains)


def _rope_tables():
    pos = jnp.arange(LAT_LEN)
    pos_r = (pos // GRID_W).astype(F32)
    pos_c = (pos % GRID_W).astype(F32)
    inv = ROPE_BASE ** (-jnp.arange(ROT_PAIRS, dtype=F32) / ROT_PAIRS)
    ang_r = pos_r[:, None] * inv
    ang_c = pos_c[:, None] * inv
    cos = jnp.concatenate([jnp.cos(ang_r)] * 2 + [jnp.cos(ang_c)] * 2, axis=1)
    sin = jnp.concatenate([-jnp.sin(ang_r), jnp.sin(ang_r), -jnp.sin(ang_c), jnp.sin(ang_c)], axis=1)
    reps = LANES // HEAD_DIM
    return jnp.tile(cos, (1, reps)), jnp.tile(sin, (1, reps))


def _qkv_kernel(x_ref, sh_ref, sc_ref, g_ref, w_ref, cos_ref, sin_ref, o_ref, kc_ref, vc_ref, *, tm):
    i = pl.program_id(0)
    hr = tm // 2

    def proj(rows, wb):
        h = _adaln(x_ref[rows, :], g_ref[...], sh_ref[...], sc_ref[...])
        return jnp.dot(h.astype(BF16), wb, preferred_element_type=F32)

    @pl.when(i * tm < CTX_TOK)
    def _():
        wb = w_ref[...].astype(BF16)
        for r in range(2):
            rows = slice(r * hr, (r + 1) * hr)
            y = proj(rows, wb)
            o_ref[rows, :] = y
            kc_ref[rows, :] = y[:, Q_DIM:Q_DIM + KV_DIM]
            vc_ref[rows, :] = y[:, Q_DIM + KV_DIM:]

    @pl.when(i * tm >= CTX_TOK)
    def _():
        wb = w_ref[...].astype(BF16)
        lane = lax.broadcasted_iota(jnp.int32, (hr, LANES), 1)
        first = (lane % (2 * ROT_PAIRS)) < ROT_PAIRS
        for r in range(2):
            rows = slice(r * hr, (r + 1) * hr)
            y = proj(rows, wb)
            cos = cos_ref[rows, :]
            sin = sin_ref[rows, :]
            for cb in range((Q_DIM + KV_DIM) // LANES):
                t = y[:, cb * LANES:(cb + 1) * LANES]
                partner = jnp.where(first, pltpu.roll(t, LANES - ROT_PAIRS, axis=1),
                                    pltpu.roll(t, ROT_PAIRS, axis=1))
                o_ref[rows, cb * LANES:(cb + 1) * LANES] = t * cos + partner * sin
            o_ref[rows, Q_DIM + KV_DIM:] = y[:, Q_DIM + KV_DIM:]


def _qkv_proj(x, mods, gains, w, widx, layer, tm=1024):
    cos, sin = _rope_tables()
    per_seq = LAT_LEN // tm
    tab = pl.BlockSpec((tm, LANES), lambda i: (jnp.maximum(i - CTX_TOK // tm, 0) % per_seq, 0))
    return pl.pallas_call(
        functools.partial(_qkv_kernel, tm=tm),
        grid=(TOK // tm,),
        in_specs=[
            pl.BlockSpec((tm, D), lambda i: (i, 0)),
            _mod_spec(layer, 3, tm, 1),
            _mod_spec(layer, 4, tm, 1),
            _gain_spec(layer, 2, 1),
            pl.BlockSpec((None, D, QKV_DIM), lambda i: (widx, 0, 0)),
            tab, tab,
        ],
        out_specs=[pl.BlockSpec((tm, QKV_DIM), lambda i: (i, 0)),
                   _split_specs(tm, KV_DIM, 1)[0], _split_specs(tm, KV_DIM, 1)[0]],
        out_shape=[jax.ShapeDtypeStruct((TOK, QKV_DIM), F32),
                   jax.ShapeDtypeStruct((CTX_TOK, KV_DIM), F32), jax.ShapeDtypeStruct((CTX_TOK, KV_DIM), F32)],
        compiler_params=_cparams(("arbitrary",)),
        name=f"qkv_l{layer}",
    )(x, mods, mods, gains, w, cos, sin)


def _attend(q4, keys, vals, sink_rep, masks, merge_sums):
    logits = []
    for k, mk in zip(keys, masks):
        s = lax.dot_general(q4, k, (((1,), (1,)), ((), ())), preferred_element_type=F32) * ATTN_SCALE
        if mk is not None:
            s = jnp.where(mk, s, NEG_INF)
        logits.append([s[:, cb * LANES:(cb + 1) * LANES] for cb in range(s.shape[1] // LANES)])
    mx = None
    for blocks in logits:
        for blk in blocks:
            mx = blk if mx is None else jnp.maximum(mx, blk)
    m = jnp.maximum(sink_rep, jnp.max(mx, axis=-1, keepdims=True))
    acc = rs = None
    for blocks, v in zip(logits, vals):
        p = jnp.concatenate([jnp.exp(blk - m) for blk in blocks], axis=1).astype(BF16)
        if merge_sums:
            v = jnp.concatenate([v, jnp.ones_like(v)], axis=1)
        else:
            ps = jnp.dot(p, jnp.ones((p.shape[1], LANES), BF16), preferred_element_type=F32)
            rs = ps if rs is None else rs + ps
        pv = jnp.dot(p, v, preferred_element_type=F32)
        acc = pv if acc is None else acc + pv
    if merge_sums:
        denom = pltpu.roll(acc, HEAD_DIM, axis=1) + jnp.exp(sink_rep - m)
        return (acc / denom)[:, :HEAD_DIM]
    return acc / (rs + jnp.exp(sink_rep - m))[:, :HEAD_DIM]


def _attn_ctx_kernel(sink_ref, q_ref, k_ref, v_ref, o_ref):
    rows = CTX_LEN
    outs = [None] * N_Q_HEADS
    for hk in range(N_KV_HEADS):
        k = k_ref[:, hk * HEAD_DIM:(hk + 1) * HEAD_DIM].astype(BF16)
        v = v_ref[:, hk * HEAD_DIM:(hk + 1) * HEAD_DIM].astype(BF16)
        heads = [hk * Q_PER_KV + g for g in range(Q_PER_KV)]
        q4 = jnp.concatenate([q_ref[:, h * HEAD_DIM:(h + 1) * HEAD_DIM] for h in heads], axis=0).astype(BF16)
        sink_rep = jnp.concatenate([jnp.full((rows, LANES), sink_ref[h], F32) for h in heads], axis=0)
        o4 = _attend(q4, [k], [v], sink_rep, [None], merge_sums=False)
        for g, h in enumerate(heads):
            outs[h] = o4[g * rows:(g + 1) * rows]
    o_ref[...] = jnp.concatenate(outs, axis=1).astype(o_ref.dtype)


def _attn_lat_kernel(sink_ref, q_ref, kp_ref, kc_ref, kn_ref, vp_ref, vc_ref, vn_ref, ck_ref, cv_ref,
                     o_ref, *, n_blk):
    qi = pl.program_id(1)
    rows = CHUNK
    r4 = Q_PER_KV * rows
    a = lax.broadcasted_iota(jnp.int32, (r4, CHUNK), 0) % rows
    s = lax.broadcasted_iota(jnp.int32, (r4, CHUNK), 1)
    mask_prev = (s >= a) & (qi > 0)
    mask_next = (s <= a) & (qi < n_blk - 1)
    outs = [None] * N_Q_HEADS
    for hk in range(N_KV_HEADS):
        sl = slice(hk * HEAD_DIM, (hk + 1) * HEAD_DIM)
        keys = [kp_ref[:, sl].astype(BF16), kc_ref[:, sl].astype(BF16), kn_ref[:, sl].astype(BF16),
                ck_ref[:, sl].astype(BF16)]
        vals = [vp_ref[:, sl].astype(BF16), vc_ref[:, sl].astype(BF16), vn_ref[:, sl].astype(BF16),
                cv_ref[:, sl].astype(BF16)]
        heads = [hk * Q_PER_KV + g for g in range(Q_PER_KV)]
        q4 = jnp.concatenate([q_ref[:, h * HEAD_DIM:(h + 1) * HEAD_DIM] for h in heads], axis=0).astype(BF16)
        sink_rep = jnp.concatenate([jnp.full((rows, LANES), sink_ref[h], F32) for h in heads], axis=0)
        o4 = _attend(q4, keys, vals, sink_rep, [mask_prev, None, mask_next, None], merge_sums=True)
        for g, h in enumerate(heads):
            outs[h] = o4[g * rows:(g + 1) * rows]
    o_ref[...] = jnp.concatenate(outs, axis=1).astype(o_ref.dtype)


def _attention(qkv, sinks, ctx_k, ctx_v):
    smem = pl.BlockSpec(memory_space=pltpu.SMEM)
    kcol, vcol = Q_DIM // KV_DIM, Q_DIM // KV_DIM + 1
    ctx_blk = CTX_LEN
    o_ctx = pl.pallas_call(
        _attn_ctx_kernel,
        grid=(N_CTX_SEQ,),
        in_specs=[
            smem,
            pl.BlockSpec((ctx_blk, Q_DIM), lambda b: (b, 0)),
            pl.BlockSpec((ctx_blk, KV_DIM), lambda b: (b, kcol)),
            pl.BlockSpec((ctx_blk, KV_DIM), lambda b: (b, vcol)),
        ],
        out_specs=pl.BlockSpec((ctx_blk, Q_DIM), lambda b: (b, 0)),
        out_shape=jax.ShapeDtypeStruct((CTX_TOK, Q_DIM), BF16),
        compiler_params=_cparams(("arbitrary",)),
        name="attn_ctx",
    )(sinks, qkv, qkv, qkv)

    n_blk = LAT_LEN // CHUNK
    base = CTX_TOK // CHUNK

    def rb(b, qi, off):
        return base + b * n_blk + jnp.clip(qi + off, 0, n_blk - 1)

    def kv_spec(col, off):
        return pl.BlockSpec((CHUNK, KV_DIM), lambda b, qi: (rb(b, qi, off), col))

    o_lat = pl.pallas_call(
        functools.partial(_attn_lat_kernel, n_blk=n_blk),
        grid=(N_LAT_SEQ, n_blk),
        in_specs=[
            smem,
            pl.BlockSpec((CHUNK, Q_DIM), lambda b, qi: (rb(b, qi, 0), 0)),
            kv_spec(kcol, -1), kv_spec(kcol, 0), kv_spec(kcol, 1),
            kv_spec(vcol, -1), kv_spec(vcol, 0), kv_spec(vcol, 1),
            pl.BlockSpec((None, CTX_LEN, KV_DIM), lambda b, qi: (b, 0, 0)),
            pl.BlockSpec((None, CTX_LEN, KV_DIM), lambda b, qi: (b, 0, 0)),
        ],
        out_specs=pl.BlockSpec((CHUNK, Q_DIM), lambda b, qi: (b * n_blk + qi, 0)),
        out_shape=jax.ShapeDtypeStruct((LAT_TOK, Q_DIM), BF16),
        compiler_params=_cparams(("arbitrary", "arbitrary")),
        name="attn_lat",
    )(sinks, qkv, qkv, qkv, qkv, qkv, qkv, qkv, ctx_k, ctx_v)
    return o_ctx, o_lat


def _softplus(x):
    return jnp.maximum(x, 0.0) + jnp.log1p(jnp.exp(-jnp.abs(x)))


SSM_TN = 1024
SSM_Z_STEPS = SSM_INNER // SSM_TN
SSM_C_STEPS = SSM_CONV_DIM // SSM_TN


def _ssm_in_kernel(x_ref, sh_ref, sc_ref, g_ref, w_ref, cw_ref, cb_ref, wdt_ref, bdt_ref,
                   z_ref, xbc_ref, dt_ref, h_ref, *, tm):
    i = pl.program_id(0)
    j = pl.program_id(1)

    @pl.when(j == 0)
    def _():
        h = _adaln(x_ref[...], g_ref[...], sh_ref[...], sc_ref[...]).astype(BF16)
        h_ref[...] = h
        y = jnp.dot(h, wdt_ref[...].astype(BF16), preferred_element_type=F32)
        dt_ref[...] = _softplus(y + bdt_ref[...])
        z_ref[...] = jnp.dot(h, w_ref[...].astype(BF16), preferred_element_type=F32).astype(BF16)

    @pl.when((j > 0) & (j < SSM_Z_STEPS))
    def _():
        z_ref[...] = jnp.dot(h_ref[...], w_ref[...].astype(BF16), preferred_element_type=F32).astype(BF16)

    @pl.when(j >= SSM_Z_STEPS)
    def _():
        y = jnp.dot(h_ref[...], w_ref[...].astype(BF16), preferred_element_type=F32)
        seq_len = jnp.where(i * tm < CTX_TOK, CTX_LEN, LAT_LEN)
        pos = lax.broadcasted_iota(jnp.int32, (tm, 1), 0) & (seq_len - 1)
        up = jnp.where(pos == 0, 0.0, pltpu.roll(y, 1, axis=0))
        dn = jnp.where(pos == seq_len - 1, 0.0, pltpu.roll(y, tm - 1, axis=0))
        c = up * cw_ref[0:1, :] + y * cw_ref[1:2, :] + dn * cw_ref[2:3, :] + cb_ref[...]
        xbc_ref[...] = _silu(c).astype(BF16)


def _ssm_in(x, mods, gains, w, widx, conv_w, conv_b, w_dt, b_dt, layer, tm=1024):
    tn = SSM_TN
    n_main = SSM_Z_STEPS + SSM_C_STEPS

    def conv_col(j):
        return jnp.clip(j - SSM_Z_STEPS, 0, SSM_C_STEPS - 1)

    return pl.pallas_call(
        functools.partial(_ssm_in_kernel, tm=tm),
        grid=(TOK // tm, n_main),
        in_specs=[
            pl.BlockSpec((tm, D), lambda i, j: (i, 0)),
            _mod_spec(layer, 3, tm, 2),
            _mod_spec(layer, 4, tm, 2),
            _gain_spec(layer, 2, 2),
            pl.BlockSpec((None, D, tn), lambda i, j: (widx, 0, j)),
            pl.BlockSpec((3, tn), lambda i, j: (0, conv_col(j))),
            pl.BlockSpec((1, tn), lambda i, j: (0, conv_col(j))),
            pl.BlockSpec((D, LANES), lambda i, j: (0, 0)),
            pl.BlockSpec((1, LANES), lambda i, j: (0, 0)),
        ],
        out_specs=[
            pl.BlockSpec((tm, tn), lambda i, j: (i, jnp.minimum(j, SSM_Z_STEPS - 1))),
            pl.BlockSpec((tm, tn), lambda i, j: (i, conv_col(j))),
            pl.BlockSpec((tm, LANES), lambda i, j: (i, 0)),
        ],
        out_shape=[jax.ShapeDtypeStruct((TOK, SSM_INNER), BF16), jax.ShapeDtypeStruct((TOK, SSM_CONV_DIM), BF16),
                   jax.ShapeDtypeStruct((TOK, LANES), F32)],
        scratch_shapes=[pltpu.VMEM((tm, D), BF16)],
        compiler_params=_cparams(("arbitrary", "arbitrary")),
        name=f"ssm_in_l{layer}",
    )(x, mods, mods, gains, w, conv_w, conv_b, w_dt, b_dt)


def _split3(q):
    hi = q.astype(BF16)
    r1 = q - hi.astype(F32)
    mid = r1.astype(BF16)
    lo = (r1 - mid.astype(F32)).astype(BF16)
    return hi, mid, lo


def _spread(q, r_ref):
    hi, mid, lo = _split3(q[:, :SSM_HEADS])
    return jnp.dot(jnp.concatenate([hi, mid, lo], axis=1), r_ref[...], preferred_element_type=F32)


def _spread_consts():
    r = jnp.arange(3 * SSM_HEADS) % SSM_HEADS
    r1 = (jnp.arange(SSM_HEADS * SSM_P)[None, :] // SSM_P == r[:, None]).astype(BF16)
    r2 = (jnp.arange(SSM_HEADS * CHUNK)[None, :] // CHUNK == r[:, None]).astype(BF16)
    return r1, r2


def _ssd_chain(xbc, dt_all, a_ref, r1_ref, r2_ref, hs_ref, bwd):
    L = CHUNK
    d = 1 if bwd else 0
    xs = xbc[:, :SSM_INNER].astype(F32)
    bcb = xbc[:, SSM_INNER:]

    dt = pltpu.roll(dt_all, LANES - SSM_HEADS, axis=1) if bwd else dt_all
    a_row = a_ref[d:d + 1, :]
    ii = lax.broadcasted_iota(jnp.int32, (L, L), 0)
    jj = lax.broadcasted_iota(jnp.int32, (L, L), 1)
    tri = (jj >= ii) if bwd else (ii >= jj)
    tri_b = jnp.where(tri, 1.0, 0.0).astype(BF16)
    neg_mask = jnp.where(tri, 0.0, -jnp.inf)
    hi_p, mid_p, lo_p = _split3(dt * a_row)
    acum = (jnp.dot(tri_b, hi_p, preferred_element_type=F32)
            + jnp.dot(tri_b, mid_p, preferred_element_type=F32)
            + jnp.dot(tri_b, lo_p, preferred_element_type=F32))
    end = 0 if bwd else L - 1
    total = acum[end:end + 1, :]
    acum2 = acum * LOG2E
    acum_t = acum2.T
    cdec_t = jnp.exp2(acum_t[:, end:end + 1])

    e_acum = _spread(acum2, r2_ref)
    e_dt = _spread(dt, r1_ref)
    e_end = _spread(dt * jnp.exp(total - acum), r1_ref)
    e_in = _spread(jnp.exp(acum), r1_ref)
    xdt = (xs * e_dt).astype(BF16)
    xw = (xs * e_end).astype(BF16)
    left = lax.broadcasted_iota(jnp.int32, (L, LANES), 1) < SSM_P

    y_parts = []
    for g in range(SSM_GROUPS):
        bm = bcb[:, g * SSM_STATE:(g + 1) * SSM_STATE]
        cm = bcb[:, SSM_GN + g * SSM_STATE:SSM_GN + (g + 1) * SSM_STATE]
        cb = lax.dot_general(cm, bm, (((1,), (1,)), ((), ())), preferred_element_type=F32)
        gsl = slice(g * SSM_HG * SSM_P, (g + 1) * SSM_HG * SSM_P)
        hprev = hs_ref[d, gsl, :]
        y_off = lax.dot_general(cm, hprev.astype(BF16), (((1,), (1,)), ((), ())),
                                preferred_element_type=F32)
        y_pairs = []
        for pr in range(SSM_HG // 2):
            wm = []
            for h in (g * SSM_HG + 2 * pr, g * SSM_HG + 2 * pr + 1):
                seg = e_acum[:, h * L:(h + 1) * L] - acum_t[h:h + 1, :] + neg_mask
                wm.append((cb * jnp.exp2(seg)).astype(BF16))
            pair = xdt[:, (g * SSM_HG + 2 * pr) * SSM_P:(g * SSM_HG + 2 * pr + 2) * SSM_P]
            rhs = jnp.concatenate([jnp.where(left, pair, jnp.zeros_like(pair)),
                                   jnp.where(left, jnp.zeros_like(pair), pair)], axis=0)
            y_pairs.append(jnp.dot(jnp.concatenate(wm, axis=1), rhs, preferred_element_type=F32))
        y_parts.append(jnp.concatenate(y_pairs, axis=1) + y_off * e_in[:, gsl])
        st = lax.dot_general(xw[:, gsl], bm, (((0,), (0,)), ((), ())), preferred_element_type=F32)
        dec = jnp.concatenate([jnp.broadcast_to(cdec_t[g * SSM_HG + k:g * SSM_HG + k + 1, :], (SSM_P, 1))
                               for k in range(SSM_HG)], axis=0)
        hs_ref[d, gsl, :] = dec * hprev + st
    return jnp.concatenate(y_parts, axis=1), xs


def _ssd_kernel(*refs, nc, has_h0, emit_state):
    (xa_ref, xb_ref, dta_ref, dtb_ref, za_ref, zb_ref, a_ref, dsk_ref, ng_ref, r1_ref, r2_ref,
     wo_ref, xres_ref, gt_ref, g3_ref) = refs[:15]
    pos = 15
    h0_ref = None
    if has_h0:
        h0_ref = refs[pos]
        pos += 1
    y_ref = refs[pos]
    pos += 1
    st_ref = None
    if emit_state:
        st_ref = refs[pos]
        pos += 1
    hs_ref, yacc_ref = refs[pos:pos + 2]

    s = pl.program_id(1)
    L = CHUNK
    half = nc // 2
    off_a = pl.multiple_of(s * L, L)
    off_b = pl.multiple_of((nc - 1 - s) * L, L)

    @pl.when(s == 0)
    def _():
        hs_ref[...] = h0_ref[...] if has_h0 else jnp.zeros_like(hs_ref)

    ya, xs_a = _ssd_chain(xa_ref[...], dta_ref[...], a_ref, r1_ref, r2_ref, hs_ref, bwd=False)
    ya = ya + dsk_ref[...] * xs_a
    yb, _ = _ssd_chain(xb_ref[...], dtb_ref[...], a_ref, r1_ref, r2_ref, hs_ref, bwd=True)

    @pl.when(s < half)
    def _():
        yacc_ref[pl.ds(off_a, L), :] = ya
        yacc_ref[pl.ds(off_b, L), :] = yb

    @pl.when(s >= half)
    def _():
        for off, y_new, z_ref in ((off_a, ya, za_ref), (off_b, yb, zb_ref)):
            yt = (yacc_ref[pl.ds(off, L), :] + y_new) * _silu(z_ref[...].astype(F32))
            yn = _rms(yt, ng_ref[...]).astype(BF16)
            m = jnp.dot(yn, wo_ref[...], preferred_element_type=F32)
            y_ref[pl.ds(off, L), :] = xres_ref[pl.ds(off, L), :] + gt_ref[...] * _rms(m, g3_ref[...])

    if emit_state:
        @pl.when(s == nc - 1)
        def _():
            st_ref[...] = hs_ref[...]


def _ssd(z, xbc, dt, a_pad, dsk, norm_g, r1, r2, w_out, x, mods, gains, layer, h0, *,
         seq0, n_seq, seq_len, emit_state):
    nc = seq_len // CHUNK
    half = nc // 2
    chunk0 = seq0 // CHUNK
    seq_blk0 = seq0 // seq_len
    mod_base = layer * MOD_ROWS * N_MOD + 5
    mod_row0 = 0 if seq0 < CTX_TOK else 1
    mod_step = 0 if seq0 < CTX_TOK else N_MOD

    def fwd_chunk(b, s):
        return chunk0 + b * nc + s

    def bwd_chunk(b, s):
        return chunk0 + b * nc + nc - 1 - s

    def late(s):
        return jnp.maximum(s, half)

    const = lambda b, s: (0, 0)
    in_specs = [
        pl.BlockSpec((CHUNK, SSM_CONV_DIM), lambda b, s: (fwd_chunk(b, s), 0)),
        pl.BlockSpec((CHUNK, SSM_CONV_DIM), lambda b, s: (bwd_chunk(b, s), 0)),
        pl.BlockSpec((CHUNK, LANES), lambda b, s: (fwd_chunk(b, s), 0)),
        pl.BlockSpec((CHUNK, LANES), lambda b, s: (bwd_chunk(b, s), 0)),
        pl.BlockSpec((CHUNK, SSM_INNER), lambda b, s: (fwd_chunk(b, late(s)), 0)),
        pl.BlockSpec((CHUNK, SSM_INNER), lambda b, s: (bwd_chunk(b, late(s)), 0)),
        pl.BlockSpec((2, LANES), const),
        pl.BlockSpec((1, SSM_INNER), const),
        pl.BlockSpec((1, SSM_INNER), const),
        pl.BlockSpec(r1.shape, const),
        pl.BlockSpec(r2.shape, const),
        pl.BlockSpec((SSM_INNER, D), const, pipeline_mode=pl.Buffered(1)),
        pl.BlockSpec((seq_len, D), lambda b, s: (seq_blk0 + b, 0)),
        pl.BlockSpec((None, 1, D), lambda b, s: (mod_base + mod_row0 * N_MOD + b * mod_step, 0, 0)),
        pl.BlockSpec((None, 1, D), lambda b, s: (layer * 6 + 3, 0, 0)),
    ]
    args = [xbc, xbc, dt, dt, z, z, a_pad, dsk, norm_g, r1, r2, w_out, x, mods, gains]
    if h0 is not None:
        in_specs.append(pl.BlockSpec((None, 2, SSM_INNER, SSM_STATE), lambda b, s: (b, 0, 0, 0)))
        args.append(h0)
    out_specs = [pl.BlockSpec((seq_len, D), lambda b, s: (b, 0))]
    out_shape = [jax.ShapeDtypeStruct((n_seq * seq_len, D), F32)]
    if emit_state:
        out_specs.append(pl.BlockSpec((None, 2, SSM_INNER, SSM_STATE), lambda b, s: (b, 0, 0, 0)))
        out_shape.append(jax.ShapeDtypeStruct((n_seq, 2, SSM_INNER, SSM_STATE), F32))
    return pl.pallas_call(
        functools.partial(_ssd_kernel, nc=nc, has_h0=h0 is not None, emit_state=emit_state),
        grid=(n_seq, nc),
        in_specs=in_specs,
        out_specs=out_specs,
        out_shape=out_shape,
        scratch_shapes=[pltpu.VMEM((2, SSM_INNER, SSM_STATE), F32), pltpu.VMEM((seq_len, SSM_INNER), F32)],
        compiler_params=_cparams(("arbitrary", "arbitrary")),
        name=f"ssd_{seq_len}",
    )(*args)


def kernel(x_prompt, x_sample, cache_k, cache_v, state_ssm, c, c_ctx, w_mod, b_mod, norm_g, ffn_in, ffn_out,
           gmlp_in, gmlp_ln_g, gmlp_ln_b, gmlp_ws, gmlp_bs, gmlp_out, attn_qkv, attn_sink, attn_out,
           ssm_in, ssm_conv_w, ssm_conv_b, ssm_dt_bias, ssm_a_log, ssm_d, ssm_norm, ssm_out):
    cond = jnp.concatenate([c_ctx[None, :], c, jnp.zeros((MOD_ROWS - 1 - N_LAT_SEQ, D), F32)], axis=0)
    mods = _modulation(cond, w_mod, b_mod).reshape(DEPTH * MOD_ROWS * N_MOD, 1, D)
    gains = norm_g.reshape(DEPTH * 6, 1, D)

    x = (x_prompt.reshape(CTX_TOK, D), x_sample.reshape(LAT_TOK, D))
    new_k = new_v = new_s = None
    for layer in range(DEPTH):
        x = _ffn(x, mods, gains, ffn_in, ffn_out, layer, 0, split_in=layer == 0)
        kind, j = layer % 3, layer // 3
        if kind == 0:
            x = _gmlp(x, mods, gains, gmlp_in, gmlp_ln_g[:, None, :], gmlp_ln_b[:, None, :], gmlp_ws,
                      gmlp_bs[..., None], gmlp_out, j, layer)
        elif kind == 1:
            qkv, kc, vc = _qkv_proj(x, mods, gains, attn_qkv, j, layer)
            new_k = kc.reshape(N_CTX_SEQ, CTX_LEN, N_KV_HEADS, HEAD_DIM)
            new_v = vc.reshape(N_CTX_SEQ, CTX_LEN, N_KV_HEADS, HEAD_DIM)
            o_ctx, o_lat = _attention(qkv, attn_sink[j],
                                      cache_k[:, j].reshape(N_LAT_SEQ, CTX_LEN, KV_DIM),
                                      cache_v[:, j].reshape(N_LAT_SEQ, CTX_LEN, KV_DIM))
            x = _out_proj(o_ctx, o_lat, attn_out, j, x, mods, gains, layer)
        else:
            pad = LANES - 2 * SSM_HEADS
            w_dt = jnp.pad(ssm_in[j][:, SSM_MAIN:], ((0, 0), (0, pad)))
            b_dt = jnp.pad(ssm_dt_bias[j].reshape(1, 2 * SSM_HEADS), ((0, 0), (0, pad)))
            z, xbc, dt = _ssm_in(x, mods, gains, ssm_in, j, ssm_conv_w[j], ssm_conv_b[j][None, :],
                                 w_dt, b_dt, layer)
            a_pad = jnp.pad(-jnp.exp(ssm_a_log[j]), ((0, 0), (0, LANES - SSM_HEADS)))
            dsk = jnp.repeat(ssm_d[j], SSM_P)[None, :]
            ng = ssm_norm[j][None, :]
            r1, r2 = _spread_consts()
            w_o = ssm_out[j].astype(BF16)
            x_ctx, st = _ssd(z, xbc, dt, a_pad, dsk, ng, r1, r2, w_o, x, mods, gains, layer, None,
                             seq0=0, n_seq=N_CTX_SEQ, seq_len=CTX_LEN, emit_state=True)
            (x_lat,) = _ssd(z, xbc, dt, a_pad, dsk, ng, r1, r2, w_o, x, mods, gains, layer,
                            state_ssm[:, j].reshape(N_LAT_SEQ, 2, SSM_INNER, SSM_STATE),
                            seq0=CTX_TOK, n_seq=N_LAT_SEQ, seq_len=LAT_LEN, emit_state=False)
            new_s = st.reshape(N_CTX_SEQ, 2, SSM_HEADS, SSM_P, SSM_STATE)
            x = (x_ctx, x_lat)
        x = _ffn(x, mods, gains, ffn_in, ffn_out, layer, 1, split_in=isinstance(x, tuple),
                 split_out=layer == DEPTH - 1)

    y_prompt = x[0].reshape(N_CTX_SEQ, CTX_LEN, D)
    y_sample = x[1].reshape(N_LAT_SEQ, LAT_LEN, D)
    return (y_prompt, y_sample, new_k[:, None], new_v[:, None], new_s[:, None])
```

```python
import functools
import math

import jax
import jax.numpy as jnp
from jax import lax
from jax.experimental import pallas as pl
from jax.experimental.pallas import tpu as pltpu

F32 = jnp.float32
BF16 = jnp.bfloat16

D = 1024
N_CTX_SEQ, CTX_LEN = 16, 256
N_LAT_SEQ, LAT_LEN = 4, 1024
CTX_TOK = N_CTX_SEQ * CTX_LEN
LAT_TOK = N_LAT_SEQ * LAT_LEN
TOK = CTX_TOK + LAT_TOK
DEPTH = 4
N_MOD = 9
MOD_ROWS = 8
D_FF = 2816
EPS = 1e-6
GRID_W = 64
GMLP_HALF = 3 * D
GMLP_GROUPS = 8
GMLP_GD = GMLP_HALF // GMLP_GROUPS
CHUNK = 128
HEAD_DIM = 64
N_Q_HEADS = 16
N_KV_HEADS = 4
Q_PER_KV = 4
Q_DIM = N_Q_HEADS * HEAD_DIM
KV_DIM = N_KV_HEADS * HEAD_DIM
QKV_DIM = Q_DIM + 2 * KV_DIM
ATTN_SCALE = HEAD_DIM ** -0.5
ROPE_BASE = 10000.0
ROT_PAIRS = HEAD_DIM // 4
NEG_INF = -1e30
SSM_INNER = 2 * D
SSM_HEADS = 32
SSM_P = 64
SSM_GROUPS = 4
SSM_HG = SSM_HEADS // SSM_GROUPS
SSM_STATE = 128
SSM_GN = SSM_GROUPS * SSM_STATE
SSM_CONV_DIM = SSM_INNER + 2 * SSM_GN
SSM_MAIN = SSM_INNER + SSM_CONV_DIM
LANES = 128
LOG2E = 1.0 / math.log(2.0)

VMEM_LIMIT = 56 * 1024 * 1024
VMEM_LIMIT_BIG = 60 * 1024 * 1024


def _cparams(sem, limit=VMEM_LIMIT):
    return pltpu.CompilerParams(dimension_semantics=sem, vmem_limit_bytes=limit)


def _sigmoid(x):
    return 1.0 / (1.0 + jnp.exp(-x))


def _silu(x):
    return x * _sigmoid(x)


def _rms(x, g):
    return x * lax.rsqrt(jnp.mean(x * x, axis=-1, keepdims=True) + EPS) * g


def _adaln(x, g, shift, scale):
    return _rms(x, g) * (1.0 + scale) + shift


def _mod_row(i, tm):
    t0 = i * tm
    return jnp.where(t0 < CTX_TOK, 0, 1 + (t0 - CTX_TOK) // LAT_LEN)


def _mod_spec(layer, k, tm, grid_rank):
    base = layer * MOD_ROWS * N_MOD + k
    if grid_rank == 1:
        return pl.BlockSpec((None, 1, D), lambda i: (base + _mod_row(i, tm) * N_MOD, 0, 0))
    return pl.BlockSpec((None, 1, D), lambda i, j: (base + _mod_row(i, tm) * N_MOD, 0, 0))


def _gain_spec(layer, k, grid_rank):
    idx = layer * 6 + k
    if grid_rank == 1:
        return pl.BlockSpec((None, 1, D), lambda i: (idx, 0, 0))
    return pl.BlockSpec((None, 1, D), lambda i, j: (idx, 0, 0))


def _split_specs(tm, width, grid_rank):
    na = CTX_TOK // tm
    if grid_rank == 1:
        return (pl.BlockSpec((tm, width), lambda i: (jnp.minimum(i, na - 1), 0)),
                pl.BlockSpec((tm, width), lambda i: (jnp.maximum(i - na, 0), 0)))
    return (pl.BlockSpec((tm, width), lambda i, j: (jnp.minimum(i, na - 1), 0)),
            pl.BlockSpec((tm, width), lambda i, j: (jnp.maximum(i - na, 0), 0)))


def _mod_kernel(cond_ref, w_ref, b_ref, o_ref):
    s = _silu(cond_ref[...]).astype(BF16)
    o_ref[...] = jnp.dot(s, w_ref[...].astype(BF16), preferred_element_type=F32) + b_ref[...]


def _modulation(cond, w_mod, b_mod):
    tn = 2304
    n = N_MOD * D
    return pl.pallas_call(
        _mod_kernel,
        grid=(DEPTH, n // tn),
        in_specs=[
            pl.BlockSpec((MOD_ROWS, D), lambda l, j: (0, 0)),
            pl.BlockSpec((None, D, tn), lambda l, j: (l, 0, j)),
            pl.BlockSpec((None, 1, tn), lambda l, j: (l, 0, j)),
        ],
        out_specs=pl.BlockSpec((None, MOD_ROWS, tn), lambda l, j: (l, 0, j)),
        out_shape=jax.ShapeDtypeStruct((DEPTH, MOD_ROWS, n), F32),
        compiler_params=_cparams(("arbitrary", "arbitrary")),
        name="modulation",
    )(cond, w_mod, b_mod.reshape(DEPTH, 1, n))


FFN_TF = 256
FFN_SLABS = D_FF // FFN_TF
FFN_SPS = 4
FFN_STEP_SLABS = (3, 4, 4)
FFN_STEPS = len(FFN_STEP_SLABS)
assert sum(FFN_STEP_SLABS) == FFN_SLABS and max(FFN_STEP_SLABS) == FFN_SPS


def _ffn_kernel(*refs, tm, split_in, split_out):
    n_x = 2 if split_in else 1
    n_o = 2 if split_out else 1
    x_refs = refs[:n_x]
    sh_ref, sc_ref, gt_ref, g0_ref, g1_ref = refs[n_x:n_x + 5]
    w0 = n_x + 5
    wg = refs[w0:w0 + FFN_SPS]
    wu = refs[w0 + FFN_SPS:w0 + 2 * FFN_SPS]
    wo = refs[w0 + 2 * FFN_SPS:w0 + 3 * FFN_SPS]
    o_refs = refs[w0 + 3 * FFN_SPS:w0 + 3 * FFN_SPS + n_o]
    h_ref, acc_ref = refs[w0 + 3 * FFN_SPS + n_o:]
    i = pl.program_id(0)
    j = pl.program_id(1)
    is_ctx = i * tm < CTX_TOK
    last = FFN_STEPS - 1

    def load_x():
        if split_in:
            return jnp.where(is_ctx, x_refs[0][...], x_refs[1][...])
        return x_refs[0][...]

    def swiglu_slabs(h, n):
        acts = []
        for k in range(n):
            g = jnp.dot(h, wg[k][...].astype(BF16), preferred_element_type=F32)
            u = jnp.dot(h, wu[k][...].astype(BF16), preferred_element_type=F32)
            acts.append((_silu(g) * u).astype(BF16))
        w = jnp.concatenate([wo[k][...].astype(BF16) for k in range(n)], axis=0)
        return jnp.dot(jnp.concatenate(acts, axis=1), w, preferred_element_type=F32)

    @pl.when(j == 0)
    def _():
        h = _adaln(load_x(), g0_ref[...], sh_ref[...], sc_ref[...]).astype(BF16)
        h_ref[...] = h
        acc_ref[...] = swiglu_slabs(h, FFN_STEP_SLABS[0])

    for step in range(1, last):
        @pl.when(j == step)
        def _(step=step):
            acc_ref[...] += swiglu_slabs(h_ref[...], FFN_STEP_SLABS[step])

    @pl.when(j == last)
    def _():
        f = acc_ref[...] + swiglu_slabs(h_ref[...], FFN_STEP_SLABS[last])
        res = load_x() + (0.5 * gt_ref[...]) * _rms(f, g1_ref[...])
        if split_out:
            @pl.when(is_ctx)
            def _():
                o_refs[0][...] = res

            @pl.when(jnp.logical_not(is_ctx))
            def _():
                o_refs[1][...] = res
        else:
            o_refs[0][...] = res


def _ffn(x, mods, gains, ffn_in, ffn_out, layer, which, split_in=False, split_out=False):
    tm, tf = 1024, FFN_TF
    k0 = 0 if which == 0 else 6
    gi = 0 if which == 0 else 4
    if split_in:
        x_specs = list(_split_specs(tm, D, 2))
    else:
        x_specs = [pl.BlockSpec((tm, D), lambda i, j: (i, 0))]
    x_args = list(x) if split_in else [x]
    if split_out:
        out_specs = list(_split_specs(tm, D, 2))
        out_shape = [jax.ShapeDtypeStruct((CTX_TOK, D), F32), jax.ShapeDtypeStruct((LAT_TOK, D), F32)]
    else:
        out_specs = pl.BlockSpec((tm, D), lambda i, j: (i, 0))
        out_shape = jax.ShapeDtypeStruct((TOK, D), F32)

    starts = [sum(FFN_STEP_SLABS[:s]) for s in range(FFN_STEPS)]
    table = []
    for k in range(FFN_SPS):
        col = [starts[s] + k if k < FFN_STEP_SLABS[s] else None for s in range(FFN_STEPS)]
        for s in reversed(range(FFN_STEPS)):
            if col[s] is None:
                col[s] = col[s + 1] if s + 1 < FFN_STEPS else col[s - 1]
        table.append(col)

    def slab(j, k):
        idx = table[k][FFN_STEPS - 1]
        for s in reversed(range(FFN_STEPS - 1)):
            idx = jnp.where(j == s, table[k][s], idx)
        return idx

    def spec_in(k, col0):
        return pl.BlockSpec((None, None, D, tf), lambda i, j: (layer, which, 0, col0 + slab(j, k)))

    def spec_out(k):
        return pl.BlockSpec((None, None, tf, D), lambda i, j: (layer, which, slab(j, k), 0))

    w_specs = ([spec_in(k, 0) for k in range(FFN_SPS)] + [spec_in(k, FFN_SLABS) for k in range(FFN_SPS)]
               + [spec_out(k) for k in range(FFN_SPS)])
    return pl.pallas_call(
        functools.partial(_ffn_kernel, tm=tm, split_in=split_in, split_out=split_out),
        grid=(TOK // tm, FFN_STEPS),
        in_specs=x_specs + [
            _mod_spec(layer, k0 + 0, tm, 2),
            _mod_spec(layer, k0 + 1, tm, 2),
            _mod_spec(layer, k0 + 2, tm, 2),
            _gain_spec(layer, gi, 2),
            _gain_spec(layer, gi + 1, 2),
        ] + w_specs,
        out_specs=out_specs,
        out_shape=out_shape,
        scratch_shapes=[pltpu.VMEM((tm, D), BF16), pltpu.VMEM((tm, D), F32)],
        compiler_params=_cparams(("arbitrary", "arbitrary"), VMEM_LIMIT_BIG),
        name=f"ffn_l{layer}_{which}",
    )(*x_args, mods, mods, mods, gains, gains, *([ffn_in] * (2 * FFN_SPS)), *([ffn_out] * FFN_SPS))


def _out_kernel(aa_ref, ab_ref, w_ref, x_ref, gt_ref, g_ref, o_ref, *, tm):
    is_ctx = pl.program_id(0) * tm < CTX_TOK
    wb = w_ref[...].astype(BF16)
    for r in range(2):
        rows = slice(r * (tm // 2), (r + 1) * (tm // 2))
        a = jnp.where(is_ctx, aa_ref[rows, :], ab_ref[rows, :])
        m = jnp.dot(a, wb, preferred_element_type=F32)
        o_ref[rows, :] = x_ref[rows, :] + gt_ref[...] * _rms(m, g_ref[...])


def _out_proj(a_ctx, a_lat, w, widx, x, mods, gains, layer, tm=1024):
    k = a_ctx.shape[1]
    sa, sb = _split_specs(tm, k, 1)
    return pl.pallas_call(
        functools.partial(_out_kernel, tm=tm),
        grid=(TOK // tm,),
        in_specs=[
            sa, sb,
            pl.BlockSpec((None, k, D), lambda i: (widx, 0, 0)),
            pl.BlockSpec((tm, D), lambda i: (i, 0)),
            _mod_spec(layer, 5, tm, 1),
            _gain_spec(layer, 3, 1),
        ],
        out_specs=pl.BlockSpec((tm, D), lambda i: (i, 0)),
        out_shape=jax.ShapeDtypeStruct((TOK, D), F32),
        compiler_params=_cparams(("arbitrary",)),
        name=f"outproj_l{layer}",
    )(a_ctx, a_lat, w, x, mods, gains)


GMLP_GPS = 2
GMLP_SLABS = GMLP_GROUPS // GMLP_GPS
GMLP_COLS = GMLP_GPS * GMLP_GD


def _gelu_exact(x):
    return 0.5 * x * (1.0 + lax.erf(x * (1.0 / math.sqrt(2.0))))


def _fold_lanes(v):
    acc = v[:, :LANES]
    for cb in range(1, v.shape[1] // LANES):
        acc = acc + v[:, cb * LANES:(cb + 1) * LANES]
    return acc


GMLP_P1 = GMLP_SLABS // 2


def _gmlp_kernel(x_ref, sh_ref, sc_ref, g2_ref, wina_ref, winb_ref, lg_ref, lb_ref, ws_ref, bs_ref, wo_ref,
                 gt_ref, g3_ref, o_ref, h_ref, v_ref, c_ref, s1_ref, s2_ref, mu_ref, rstd_ref, *, tm):
    j = pl.program_id(1)
    last = GMLP_P1 + GMLP_SLABS - 1

    def slab(h, w_ref=wina_ref):
        return _gelu_exact(jnp.dot(h, w_ref[...].astype(BF16), preferred_element_type=F32))

    def park(y, idx):
        if idx == 0:
            c_ref[...] = jnp.sum(_fold_lanes(y), axis=-1, keepdims=True) * (1.0 / GMLP_COLS)
        dv = y - c_ref[...]
        if idx == 0:
            s1_ref[...] = _fold_lanes(dv)
            s2_ref[...] = _fold_lanes(dv * dv)
        else:
            s1_ref[...] += _fold_lanes(dv)
            s2_ref[...] += _fold_lanes(dv * dv)
        v_ref[idx] = y

    def mix(u, first):
        s = j - GMLP_P1
        vn = ((v_ref[s] - mu_ref[...]) * rstd_ref[...] * lg_ref[...] + lb_ref[...]).astype(BF16)
        rows = []
        for c in range(tm // CHUNK):
            cols = []
            for gg in range(GMLP_GPS):
                vg = vn[c * CHUNK:(c + 1) * CHUNK, gg * GMLP_GD:(gg + 1) * GMLP_GD]
                cols.append(jnp.dot(ws_ref[gg].astype(BF16), vg, preferred_element_type=F32) + bs_ref[gg])
            rows.append(jnp.concatenate(cols, axis=1))
        a = (u * jnp.concatenate(rows, axis=0)).astype(BF16)
        t = jnp.dot(a, wo_ref[...].astype(BF16), preferred_element_type=F32)
        return t if first else o_ref[...] + t

    @pl.when(j == 0)
    def _():
        h = _adaln(x_ref[...], g2_ref[...], sh_ref[...], sc_ref[...]).astype(BF16)
        h_ref[...] = h
        park(slab(h), 0)
        park(slab(h, winb_ref), 1)

    for step in range(1, GMLP_P1):
        @pl.when(j == step)
        def _(step=step):
            park(slab(h_ref[...]), 2 * step)
            park(slab(h_ref[...], winb_ref), 2 * step + 1)

    @pl.when(j == GMLP_P1)
    def _():
        d1 = jnp.sum(s1_ref[...], axis=-1, keepdims=True) * (1.0 / GMLP_HALF)
        d2 = jnp.sum(s2_ref[...], axis=-1, keepdims=True) * (1.0 / GMLP_HALF)
        mu_ref[...] = c_ref[...] + d1
        rstd_ref[...] = lax.rsqrt(d2 - d1 * d1 + EPS)
        o_ref[...] = mix(slab(h_ref[...]), True)

    @pl.when((j > GMLP_P1) & (j < last))
    def _():
        o_ref[...] = mix(slab(h_ref[...]), False)

    @pl.when(j == last)
    def _():
        m = mix(slab(h_ref[...]), False)
        o_ref[...] = x_ref[...] + gt_ref[...] * _rms(m, g3_ref[...])


def _gmlp(x, mods, gains, w_in, ln_g, ln_b, w_s, b_s, w_out, widx, layer, tm=1024):
    ns, p1 = GMLP_SLABS, GMLP_P1

    def u_slab(j):
        return jnp.maximum(j - p1, 0)

    def win_a(j):
        return jnp.where(j < p1, ns + 2 * j, j - p1)

    def win_b(j):
        return jnp.where(j < p1, ns + 2 * j + 1, ns + 1)

    return pl.pallas_call(
        functools.partial(_gmlp_kernel, tm=tm),
        grid=(TOK // tm, p1 + ns),
        in_specs=[
            pl.BlockSpec((tm, D), lambda i, j: (i, 0)),
            _mod_spec(layer, 3, tm, 2),
            _mod_spec(layer, 4, tm, 2),
            _gain_spec(layer, 2, 2),
            pl.BlockSpec((None, D, GMLP_COLS), lambda i, j: (widx, 0, win_a(j))),
            pl.BlockSpec((None, D, GMLP_COLS), lambda i, j: (widx, 0, win_b(j))),
            pl.BlockSpec((None, 1, GMLP_COLS), lambda i, j: (widx, 0, u_slab(j))),
            pl.BlockSpec((None, 1, GMLP_COLS), lambda i, j: (widx, 0, u_slab(j))),
            pl.BlockSpec((None, GMLP_GPS, CHUNK, CHUNK), lambda i, j: (widx, u_slab(j), 0, 0)),
            pl.BlockSpec((None, GMLP_GPS, CHUNK, 1), lambda i, j: (widx, u_slab(j), 0, 0)),
            pl.BlockSpec((None, GMLP_COLS, D), lambda i, j: (widx, u_slab(j), 0)),
            _mod_spec(layer, 5, tm, 2),
            _gain_spec(layer, 3, 2),
        ],
        out_specs=pl.BlockSpec((tm, D), lambda i, j: (i, 0)),
        out_shape=jax.ShapeDtypeStruct((TOK, D), F32),
        scratch_shapes=[pltpu.VMEM((tm, D), BF16), pltpu.VMEM((ns, tm, GMLP_COLS), F32),
                        pltpu.VMEM((tm, 1), F32), pltpu.VMEM((tm, LANES), F32), pltpu.VMEM((tm, LANES), F32),
                        pltpu.VMEM((tm, 1), F32), pltpu.VMEM((tm, 1), F32)],
        compiler_params=_cparams(("arbitrary", "arbitrary"), VMEM_LIMIT_BIG),
        name=f"gmlp_l{layer}",
    )(x, mods, mods, gains, w_in, w_in, ln_g, ln_b, w_s, b_s, w_out, mods, gains)


def _rope_tables():
    pos = jnp.arange(LAT_LEN)
    pos_r = (pos // GRID_W).astype(F32)
    pos_c = (pos % GRID_W).astype(F32)
    inv = ROPE_BASE ** (-jnp.arange(ROT_PAIRS, dtype=F32) / ROT_PAIRS)
    ang_r = pos_r[:, None] * inv
    ang_c = pos_c[:, None] * inv
    cos = jnp.concatenate([jnp.cos(ang_r)] * 2 + [jnp.cos(ang_c)] * 2, axis=1)
    sin = jnp.concatenate([-jnp.sin(ang_r), jnp.sin(ang_r), -jnp.sin(ang_c), jnp.sin(ang_c)], axis=1)
    reps = LANES // HEAD_DIM
    return jnp.tile(cos, (1, reps)), jnp.tile(sin, (1, reps))


def _qkv_kernel(x_ref, sh_ref, sc_ref, g_ref, w_ref, cos_ref, sin_ref, o_ref, kc_ref, vc_ref, *, tm):
    i = pl.program_id(0)
    hr = tm // 2

    def proj(rows, wb):
        h = _adaln(x_ref[rows, :], g_ref[...], sh_ref[...], sc_ref[...])
        return jnp.dot(h.astype(BF16), wb, preferred_element_type=F32)

    @pl.when(i * tm < CTX_TOK)
    def _():
        wb = w_ref[...].astype(BF16)
        for r in range(2):
            rows = slice(r * hr, (r + 1) * hr)
            y = proj(rows, wb)
            o_ref[rows, :] = y
            for sq in range(hr // CTX_LEN):
                blk = y[sq * CTX_LEN:(sq + 1) * CTX_LEN, :]
                b = r * (hr // CTX_LEN) + sq
                kc_ref[b] = blk[:, Q_DIM:Q_DIM + KV_DIM].T
                vc_ref[b] = blk[:, Q_DIM + KV_DIM:].T

    @pl.when(i * tm >= CTX_TOK)
    def _():
        wb = w_ref[...].astype(BF16)
        lane = lax.broadcasted_iota(jnp.int32, (hr, LANES), 1)
        first = (lane % (2 * ROT_PAIRS)) < ROT_PAIRS
        for r in range(2):
            rows = slice(r * hr, (r + 1) * hr)
            y = proj(rows, wb)
            cos = cos_ref[rows, :]
            sin = sin_ref[rows, :]
            for cb in range((Q_DIM + KV_DIM) // LANES):
                t = y[:, cb * LANES:(cb + 1) * LANES]
                partner = jnp.where(first, pltpu.roll(t, LANES - ROT_PAIRS, axis=1),
                                    pltpu.roll(t, ROT_PAIRS, axis=1))
                o_ref[rows, cb * LANES:(cb + 1) * LANES] = t * cos + partner * sin
            o_ref[rows, Q_DIM + KV_DIM:] = y[:, Q_DIM + KV_DIM:]


def _qkv_proj(x, mods, gains, w, widx, layer, tm=1024):
    cos, sin = _rope_tables()
    per_seq = LAT_LEN // tm
    tab = pl.BlockSpec((tm, LANES), lambda i: (jnp.maximum(i - CTX_TOK // tm, 0) % per_seq, 0))
    seq_per_tile = tm // CTX_LEN
    kv_spec = pl.BlockSpec((seq_per_tile, KV_DIM, CTX_LEN),
                           lambda i: (jnp.minimum(i, CTX_TOK // tm - 1), 0, 0))
    return pl.pallas_call(
        functools.partial(_qkv_kernel, tm=tm),
        grid=(TOK // tm,),
        in_specs=[
            pl.BlockSpec((tm, D), lambda i: (i, 0)),
            _mod_spec(layer, 3, tm, 1),
            _mod_spec(layer, 4, tm, 1),
            _gain_spec(layer, 2, 1),
            pl.BlockSpec((None, D, QKV_DIM), lambda i: (widx, 0, 0)),
            tab, tab,
        ],
        out_specs=[pl.BlockSpec((tm, QKV_DIM), lambda i: (i, 0)), kv_spec, kv_spec],
        out_shape=[jax.ShapeDtypeStruct((TOK, QKV_DIM), F32),
                   jax.ShapeDtypeStruct((N_CTX_SEQ, KV_DIM, CTX_LEN), F32),
                   jax.ShapeDtypeStruct((N_CTX_SEQ, KV_DIM, CTX_LEN), F32)],
        compiler_params=_cparams(("arbitrary",)),
        name=f"qkv_l{layer}",
    )(x, mods, mods, gains, w, cos, sin)


def _attend(q4, keys, vals, sink_rep, masks, merge_sums):
    logits = []
    for k, mk in zip(keys, masks):
        s = lax.dot_general(q4, k, (((1,), (1,)), ((), ())), preferred_element_type=F32) * ATTN_SCALE
        if mk is not None:
            s = jnp.where(mk, s, NEG_INF)
        logits.append([s[:, cb * LANES:(cb + 1) * LANES] for cb in range(s.shape[1] // LANES)])
    mx = None
    for blocks in logits:
        for blk in blocks:
            mx = blk if mx is None else jnp.maximum(mx, blk)
    m = jnp.maximum(sink_rep, jnp.max(mx, axis=-1, keepdims=True))
    acc = rs = None
    for blocks, v in zip(logits, vals):
        p = jnp.concatenate([jnp.exp(blk - m) for blk in blocks], axis=1).astype(BF16)
        if merge_sums:
            v = jnp.concatenate([v, jnp.ones_like(v)], axis=1)
        else:
            ps = jnp.dot(p, jnp.ones((p.shape[1], LANES), BF16), preferred_element_type=F32)
            rs = ps if rs is None else rs + ps
        pv = jnp.dot(p, v, preferred_element_type=F32)
        acc = pv if acc is None else acc + pv
    if merge_sums:
        denom = pltpu.roll(acc, HEAD_DIM, axis=1) + jnp.exp(sink_rep - m)
        return (acc / denom)[:, :HEAD_DIM]
    return acc / (rs + jnp.exp(sink_rep - m))[:, :HEAD_DIM]


def _attn_ctx_kernel(sink_ref, q_ref, k_ref, v_ref, o_ref):
    rows = CTX_LEN
    outs = [None] * N_Q_HEADS
    for hk in range(N_KV_HEADS):
        k = k_ref[:, hk * HEAD_DIM:(hk + 1) * HEAD_DIM].astype(BF16)
        v = v_ref[:, hk * HEAD_DIM:(hk + 1) * HEAD_DIM].astype(BF16)
        heads = [hk * Q_PER_KV + g for g in range(Q_PER_KV)]
        q4 = jnp.concatenate([q_ref[:, h * HEAD_DIM:(h + 1) * HEAD_DIM] for h in heads], axis=0).astype(BF16)
        sink_rep = jnp.concatenate([jnp.full((rows, LANES), sink_ref[h], F32) for h in heads], axis=0)
        o4 = _attend(q4, [k], [v], sink_rep, [None], merge_sums=False)
        for g, h in enumerate(heads):
            outs[h] = o4[g * rows:(g + 1) * rows]
    o_ref[...] = jnp.concatenate(outs, axis=1).astype(o_ref.dtype)


def _attn_lat_kernel(sink_ref, q_ref, kp_ref, kc_ref, kn_ref, vp_ref, vc_ref, vn_ref, ck_ref, cv_ref,
                     o_ref, *, n_blk):
    qi = pl.program_id(1)
    rows = CHUNK
    r4 = Q_PER_KV * rows
    a = lax.broadcasted_iota(jnp.int32, (r4, CHUNK), 0) % rows
    s = lax.broadcasted_iota(jnp.int32, (r4, CHUNK), 1)
    mask_prev = (s >= a) & (qi > 0)
    mask_next = (s <= a) & (qi < n_blk - 1)
    outs = [None] * N_Q_HEADS
    for hk in range(N_KV_HEADS):
        sl = slice(hk * HEAD_DIM, (hk + 1) * HEAD_DIM)
        keys = [kp_ref[:, sl].astype(BF16), kc_ref[:, sl].astype(BF16), kn_ref[:, sl].astype(BF16),
                ck_ref[:, sl].astype(BF16)]
        vals = [vp_ref[:, sl].astype(BF16), vc_ref[:, sl].astype(BF16), vn_ref[:, sl].astype(BF16),
                cv_ref[:, sl].astype(BF16)]
        heads = [hk * Q_PER_KV + g for g in range(Q_PER_KV)]
        q4 = jnp.concatenate([q_ref[:, h * HEAD_DIM:(h + 1) * HEAD_DIM] for h in heads], axis=0).astype(BF16)
        sink_rep = jnp.concatenate([jnp.full((rows, LANES), sink_ref[h], F32) for h in heads], axis=0)
        o4 = _attend(q4, keys, vals, sink_rep, [mask_prev, None, mask_next, None], merge_sums=True)
        for g, h in enumerate(heads):
            outs[h] = o4[g * rows:(g + 1) * rows]
    o_ref[...] = jnp.concatenate(outs, axis=1).astype(o_ref.dtype)


def _attention(qkv, sinks, ctx_k, ctx_v):
    smem = pl.BlockSpec(memory_space=pltpu.SMEM)
    kcol, vcol = Q_DIM // KV_DIM, Q_DIM // KV_DIM + 1
    ctx_blk = CTX_LEN
    o_ctx = pl.pallas_call(
        _attn_ctx_kernel,
        grid=(N_CTX_SEQ,),
        in_specs=[
            smem,
            pl.BlockSpec((ctx_blk, Q_DIM), lambda b: (b, 0)),
            pl.BlockSpec((ctx_blk, KV_DIM), lambda b: (b, kcol)),
            pl.BlockSpec((ctx_blk, KV_DIM), lambda b: (b, vcol)),
        ],
        out_specs=pl.BlockSpec((ctx_blk, Q_DIM), lambda b: (b, 0)),
        out_shape=jax.ShapeDtypeStruct((CTX_TOK, Q_DIM), BF16),
        compiler_params=_cparams(("arbitrary",)),
        name="attn_ctx",
    )(sinks, qkv, qkv, qkv)

    n_blk = LAT_LEN // CHUNK
    base = CTX_TOK // CHUNK

    def rb(b, qi, off):
        return base + b * n_blk + jnp.clip(qi + off, 0, n_blk - 1)

    def kv_spec(col, off):
        return pl.BlockSpec((CHUNK, KV_DIM), lambda b, qi: (rb(b, qi, off), col))

    o_lat = pl.pallas_call(
        functools.partial(_attn_lat_kernel, n_blk=n_blk),
        grid=(N_LAT_SEQ, n_blk),
        in_specs=[
            smem,
            pl.BlockSpec((CHUNK, Q_DIM), lambda b, qi: (rb(b, qi, 0), 0)),
            kv_spec(kcol, -1), kv_spec(kcol, 0), kv_spec(kcol, 1),
            kv_spec(vcol, -1), kv_spec(vcol, 0), kv_spec(vcol, 1),
            pl.BlockSpec((None, CTX_LEN, KV_DIM), lambda b, qi: (b, 0, 0)),
            pl.BlockSpec((None, CTX_LEN, KV_DIM), lambda b, qi: (b, 0, 0)),
        ],
        out_specs=pl.BlockSpec((CHUNK, Q_DIM), lambda b, qi: (b * n_blk + qi, 0)),
        out_shape=jax.ShapeDtypeStruct((LAT_TOK, Q_DIM), BF16),
        compiler_params=_cparams(("arbitrary", "arbitrary")),
        name="attn_lat",
    )(sinks, qkv, qkv, qkv, qkv, qkv, qkv, qkv, ctx_k, ctx_v)
    return o_ctx, o_lat


def _softplus(x):
    return jnp.maximum(x, 0.0) + jnp.log1p(jnp.exp(-jnp.abs(x)))


SSM_TN = 1024
SSM_Z_STEPS = SSM_INNER // SSM_TN
SSM_C_STEPS = SSM_CONV_DIM // SSM_TN


def _ssm_in_kernel(x_ref, sh_ref, sc_ref, g_ref, w_ref, cw_ref, cb_ref, wdt_ref, bdt_ref,
                   z_ref, xbc_ref, dt_ref, h_ref, *, tm):
    i = pl.program_id(0)
    j = pl.program_id(1)

    @pl.when(j == 0)
    def _():
        h = _adaln(x_ref[...], g_ref[...], sh_ref[...], sc_ref[...]).astype(BF16)
        h_ref[...] = h
        y = jnp.dot(h, wdt_ref[...].astype(BF16), preferred_element_type=F32)
        dt_ref[...] = _softplus(y + bdt_ref[...])
        z_ref[...] = jnp.dot(h, w_ref[...].astype(BF16), preferred_element_type=F32).astype(BF16)

    @pl.when((j > 0) & (j < SSM_Z_STEPS))
    def _():
        z_ref[...] = jnp.dot(h_ref[...], w_ref[...].astype(BF16), preferred_element_type=F32).astype(BF16)

    @pl.when(j >= SSM_Z_STEPS)
    def _():
        y = jnp.dot(h_ref[...], w_ref[...].astype(BF16), preferred_element_type=F32)
        seq_len = jnp.where(i * tm < CTX_TOK, CTX_LEN, LAT_LEN)
        pos = lax.broadcasted_iota(jnp.int32, (tm, 1), 0) & (seq_len - 1)
        up = jnp.where(pos == 0, 0.0, pltpu.roll(y, 1, axis=0))
        dn = jnp.where(pos == seq_len - 1, 0.0, pltpu.roll(y, tm - 1, axis=0))
        c = up * cw_ref[0:1, :] + y * cw_ref[1:2, :] + dn * cw_ref[2:3, :] + cb_ref[...]
        xbc_ref[...] = _silu(c).astype(BF16)


def _ssm_in(x, mods, gains, w, widx, conv_w, conv_b, w_dt, b_dt, layer, tm=1024):
    tn = SSM_TN
    n_main = SSM_Z_STEPS + SSM_C_STEPS

    def conv_col(j):
        return jnp.clip(j - SSM_Z_STEPS, 0, SSM_C_STEPS - 1)

    return pl.pallas_call(
        functools.partial(_ssm_in_kernel, tm=tm),
        grid=(TOK // tm, n_main),
        in_specs=[
            pl.BlockSpec((tm, D), lambda i, j: (i, 0)),
            _mod_spec(layer, 3, tm, 2),
            _mod_spec(layer, 4, tm, 2),
            _gain_spec(layer, 2, 2),
            pl.BlockSpec((None, D, tn), lambda i, j: (widx, 0, j)),
            pl.BlockSpec((3, tn), lambda i, j: (0, conv_col(j))),
            pl.BlockSpec((1, tn), lambda i, j: (0, conv_col(j))),
            pl.BlockSpec((D, LANES), lambda i, j: (0, 0)),
            pl.BlockSpec((1, LANES), lambda i, j: (0, 0)),
        ],
        out_specs=[
            pl.BlockSpec((tm, tn), lambda i, j: (i, jnp.minimum(j, SSM_Z_STEPS - 1))),
            pl.BlockSpec((tm, tn), lambda i, j: (i, conv_col(j))),
            pl.BlockSpec((tm, LANES), lambda i, j: (i, 0)),
        ],
        out_shape=[jax.ShapeDtypeStruct((TOK, SSM_INNER), BF16), jax.ShapeDtypeStruct((TOK, SSM_CONV_DIM), BF16),
                   jax.ShapeDtypeStruct((TOK, LANES), F32)],
        scratch_shapes=[pltpu.VMEM((tm, D), BF16)],
        compiler_params=_cparams(("arbitrary", "arbitrary")),
        name=f"ssm_in_l{layer}",
    )(x, mods, mods, gains, w, conv_w, conv_b, w_dt, b_dt)


def _split3(q):
    hi = q.astype(BF16)
    r1 = q - hi.astype(F32)
    mid = r1.astype(BF16)
    lo = (r1 - mid.astype(F32)).astype(BF16)
    return hi, mid, lo


def _spread(q, r_ref):
    hi, mid, lo = _split3(q[:, :SSM_HEADS])
    return jnp.dot(jnp.concatenate([hi, mid, lo], axis=1), r_ref[...], preferred_element_type=F32)


def _spread_consts():
    r = jnp.arange(3 * SSM_HEADS) % SSM_HEADS
    r1 = (jnp.arange(SSM_HEADS * SSM_P)[None, :] // SSM_P == r[:, None]).astype(BF16)
    r2 = (jnp.arange(SSM_HEADS * CHUNK)[None, :] // CHUNK == r[:, None]).astype(BF16)
    return r1, r2


def _ssd_chain(xbc, dt_all, a_ref, r1_ref, r2_ref, hs_ref, bwd):
    L = CHUNK
    d = 1 if bwd else 0
    xs = xbc[:, :SSM_INNER].astype(F32)
    bcb = xbc[:, SSM_INNER:]

    dt = pltpu.roll(dt_all, LANES - SSM_HEADS, axis=1) if bwd else dt_all
    a_row = a_ref[d:d + 1, :]
    ii = lax.broadcasted_iota(jnp.int32, (L, L), 0)
    jj = lax.broadcasted_iota(jnp.int32, (L, L), 1)
    tri = (jj >= ii) if bwd else (ii >= jj)
    tri_b = jnp.where(tri, 1.0, 0.0).astype(BF16)
    neg_mask = jnp.where(tri, 0.0, -jnp.inf)
    hi_p, mid_p, lo_p = _split3(dt * a_row)
    acum = (jnp.dot(tri_b, hi_p, preferred_element_type=F32)
            + jnp.dot(tri_b, mid_p, preferred_element_type=F32)
            + jnp.dot(tri_b, lo_p, preferred_element_type=F32))
    end = 0 if bwd else L - 1
    total = acum[end:end + 1, :]
    acum2 = acum * LOG2E
    acum_t = acum2.T
    cdec_t = jnp.exp2(acum_t[:, end:end + 1])

    e_acum = _spread(acum2, r2_ref)
    e_dt = _spread(dt, r1_ref)
    e_end = _spread(dt * jnp.exp(total - acum), r1_ref)
    e_in = _spread(jnp.exp(acum), r1_ref)
    xdt = (xs * e_dt).astype(BF16)
    xw = (xs * e_end).astype(BF16)
    left = lax.broadcasted_iota(jnp.int32, (L, LANES), 1) < SSM_P

    y_parts = []
    for g in range(SSM_GROUPS):
        bm = bcb[:, g * SSM_STATE:(g + 1) * SSM_STATE]
        cm = bcb[:, SSM_GN + g * SSM_STATE:SSM_GN + (g + 1) * SSM_STATE]
        cb = lax.dot_general(cm, bm, (((1,), (1,)), ((), ())), preferred_element_type=F32)
        gsl = slice(g * SSM_HG * SSM_P, (g + 1) * SSM_HG * SSM_P)
        hprev = hs_ref[d, gsl, :]
        y_off = lax.dot_general(cm, hprev.astype(BF16), (((1,), (1,)), ((), ())),
                                preferred_element_type=F32)
        y_pairs = []
        for pr in range(SSM_HG // 2):
            wm = []
            for h in (g * SSM_HG + 2 * pr, g * SSM_HG + 2 * pr + 1):
                seg = e_acum[:, h * L:(h + 1) * L] - acum_t[h:h + 1, :] + neg_mask
                wm.append((cb * jnp.exp2(seg)).astype(BF16))
            pair = xdt[:, (g * SSM_HG + 2 * pr) * SSM_P:(g * SSM_HG + 2 * pr + 2) * SSM_P]
            rhs = jnp.concatenate([jnp.where(left, pair, jnp.zeros_like(pair)),
                                   jnp.where(left, jnp.zeros_like(pair), pair)], axis=0)
            y_pairs.append(jnp.dot(jnp.concatenate(wm, axis=1), rhs, preferred_element_type=F32))
        y_parts.append(jnp.concatenate(y_pairs, axis=1) + y_off * e_in[:, gsl])
        st = lax.dot_general(xw[:, gsl], bm, (((0,), (0,)), ((), ())), preferred_element_type=F32)
        dec = jnp.concatenate([jnp.broadcast_to(cdec_t[g * SSM_HG + k:g * SSM_HG + k + 1, :], (SSM_P, 1))
                               for k in range(SSM_HG)], axis=0)
        hs_ref[d, gsl, :] = dec * hprev + st
    return jnp.concatenate(y_parts, axis=1), xs


def _ssd_kernel(*refs, nc, has_h0, emit_state):
    (xa_ref, xb_ref, dta_ref, dtb_ref, za_ref, zb_ref, a_ref, dsk_ref, ng_ref, r1_ref, r2_ref,
     wo_ref, xres_ref, gt_ref, g3_ref) = refs[:15]
    pos = 15
    h0_ref = None
    if has_h0:
        h0_ref = refs[pos]
        pos += 1
    y_ref = refs[pos]
    pos += 1
    st_ref = None
    if emit_state:
        st_ref = refs[pos]
        pos += 1
    hs_ref, yacc_ref = refs[pos:pos + 2]

    s = pl.program_id(1)
    L = CHUNK
    half = nc // 2
    off_a = pl.multiple_of(s * L, L)
    off_b = pl.multiple_of((nc - 1 - s) * L, L)

    @pl.when(s == 0)
    def _():
        hs_ref[...] = h0_ref[...] if has_h0 else jnp.zeros_like(hs_ref)

    ya, xs_a = _ssd_chain(xa_ref[...], dta_ref[...], a_ref, r1_ref, r2_ref, hs_ref, bwd=False)
    ya = ya + dsk_ref[...] * xs_a
    yb, _ = _ssd_chain(xb_ref[...], dtb_ref[...], a_ref, r1_ref, r2_ref, hs_ref, bwd=True)

    @pl.when(s < half)
    def _():
        yacc_ref[pl.ds(off_a, L), :] = ya
        yacc_ref[pl.ds(off_b, L), :] = yb

    @pl.when(s >= half)
    def _():
        for off, y_new, z_ref in ((off_a, ya, za_ref), (off_b, yb, zb_ref)):
            yt = (yacc_ref[pl.ds(off, L), :] + y_new) * _silu(z_ref[...].astype(F32))
            yn = _rms(yt, ng_ref[...]).astype(BF16)
            m = jnp.dot(yn, wo_ref[...], preferred_element_type=F32)
            y_ref[pl.ds(off, L), :] = xres_ref[pl.ds(off, L), :] + gt_ref[...] * _rms(m, g3_ref[...])

    if emit_state:
        @pl.when(s == nc - 1)
        def _():
            st_ref[...] = hs_ref[...]


def _ssd(z, xbc, dt, a_pad, dsk, norm_g, r1, r2, w_out, x, mods, gains, layer, h0, *,
         seq0, n_seq, seq_len, emit_state):
    nc = seq_len // CHUNK
    half = nc // 2
    chunk0 = seq0 // CHUNK
    seq_blk0 = seq0 // seq_len
    mod_base = layer * MOD_ROWS * N_MOD + 5
    mod_row0 = 0 if seq0 < CTX_TOK else 1
    mod_step = 0 if seq0 < CTX_TOK else N_MOD

    def fwd_chunk(b, s):
        return chunk0 + b * nc + s

    def bwd_chunk(b, s):
        return chunk0 + b * nc + nc - 1 - s

    def late(s):
        return jnp.maximum(s, half)

    const = lambda b, s: (0, 0)
    in_specs = [
        pl.BlockSpec((CHUNK, SSM_CONV_DIM), lambda b, s: (fwd_chunk(b, s), 0)),
        pl.BlockSpec((CHUNK, SSM_CONV_DIM), lambda b, s: (bwd_chunk(b, s), 0)),
        pl.BlockSpec((CHUNK, LANES), lambda b, s: (fwd_chunk(b, s), 0)),
        pl.BlockSpec((CHUNK, LANES), lambda b, s: (bwd_chunk(b, s), 0)),
        pl.BlockSpec((CHUNK, SSM_INNER), lambda b, s: (fwd_chunk(b, late(s)), 0)),
        pl.BlockSpec((CHUNK, SSM_INNER), lambda b, s: (bwd_chunk(b, late(s)), 0)),
        pl.BlockSpec((2, LANES), const),
        pl.BlockSpec((1, SSM_INNER), const),
        pl.BlockSpec((1, SSM_INNER), const),
        pl.BlockSpec(r1.shape, const),
        pl.BlockSpec(r2.shape, const),
        pl.BlockSpec((SSM_INNER, D), const, pipeline_mode=pl.Buffered(1)),
        pl.BlockSpec((seq_len, D), lambda b, s: (seq_blk0 + b, 0)),
        pl.BlockSpec((None, 1, D), lambda b, s: (mod_base + mod_row0 * N_MOD + b * mod_step, 0, 0)),
        pl.BlockSpec((None, 1, D), lambda b, s: (layer * 6 + 3, 0, 0)),
    ]
    args = [xbc, xbc, dt, dt, z, z, a_pad, dsk, norm_g, r1, r2, w_out, x, mods, gains]
    if h0 is not None:
        in_specs.append(pl.BlockSpec((None, 2, SSM_INNER, SSM_STATE), lambda b, s: (b, 0, 0, 0)))
        args.append(h0)
    out_specs = [pl.BlockSpec((seq_len, D), lambda b, s: (b, 0))]
    out_shape = [jax.ShapeDtypeStruct((n_seq * seq_len, D), F32)]
    if emit_state:
        out_specs.append(pl.BlockSpec((None, 2, SSM_INNER, SSM_STATE), lambda b, s: (b, 0, 0, 0)))
        out_shape.append(jax.ShapeDtypeStruct((n_seq, 2, SSM_INNER, SSM_STATE), F32))
    return pl.pallas_call(
        functools.partial(_ssd_kernel, nc=nc, has_h0=h0 is not None, emit_state=emit_state),
        grid=(n_seq, nc),
        in_specs=in_specs,
        out_specs=out_specs,
        out_shape=out_shape,
        scratch_shapes=[pltpu.VMEM((2, SSM_INNER, SSM_STATE), F32), pltpu.VMEM((seq_len, SSM_INNER), F32)],
        compiler_params=_cparams(("arbitrary", "arbitrary")),
        name=f"ssd_{seq_len}",
    )(*args)


def kernel(x_prompt, x_sample, cache_k, cache_v, state_ssm, c, c_ctx, w_mod, b_mod, norm_g, ffn_in, ffn_out,
           gmlp_in, gmlp_ln_g, gmlp_ln_b, gmlp_ws, gmlp_bs, gmlp_out, attn_qkv, attn_sink, attn_out,
           ssm_in, ssm_conv_w, ssm_conv_b, ssm_dt_bias, ssm_a_log, ssm_d, ssm_norm, ssm_out):
    cond = jnp.concatenate([c_ctx[None, :], c, jnp.zeros((MOD_ROWS - 1 - N_LAT_SEQ, D), F32)], axis=0)
    mods = _modulation(cond, w_mod, b_mod).reshape(DEPTH * MOD_ROWS * N_MOD, 1, D)
    gains = norm_g.reshape(DEPTH * 6, 1, D)

    x = (x_prompt.reshape(CTX_TOK, D), x_sample.reshape(LAT_TOK, D))
    new_k = new_v = new_s = None
    for layer in range(DEPTH):
        x = _ffn(x, mods, gains, ffn_in, ffn_out, layer, 0, split_in=layer == 0)
        kind, j = layer % 3, layer // 3
        if kind == 0:
            x = _gmlp(x, mods, gains, gmlp_in, gmlp_ln_g[:, None, :], gmlp_ln_b[:, None, :], gmlp_ws,
                      gmlp_bs[..., None], gmlp_out, j, layer)
        elif kind == 1:
            qkv, kc, vc = _qkv_proj(x, mods, gains, attn_qkv, j, layer)
            new_k = kc.reshape(N_CTX_SEQ, N_KV_HEADS, HEAD_DIM, CTX_LEN).transpose(0, 3, 1, 2)
            new_v = vc.reshape(N_CTX_SEQ, N_KV_HEADS, HEAD_DIM, CTX_LEN).transpose(0, 3, 1, 2)
            o_ctx, o_lat = _attention(qkv, attn_sink[j],
                                      cache_k[:, j].reshape(N_LAT_SEQ, CTX_LEN, KV_DIM),
                                      cache_v[:, j].reshape(N_LAT_SEQ, CTX_LEN, KV_DIM))
            x = _out_proj(o_ctx, o_lat, attn_out, j, x, mods, gains, layer)
        else:
            pad = LANES - 2 * SSM_HEADS
            w_dt = jnp.pad(ssm_in[j][:, SSM_MAIN:], ((0, 0), (0, pad)))
            b_dt = jnp.pad(ssm_dt_bias[j].reshape(1, 2 * SSM_HEADS), ((0, 0), (0, pad)))
            z, xbc, dt = _ssm_in(x, mods, gains, ssm_in, j, ssm_conv_w[j], ssm_conv_b[j][None, :],
                                 w_dt, b_dt, layer)
            a_pad = jnp.pad(-jnp.exp(ssm_a_log[j]), ((0, 0), (0, LANES - SSM_HEADS)))
            dsk = jnp.repeat(ssm_d[j], SSM_P)[None, :]
            ng = ssm_norm[j][None, :]
            r1, r2 = _spread_consts()
            w_o = ssm_out[j].astype(BF16)
            x_ctx, st = _ssd(z, xbc, dt, a_pad, dsk, ng, r1, r2, w_o, x, mods, gains, layer, None,
                             seq0=0, n_seq=N_CTX_SEQ, seq_len=CTX_LEN, emit_state=True)
            (x_lat,) = _ssd(z, xbc, dt, a_pad, dsk, ng, r1, r2, w_o, x, mods, gains, layer,
                            state_ssm[:, j].reshape(N_LAT_SEQ, 2, SSM_INNER, SSM_STATE),
                            seq0=CTX_TOK, n_seq=N_LAT_SEQ, seq_len=LAT_LEN, emit_state=False)
            new_s = st.reshape(N_CTX_SEQ, 2, SSM_HEADS, SSM_P, SSM_STATE)
            x = (x_ctx, x_lat)
        x = _ffn(x, mods, gains, ffn_in, ffn_out, layer, 1, split_in=isinstance(x, tuple),
                 split_out=layer == DEPTH - 1)

    y_prompt = x[0].reshape(N_CTX_SEQ, CTX_LEN, D)
    y_sample = x[1].reshape(N_LAT_SEQ, LAT_LEN, D)
    return (y_prompt, y_sample, new_k[:, None], new_v[:, None], new_s[:, None])
```

```python
import functools
import math

import jax
import jax.numpy as jnp
from jax import lax
from jax.experimental import pallas as pl
from jax.experimental.pallas import tpu as pltpu

F32 = jnp.float32
BF16 = jnp.bfloat16

D = 1024
N_CTX_SEQ, CTX_LEN = 16, 256
N_LAT_SEQ, LAT_LEN = 4, 1024
CTX_TOK = N_CTX_SEQ * CTX_LEN
LAT_TOK = N_LAT_SEQ * LAT_LEN
TOK = CTX_TOK + LAT_TOK
DEPTH = 4
N_MOD = 9
MOD_ROWS = 8
D_FF = 2816
EPS = 1e-6
GRID_W = 64
GMLP_HALF = 3 * D
GMLP_GROUPS = 8
GMLP_GD = GMLP_HALF // GMLP_GROUPS
CHUNK = 128
HEAD_DIM = 64
N_Q_HEADS = 16
N_KV_HEADS = 4
Q_PER_KV = 4
Q_DIM = N_Q_HEADS * HEAD_DIM
KV_DIM = N_KV_HEADS * HEAD_DIM
QKV_DIM = Q_DIM + 2 * KV_DIM
ATTN_SCALE = HEAD_DIM ** -0.5
ROPE_BASE = 10000.0
ROT_PAIRS = HEAD_DIM // 4
NEG_INF = -1e30
SSM_INNER = 2 * D
SSM_HEADS = 32
SSM_P = 64
SSM_GROUPS = 4
SSM_HG = SSM_HEADS // SSM_GROUPS
SSM_STATE = 128
SSM_GN = SSM_GROUPS * SSM_STATE
SSM_CONV_DIM = SSM_INNER + 2 * SSM_GN
SSM_MAIN = SSM_INNER + SSM_CONV_DIM
LANES = 128
LOG2E = 1.0 / math.log(2.0)

VMEM_LIMIT = 56 * 1024 * 1024
VMEM_LIMIT_BIG = 60 * 1024 * 1024


def _cparams(sem, limit=VMEM_LIMIT):
    return pltpu.CompilerParams(dimension_semantics=sem, vmem_limit_bytes=limit)


def _sigmoid(x):
    return 1.0 / (1.0 + jnp.exp(-x))


def _silu(x):
    return x * _sigmoid(x)


def _rms(x, g):
    return x * lax.rsqrt(jnp.mean(x * x, axis=-1, keepdims=True) + EPS) * g


def _adaln(x, g, shift, scale):
    return _rms(x, g) * (1.0 + scale) + shift


def _mod_row(i, tm):
    t0 = i * tm
    return jnp.where(t0 < CTX_TOK, 0, 1 + (t0 - CTX_TOK) // LAT_LEN)


def _mod_spec(layer, k, tm, grid_rank):
    base = layer * MOD_ROWS * N_MOD + k
    if grid_rank == 1:
        return pl.BlockSpec((None, 1, D), lambda i: (base + _mod_row(i, tm) * N_MOD, 0, 0))
    return pl.BlockSpec((None, 1, D), lambda i, j: (base + _mod_row(i, tm) * N_MOD, 0, 0))


def _gain_spec(layer, k, grid_rank):
    idx = layer * 6 + k
    if grid_rank == 1:
        return pl.BlockSpec((None, 1, D), lambda i: (idx, 0, 0))
    return pl.BlockSpec((None, 1, D), lambda i, j: (idx, 0, 0))


def _split_specs(tm, width, grid_rank):
    na = CTX_TOK // tm
    if grid_rank == 1:
        return (pl.BlockSpec((tm, width), lambda i: (jnp.minimum(i, na - 1), 0)),
                pl.BlockSpec((tm, width), lambda i: (jnp.maximum(i - na, 0), 0)))
    return (pl.BlockSpec((tm, width), lambda i, j: (jnp.minimum(i, na - 1), 0)),
            pl.BlockSpec((tm, width), lambda i, j: (jnp.maximum(i - na, 0), 0)))


def _mod_kernel(cond_ref, w_ref, b_ref, o_ref):
    s = _silu(cond_ref[...]).astype(BF16)
    o_ref[...] = jnp.dot(s, w_ref[...].astype(BF16), preferred_element_type=F32) + b_ref[...]


def _modulation(cond, w_mod, b_mod):
    tn = 2304
    n = N_MOD * D
    return pl.pallas_call(
        _mod_kernel,
        grid=(DEPTH, n // tn),
        in_specs=[
            pl.BlockSpec((MOD_ROWS, D), lambda l, j: (0, 0)),
            pl.BlockSpec((None, D, tn), lambda l, j: (l, 0, j)),
            pl.BlockSpec((None, 1, tn), lambda l, j: (l, 0, j)),
        ],
        out_specs=pl.BlockSpec((None, MOD_ROWS, tn), lambda l, j: (l, 0, j)),
        out_shape=jax.ShapeDtypeStruct((DEPTH, MOD_ROWS, n), F32),
        compiler_params=_cparams(("arbitrary", "arbitrary")),
        name="modulation",
    )(cond, w_mod, b_mod.reshape(DEPTH, 1, n))


FFN_TF = 256
FFN_SLABS = D_FF // FFN_TF
FFN_SPS = 4
FFN_STEP_SLABS = (3, 4, 4)
FFN_STEPS = len(FFN_STEP_SLABS)
assert sum(FFN_STEP_SLABS) == FFN_SLABS and max(FFN_STEP_SLABS) == FFN_SPS


def _ffn_kernel(*refs, tm, split_in, split_out):
    n_x = 2 if split_in else 1
    n_o = 2 if split_out else 1
    x_refs = refs[:n_x]
    sh_ref, sc_ref, gt_ref, g0_ref, g1_ref = refs[n_x:n_x + 5]
    w0 = n_x + 5
    wg = refs[w0:w0 + FFN_SPS]
    wu = refs[w0 + FFN_SPS:w0 + 2 * FFN_SPS]
    wo = refs[w0 + 2 * FFN_SPS:w0 + 3 * FFN_SPS]
    o_refs = refs[w0 + 3 * FFN_SPS:w0 + 3 * FFN_SPS + n_o]
    h_ref, acc_ref = refs[w0 + 3 * FFN_SPS + n_o:]
    i = pl.program_id(0)
    j = pl.program_id(1)
    is_ctx = i * tm < CTX_TOK
    last = FFN_STEPS - 1

    def load_x():
        if split_in:
            return jnp.where(is_ctx, x_refs[0][...], x_refs[1][...])
        return x_refs[0][...]

    def swiglu_slabs(h, n):
        acts = []
        for k in range(n):
            g = jnp.dot(h, wg[k][...].astype(BF16), preferred_element_type=F32)
            u = jnp.dot(h, wu[k][...].astype(BF16), preferred_element_type=F32)
            acts.append((_silu(g) * u).astype(BF16))
        w = jnp.concatenate([wo[k][...].astype(BF16) for k in range(n)], axis=0)
        return jnp.dot(jnp.concatenate(acts, axis=1), w, preferred_element_type=F32)

    @pl.when(j == 0)
    def _():
        h = _adaln(load_x(), g0_ref[...], sh_ref[...], sc_ref[...]).astype(BF16)
        h_ref[...] = h
        acc_ref[...] = swiglu_slabs(h, FFN_STEP_SLABS[0])

    for step in range(1, last):
        @pl.when(j == step)
        def _(step=step):
            acc_ref[...] += swiglu_slabs(h_ref[...], FFN_STEP_SLABS[step])

    @pl.when(j == last)
    def _():
        f = acc_ref[...] + swiglu_slabs(h_ref[...], FFN_STEP_SLABS[last])
        res = load_x() + (0.5 * gt_ref[...]) * _rms(f, g1_ref[...])
        if split_out:
            @pl.when(is_ctx)
            def _():
                o_refs[0][...] = res

            @pl.when(jnp.logical_not(is_ctx))
            def _():
                o_refs[1][...] = res
        else:
            o_refs[0][...] = res


def _ffn(x, mods, gains, ffn_in, ffn_out, layer, which, split_in=False, split_out=False):
    tm, tf = 1024, FFN_TF
    k0 = 0 if which == 0 else 6
    gi = 0 if which == 0 else 4
    if split_in:
        x_specs = list(_split_specs(tm, D, 2))
    else:
        x_specs = [pl.BlockSpec((tm, D), lambda i, j: (i, 0))]
    x_args = list(x) if split_in else [x]
    if split_out:
        out_specs = list(_split_specs(tm, D, 2))
        out_shape = [jax.ShapeDtypeStruct((CTX_TOK, D), F32), jax.ShapeDtypeStruct((LAT_TOK, D), F32)]
    else:
        out_specs = pl.BlockSpec((tm, D), lambda i, j: (i, 0))
        out_shape = jax.ShapeDtypeStruct((TOK, D), F32)

    starts = [sum(FFN_STEP_SLABS[:s]) for s in range(FFN_STEPS)]
    table = []
    for k in range(FFN_SPS):
        col = [starts[s] + k if k < FFN_STEP_SLABS[s] else None for s in range(FFN_STEPS)]
        for s in reversed(range(FFN_STEPS)):
            if col[s] is None:
                col[s] = col[s + 1] if s + 1 < FFN_STEPS else col[s - 1]
        table.append(col)

    def slab(j, k):
        idx = table[k][FFN_STEPS - 1]
        for s in reversed(range(FFN_STEPS - 1)):
            idx = jnp.where(j == s, table[k][s], idx)
        return idx

    def spec_in(k, col0):
        return pl.BlockSpec((None, None, D, tf), lambda i, j: (layer, which, 0, col0 + slab(j, k)))

    def spec_out(k):
        return pl.BlockSpec((None, None, tf, D), lambda i, j: (layer, which, slab(j, k), 0))

    w_specs = ([spec_in(k, 0) for k in range(FFN_SPS)] + [spec_in(k, FFN_SLABS) for k in range(FFN_SPS)]
               + [spec_out(k) for k in range(FFN_SPS)])
    return pl.pallas_call(
        functools.partial(_ffn_kernel, tm=tm, split_in=split_in, split_out=split_out),
        grid=(TOK // tm, FFN_STEPS),
        in_specs=x_specs + [
            _mod_spec(layer, k0 + 0, tm, 2),
            _mod_spec(layer, k0 + 1, tm, 2),
            _mod_spec(layer, k0 + 2, tm, 2),
            _gain_spec(layer, gi, 2),
            _gain_spec(layer, gi + 1, 2),
        ] + w_specs,
        out_specs=out_specs,
        out_shape=out_shape,
        scratch_shapes=[pltpu.VMEM((tm, D), BF16), pltpu.VMEM((tm, D), F32)],
        compiler_params=_cparams(("arbitrary", "arbitrary"), VMEM_LIMIT_BIG),
        name=f"ffn_l{layer}_{which}",
    )(*x_args, mods, mods, mods, gains, gains, *([ffn_in] * (2 * FFN_SPS)), *([ffn_out] * FFN_SPS))


def _out_kernel(aa_ref, ab_ref, w_ref, x_ref, gt_ref, g_ref, o_ref, *, tm):
    is_ctx = pl.program_id(0) * tm < CTX_TOK
    wb = w_ref[...].astype(BF16)
    for r in range(2):
        rows = slice(r * (tm // 2), (r + 1) * (tm // 2))
        a = jnp.where(is_ctx, aa_ref[rows, :], ab_ref[rows, :])
        m = jnp.dot(a, wb, preferred_element_type=F32)
        o_ref[rows, :] = x_ref[rows, :] + gt_ref[...] * _rms(m, g_ref[...])


def _out_proj(a_ctx, a_lat, w, widx, x, mods, gains, layer, tm=1024):
    k = a_ctx.shape[1]
    sa, sb = _split_specs(tm, k, 1)
    return pl.pallas_call(
        functools.partial(_out_kernel, tm=tm),
        grid=(TOK // tm,),
        in_specs=[
            sa, sb,
            pl.BlockSpec((None, k, D), lambda i: (widx, 0, 0)),
            pl.BlockSpec((tm, D), lambda i: (i, 0)),
            _mod_spec(layer, 5, tm, 1),
            _gain_spec(layer, 3, 1),
        ],
        out_specs=pl.BlockSpec((tm, D), lambda i: (i, 0)),
        out_shape=jax.ShapeDtypeStruct((TOK, D), F32),
        compiler_params=_cparams(("arbitrary",)),
        name=f"outproj_l{layer}",
    )(a_ctx, a_lat, w, x, mods, gains)


GMLP_GPS = 2
GMLP_SLABS = GMLP_GROUPS // GMLP_GPS
GMLP_COLS = GMLP_GPS * GMLP_GD


def _gelu_exact(x):
    return 0.5 * x * (1.0 + lax.erf(x * (1.0 / math.sqrt(2.0))))


def _fold_lanes(v):
    acc = v[:, :LANES]
    for cb in range(1, v.shape[1] // LANES):
        acc = acc + v[:, cb * LANES:(cb + 1) * LANES]
    return acc


GMLP_P1 = GMLP_SLABS // 2


def _gmlp_kernel(x_ref, sh_ref, sc_ref, g2_ref, wina_ref, winb_ref, lg_ref, lb_ref, ws_ref, bs_ref, wo_ref,
                 gt_ref, g3_ref, o_ref, h_ref, v_ref, c_ref, s1_ref, s2_ref, mu_ref, rstd_ref, *, tm):
    j = pl.program_id(1)
    last = GMLP_P1 + GMLP_SLABS - 1

    def slab(h, w_ref=wina_ref):
        return _gelu_exact(jnp.dot(h, w_ref[...].astype(BF16), preferred_element_type=F32))

    def park(y, idx):
        if idx == 0:
            c_ref[...] = jnp.sum(_fold_lanes(y), axis=-1, keepdims=True) * (1.0 / GMLP_COLS)
        dv = y - c_ref[...]
        if idx == 0:
            s1_ref[...] = _fold_lanes(dv)
            s2_ref[...] = _fold_lanes(dv * dv)
        else:
            s1_ref[...] += _fold_lanes(dv)
            s2_ref[...] += _fold_lanes(dv * dv)
        v_ref[idx] = y

    def mix(u, first):
        s = j - GMLP_P1
        vn = ((v_ref[s] - mu_ref[...]) * rstd_ref[...] * lg_ref[...] + lb_ref[...]).astype(BF16)
        rows = []
        for c in range(tm // CHUNK):
            cols = []
            for gg in range(GMLP_GPS):
                vg = vn[c * CHUNK:(c + 1) * CHUNK, gg * GMLP_GD:(gg + 1) * GMLP_GD]
                cols.append(jnp.dot(ws_ref[gg].astype(BF16), vg, preferred_element_type=F32) + bs_ref[gg])
            rows.append(jnp.concatenate(cols, axis=1))
        a = (u * jnp.concatenate(rows, axis=0)).astype(BF16)
        t = jnp.dot(a, wo_ref[...].astype(BF16), preferred_element_type=F32)
        return t if first else o_ref[...] + t

    @pl.when(j == 0)
    def _():
        h = _adaln(x_ref[...], g2_ref[...], sh_ref[...], sc_ref[...]).astype(BF16)
        h_ref[...] = h
        park(slab(h), 0)
        park(slab(h, winb_ref), 1)

    for step in range(1, GMLP_P1):
        @pl.when(j == step)
        def _(step=step):
            park(slab(h_ref[...]), 2 * step)
            park(slab(h_ref[...], winb_ref), 2 * step + 1)

    @pl.when(j == GMLP_P1)
    def _():
        d1 = jnp.sum(s1_ref[...], axis=-1, keepdims=True) * (1.0 / GMLP_HALF)
        d2 = jnp.sum(s2_ref[...], axis=-1, keepdims=True) * (1.0 / GMLP_HALF)
        mu_ref[...] = c_ref[...] + d1
        rstd_ref[...] = lax.rsqrt(d2 - d1 * d1 + EPS)
        o_ref[...] = mix(slab(h_ref[...]), True)

    @pl.when((j > GMLP_P1) & (j < last))
    def _():
        o_ref[...] = mix(slab(h_ref[...]), False)

    @pl.when(j == last)
    def _():
        m = mix(slab(h_ref[...]), False)
        o_ref[...] = x_ref[...] + gt_ref[...] * _rms(m, g3_ref[...])


def _gmlp(x, mods, gains, w_in, ln_g, ln_b, w_s, b_s, w_out, widx, layer, tm=1024):
    ns, p1 = GMLP_SLABS, GMLP_P1

    def u_slab(j):
        return jnp.maximum(j - p1, 0)

    def win_a(j):
        return jnp.where(j < p1, ns + 2 * j, j - p1)

    def win_b(j):
        return jnp.where(j < p1, ns + 2 * j + 1, ns + 1)

    return pl.pallas_call(
        functools.partial(_gmlp_kernel, tm=tm),
        grid=(TOK // tm, p1 + ns),
        in_specs=[
            pl.BlockSpec((tm, D), lambda i, j: (i, 0)),
            _mod_spec(layer, 3, tm, 2),
            _mod_spec(layer, 4, tm, 2),
            _gain_spec(layer, 2, 2),
            pl.BlockSpec((None, D, GMLP_COLS), lambda i, j: (widx, 0, win_a(j))),
            pl.BlockSpec((None, D, GMLP_COLS), lambda i, j: (widx, 0, win_b(j))),
            pl.BlockSpec((None, 1, GMLP_COLS), lambda i, j: (widx, 0, u_slab(j))),
            pl.BlockSpec((None, 1, GMLP_COLS), lambda i, j: (widx, 0, u_slab(j))),
            pl.BlockSpec((None, GMLP_GPS, CHUNK, CHUNK), lambda i, j: (widx, u_slab(j), 0, 0)),
            pl.BlockSpec((None, GMLP_GPS, CHUNK, 1), lambda i, j: (widx, u_slab(j), 0, 0)),
            pl.BlockSpec((None, GMLP_COLS, D), lambda i, j: (widx, u_slab(j), 0)),
            _mod_spec(layer, 5, tm, 2),
            _gain_spec(layer, 3, 2),
        ],
        out_specs=pl.BlockSpec((tm, D), lambda i, j: (i, 0)),
        out_shape=jax.ShapeDtypeStruct((TOK, D), F32),
        scratch_shapes=[pltpu.VMEM((tm, D), BF16), pltpu.VMEM((ns, tm, GMLP_COLS), F32),
                        pltpu.VMEM((tm, 1), F32), pltpu.VMEM((tm, LANES), F32), pltpu.VMEM((tm, LANES), F32),
                        pltpu.VMEM((tm, 1), F32), pltpu.VMEM((tm, 1), F32)],
        compiler_params=_cparams(("arbitrary", "arbitrary"), VMEM_LIMIT_BIG),
        name=f"gmlp_l{layer}",
    )(x, mods, mods, gains, w_in, w_in, ln_g, ln_b, w_s, b_s, w_out, mods, gains)


def _rope_tables():
    pos = jnp.arange(LAT_LEN)
    pos_r = (pos // GRID_W).astype(F32)
    pos_c = (pos % GRID_W).astype(F32)
    inv = ROPE_BASE ** (-jnp.arange(ROT_PAIRS, dtype=F32) / ROT_PAIRS)
    ang_r = pos_r[:, None] * inv
    ang_c = pos_c[:, None] * inv
    cos = jnp.concatenate([jnp.cos(ang_r)] * 2 + [jnp.cos(ang_c)] * 2, axis=1)
    sin = jnp.concatenate([-jnp.sin(ang_r), jnp.sin(ang_r), -jnp.sin(ang_c), jnp.sin(ang_c)], axis=1)
    reps = LANES // HEAD_DIM
    return jnp.tile(cos, (1, reps)), jnp.tile(sin, (1, reps))


def _qkv_kernel(x_ref, sh_ref, sc_ref, g_ref, w_ref, cos_ref, sin_ref, o_ref, kc_ref, vc_ref, *, tm):
    i = pl.program_id(0)
    hr = tm // 2

    def proj(rows, wb):
        h = _adaln(x_ref[rows, :], g_ref[...], sh_ref[...], sc_ref[...])
        return jnp.dot(h.astype(BF16), wb, preferred_element_type=F32)

    @pl.when(i * tm < CTX_TOK)
    def _():
        wb = w_ref[...].astype(BF16)
        for r in range(2):
            rows = slice(r * hr, (r + 1) * hr)
            y = proj(rows, wb)
            o_ref[rows, :] = y
            for sq in range(hr // CTX_LEN):
                blk = y[sq * CTX_LEN:(sq + 1) * CTX_LEN, :]
                b = r * (hr // CTX_LEN) + sq
                kc_ref[b] = blk[:, Q_DIM:Q_DIM + KV_DIM].T
                vc_ref[b] = blk[:, Q_DIM + KV_DIM:].T

    @pl.when(i * tm >= CTX_TOK)
    def _():
        wb = w_ref[...].astype(BF16)
        lane = lax.broadcasted_iota(jnp.int32, (hr, LANES), 1)
        first = (lane % (2 * ROT_PAIRS)) < ROT_PAIRS
        for r in range(2):
            rows = slice(r * hr, (r + 1) * hr)
            y = proj(rows, wb)
            cos = cos_ref[rows, :]
            sin = sin_ref[rows, :]
            for cb in range((Q_DIM + KV_DIM) // LANES):
                t = y[:, cb * LANES:(cb + 1) * LANES]
                partner = jnp.where(first, pltpu.roll(t, LANES - ROT_PAIRS, axis=1),
                                    pltpu.roll(t, ROT_PAIRS, axis=1))
                o_ref[rows, cb * LANES:(cb + 1) * LANES] = t * cos + partner * sin
            o_ref[rows, Q_DIM + KV_DIM:] = y[:, Q_DIM + KV_DIM:]


def _qkv_proj(x, mods, gains, w, widx, layer, tm=1024):
    cos, sin = _rope_tables()
    per_seq = LAT_LEN // tm
    tab = pl.BlockSpec((tm, LANES), lambda i: (jnp.maximum(i - CTX_TOK // tm, 0) % per_seq, 0))
    seq_per_tile = tm // CTX_LEN
    kv_spec = pl.BlockSpec((seq_per_tile, KV_DIM, CTX_LEN),
                           lambda i: (jnp.minimum(i, CTX_TOK // tm - 1), 0, 0))
    return pl.pallas_call(
        functools.partial(_qkv_kernel, tm=tm),
        grid=(TOK // tm,),
        in_specs=[
            pl.BlockSpec((tm, D), lambda i: (i, 0)),
            _mod_spec(layer, 3, tm, 1),
            _mod_spec(layer, 4, tm, 1),
            _gain_spec(layer, 2, 1),
            pl.BlockSpec((None, D, QKV_DIM), lambda i: (widx, 0, 0)),
            tab, tab,
        ],
        out_specs=[pl.BlockSpec((tm, QKV_DIM), lambda i: (i, 0)), kv_spec, kv_spec],
        out_shape=[jax.ShapeDtypeStruct((TOK, QKV_DIM), F32),
                   jax.ShapeDtypeStruct((N_CTX_SEQ, KV_DIM, CTX_LEN), F32),
                   jax.ShapeDtypeStruct((N_CTX_SEQ, KV_DIM, CTX_LEN), F32)],
        compiler_params=_cparams(("arbitrary",)),
        name=f"qkv_l{layer}",
    )(x, mods, mods, gains, w, cos, sin)


def _attend(q4, keys, vals, sink_rep, masks, merge_sums):
    logits = []
    for k, mk in zip(keys, masks):
        s = lax.dot_general(q4, k, (((1,), (1,)), ((), ())), preferred_element_type=F32) * ATTN_SCALE
        if mk is not None:
            s = jnp.where(mk, s, NEG_INF)
        logits.append([s[:, cb * LANES:(cb + 1) * LANES] for cb in range(s.shape[1] // LANES)])
    mx = None
    for blocks in logits:
        for blk in blocks:
            mx = blk if mx is None else jnp.maximum(mx, blk)
    m = jnp.maximum(sink_rep, jnp.max(mx, axis=-1, keepdims=True))
    acc = rs = None
    for blocks, v in zip(logits, vals):
        p = jnp.concatenate([jnp.exp(blk - m) for blk in blocks], axis=1).astype(BF16)
        if merge_sums:
            v = jnp.concatenate([v, jnp.ones_like(v)], axis=1)
        else:
            ps = jnp.dot(p, jnp.ones((p.shape[1], LANES), BF16), preferred_element_type=F32)
            rs = ps if rs is None else rs + ps
        pv = jnp.dot(p, v, preferred_element_type=F32)
        acc = pv if acc is None else acc + pv
    if merge_sums:
        denom = pltpu.roll(acc, HEAD_DIM, axis=1) + jnp.exp(sink_rep - m)
        return (acc / denom)[:, :HEAD_DIM]
    return acc / (rs + jnp.exp(sink_rep - m))[:, :HEAD_DIM]


def _attn_ctx_kernel(sink_ref, q_ref, k_ref, v_ref, o_ref):
    rows = CTX_LEN
    outs = [None] * N_Q_HEADS
    for hk in range(N_KV_HEADS):
        k = k_ref[:, hk * HEAD_DIM:(hk + 1) * HEAD_DIM].astype(BF16)
        v = v_ref[:, hk * HEAD_DIM:(hk + 1) * HEAD_DIM].astype(BF16)
        heads = [hk * Q_PER_KV + g for g in range(Q_PER_KV)]
        q4 = jnp.concatenate([q_ref[:, h * HEAD_DIM:(h + 1) * HEAD_DIM] for h in heads], axis=0).astype(BF16)
        sink_rep = jnp.concatenate([jnp.full((rows, LANES), sink_ref[h], F32) for h in heads], axis=0)
        o4 = _attend(q4, [k], [v], sink_rep, [None], merge_sums=False)
        for g, h in enumerate(heads):
            outs[h] = o4[g * rows:(g + 1) * rows]
    o_ref[...] = jnp.concatenate(outs, axis=1).astype(o_ref.dtype)


def _attn_lat_kernel(sink_ref, q_ref, kp_ref, kc_ref, kn_ref, vp_ref, vc_ref, vn_ref, ck_ref, cv_ref,
                     o_ref, *, n_blk):
    qi = pl.program_id(1)
    rows = CHUNK
    r4 = Q_PER_KV * rows
    a = lax.broadcasted_iota(jnp.int32, (r4, CHUNK), 0) % rows
    s = lax.broadcasted_iota(jnp.int32, (r4, CHUNK), 1)
    mask_prev = (s >= a) & (qi > 0)
    mask_next = (s <= a) & (qi < n_blk - 1)
    outs = [None] * N_Q_HEADS
    for hk in range(N_KV_HEADS):
        sl = slice(hk * HEAD_DIM, (hk + 1) * HEAD_DIM)
        keys = [kp_ref[:, sl].astype(BF16), kc_ref[:, sl].astype(BF16), kn_ref[:, sl].astype(BF16),
                ck_ref[:, sl].astype(BF16)]
        vals = [vp_ref[:, sl].astype(BF16), vc_ref[:, sl].astype(BF16), vn_ref[:, sl].astype(BF16),
                cv_ref[:, sl].astype(BF16)]
        heads = [hk * Q_PER_KV + g for g in range(Q_PER_KV)]
        q4 = jnp.concatenate([q_ref[:, h * HEAD_DIM:(h + 1) * HEAD_DIM] for h in heads], axis=0).astype(BF16)
        sink_rep = jnp.concatenate([jnp.full((rows, LANES), sink_ref[h], F32) for h in heads], axis=0)
        o4 = _attend(q4, keys, vals, sink_rep, [mask_prev, None, mask_next, None], merge_sums=True)
        for g, h in enumerate(heads):
            outs[h] = o4[g * rows:(g + 1) * rows]
    o_ref[...] = jnp.concatenate(outs, axis=1).astype(o_ref.dtype)


def _attention(qkv, sinks, ctx_k, ctx_v):
    smem = pl.BlockSpec(memory_space=pltpu.SMEM)
    kcol, vcol = Q_DIM // KV_DIM, Q_DIM // KV_DIM + 1
    ctx_blk = CTX_LEN
    o_ctx = pl.pallas_call(
        _attn_ctx_kernel,
        grid=(N_CTX_SEQ,),
        in_specs=[
            smem,
            pl.BlockSpec((ctx_blk, Q_DIM), lambda b: (b, 0)),
            pl.BlockSpec((ctx_blk, KV_DIM), lambda b: (b, kcol)),
            pl.BlockSpec((ctx_blk, KV_DIM), lambda b: (b, vcol)),
        ],
        out_specs=pl.BlockSpec((ctx_blk, Q_DIM), lambda b: (b, 0)),
        out_shape=jax.ShapeDtypeStruct((CTX_TOK, Q_DIM), BF16),
        compiler_params=_cparams(("arbitrary",)),
        name="attn_ctx",
    )(sinks, qkv, qkv, qkv)

    n_blk = LAT_LEN // CHUNK
    base = CTX_TOK // CHUNK

    def rb(b, qi, off):
        return base + b * n_blk + jnp.clip(qi + off, 0, n_blk - 1)

    def kv_spec(col, off):
        return pl.BlockSpec((CHUNK, KV_DIM), lambda b, qi: (rb(b, qi, off), col))

    o_lat = pl.pallas_call(
        functools.partial(_attn_lat_kernel, n_blk=n_blk),
        grid=(N_LAT_SEQ, n_blk),
        in_specs=[
            smem,
            pl.BlockSpec((CHUNK, Q_DIM), lambda b, qi: (rb(b, qi, 0), 0)),
            kv_spec(kcol, -1), kv_spec(kcol, 0), kv_spec(kcol, 1),
            kv_spec(vcol, -1), kv_spec(vcol, 0), kv_spec(vcol, 1),
            pl.BlockSpec((None, CTX_LEN, KV_DIM), lambda b, qi: (b, 0, 0)),
            pl.BlockSpec((None, CTX_LEN, KV_DIM), lambda b, qi: (b, 0, 0)),
        ],
        out_specs=pl.BlockSpec((CHUNK, Q_DIM), lambda b, qi: (b * n_blk + qi, 0)),
        out_shape=jax.ShapeDtypeStruct((LAT_TOK, Q_DIM), BF16),
        compiler_params=_cparams(("arbitrary", "arbitrary")),
        name="attn_lat",
    )(sinks, qkv, qkv, qkv, qkv, qkv, qkv, qkv, ctx_k, ctx_v)
    return o_ctx, o_lat


def _softplus(x):
    return jnp.maximum(x, 0.0) + jnp.log1p(jnp.exp(-jnp.abs(x)))


def _dot_nt(a, b):
    return lax.dot_general(a, b, (((1,), (1,)), ((), ())), preferred_element_type=F32)


SSM_TN = 1024
SSM_Z_STEPS = SSM_INNER // SSM_TN
SSM_C_STEPS = SSM_CONV_DIM // SSM_TN


def _ssm_in_kernel(x_ref, sh_ref, sc_ref, g_ref, w_ref, cw_ref, cb_ref, wdt_ref, bdt_ref,
                   z_ref, xbc_ref, dt_ref, h_ref, *, tm):
    i = pl.program_id(0)
    j = pl.program_id(1)

    @pl.when(j == 0)
    def _():
        h = _adaln(x_ref[...], g_ref[...], sh_ref[...], sc_ref[...]).astype(BF16)
        h_ref[...] = h
        y = _dot_nt(h, wdt_ref[...].astype(BF16))
        y = jnp.concatenate([y, jnp.zeros((tm, LANES - y.shape[1]), F32)], axis=1)
        dt_ref[...] = _softplus(y + bdt_ref[...])
        z_ref[...] = _dot_nt(h, w_ref[...].astype(BF16)).astype(BF16)

    @pl.when((j > 0) & (j < SSM_Z_STEPS))
    def _():
        z_ref[...] = _dot_nt(h_ref[...], w_ref[...].astype(BF16)).astype(BF16)

    @pl.when(j >= SSM_Z_STEPS)
    def _():
        y = _dot_nt(h_ref[...], w_ref[...].astype(BF16))
        seq_len = jnp.where(i * tm < CTX_TOK, CTX_LEN, LAT_LEN)
        pos = lax.broadcasted_iota(jnp.int32, (tm, 1), 0) & (seq_len - 1)
        up = jnp.where(pos == 0, 0.0, pltpu.roll(y, 1, axis=0))
        dn = jnp.where(pos == seq_len - 1, 0.0, pltpu.roll(y, tm - 1, axis=0))
        c = up * cw_ref[0:1, :] + y * cw_ref[1:2, :] + dn * cw_ref[2:3, :] + cb_ref[...]
        xbc_ref[...] = _silu(c).astype(BF16)


def _ssm_in(x, mods, gains, w_t, widx, conv_w, conv_b, b_dt, layer, tm=1024):
    tn = SSM_TN
    n_main = SSM_Z_STEPS + SSM_C_STEPS
    n_dt = 2 * SSM_HEADS

    def conv_col(j):
        return jnp.clip(j - SSM_Z_STEPS, 0, SSM_C_STEPS - 1)

    return pl.pallas_call(
        functools.partial(_ssm_in_kernel, tm=tm),
        grid=(TOK // tm, n_main),
        in_specs=[
            pl.BlockSpec((tm, D), lambda i, j: (i, 0)),
            _mod_spec(layer, 3, tm, 2),
            _mod_spec(layer, 4, tm, 2),
            _gain_spec(layer, 2, 2),
            pl.BlockSpec((None, tn, D), lambda i, j: (widx, j, 0)),
            pl.BlockSpec((3, tn), lambda i, j: (0, conv_col(j))),
            pl.BlockSpec((1, tn), lambda i, j: (0, conv_col(j))),
            pl.BlockSpec((None, n_dt, D), lambda i, j: (widx, SSM_MAIN // n_dt, 0)),
            pl.BlockSpec((1, LANES), lambda i, j: (0, 0)),
        ],
        out_specs=[
            pl.BlockSpec((tm, tn), lambda i, j: (i, jnp.minimum(j, SSM_Z_STEPS - 1))),
            pl.BlockSpec((tm, tn), lambda i, j: (i, conv_col(j))),
            pl.BlockSpec((tm, LANES), lambda i, j: (i, 0)),
        ],
        out_shape=[jax.ShapeDtypeStruct((TOK, SSM_INNER), BF16), jax.ShapeDtypeStruct((TOK, SSM_CONV_DIM), BF16),
                   jax.ShapeDtypeStruct((TOK, LANES), F32)],
        scratch_shapes=[pltpu.VMEM((tm, D), BF16)],
        compiler_params=_cparams(("arbitrary", "arbitrary")),
        name=f"ssm_in_l{layer}",
    )(x, mods, mods, gains, w_t, conv_w, conv_b, w_t, b_dt)


def _split3(q):
    hi = q.astype(BF16)
    r1 = q - hi.astype(F32)
    mid = r1.astype(BF16)
    lo = (r1 - mid.astype(F32)).astype(BF16)
    return hi, mid, lo


def _spread(q, r_ref):
    hi, mid, lo = _split3(q[:, :SSM_HEADS])
    return jnp.dot(jnp.concatenate([hi, mid, lo], axis=1), r_ref[...], preferred_element_type=F32)


def _spread_consts():
    r = jnp.arange(3 * SSM_HEADS) % SSM_HEADS
    r1 = (jnp.arange(SSM_HEADS * SSM_P)[None, :] // SSM_P == r[:, None]).astype(BF16)
    r2 = (jnp.arange(SSM_HEADS * CHUNK)[None, :] // CHUNK == r[:, None]).astype(BF16)
    return r1, r2


def _ssd_chain(xbc, dt_all, a_ref, r1_ref, r2_ref, hs_ref, bwd):
    L = CHUNK
    d = 1 if bwd else 0
    xs = xbc[:, :SSM_INNER].astype(F32)
    bcb = xbc[:, SSM_INNER:]

    dt = pltpu.roll(dt_all, LANES - SSM_HEADS, axis=1) if bwd else dt_all
    a_row = a_ref[d:d + 1, :]
    ii = lax.broadcasted_iota(jnp.int32, (L, L), 0)
    jj = lax.broadcasted_iota(jnp.int32, (L, L), 1)
    tri = (jj >= ii) if bwd else (ii >= jj)
    tri_b = jnp.where(tri, 1.0, 0.0).astype(BF16)
    neg_mask = jnp.where(tri, 0.0, -jnp.inf)
    hi_p, mid_p, lo_p = _split3(dt * a_row)
    acum = (jnp.dot(tri_b, hi_p, preferred_element_type=F32)
            + jnp.dot(tri_b, mid_p, preferred_element_type=F32)
            + jnp.dot(tri_b, lo_p, preferred_element_type=F32))
    end = 0 if bwd else L - 1
    total = acum[end:end + 1, :]
    acum2 = acum * LOG2E
    acum_t = acum2.T
    cdec_t = jnp.exp2(acum_t[:, end:end + 1])

    e_acum = _spread(acum2, r2_ref)
    e_dt = _spread(dt, r1_ref)
    e_end = _spread(dt * jnp.exp(total - acum), r1_ref)
    e_in = _spread(jnp.exp(acum), r1_ref)
    xdt = (xs * e_dt).astype(BF16)
    xw = (xs * e_end).astype(BF16)
    left = lax.broadcasted_iota(jnp.int32, (L, LANES), 1) < SSM_P

    y_parts = []
    for g in range(SSM_GROUPS):
        bm = bcb[:, g * SSM_STATE:(g + 1) * SSM_STATE]
        cm = bcb[:, SSM_GN + g * SSM_STATE:SSM_GN + (g + 1) * SSM_STATE]
        cb = lax.dot_general(cm, bm, (((1,), (1,)), ((), ())), preferred_element_type=F32)
        gsl = slice(g * SSM_HG * SSM_P, (g + 1) * SSM_HG * SSM_P)
        hprev = hs_ref[d, gsl, :]
        y_off = lax.dot_general(cm, hprev.astype(BF16), (((1,), (1,)), ((), ())),
                                preferred_element_type=F32)
        y_pairs = []
        for pr in range(SSM_HG // 2):
            wm = []
            for h in (g * SSM_HG + 2 * pr, g * SSM_HG + 2 * pr + 1):
                seg = e_acum[:, h * L:(h + 1) * L] - acum_t[h:h + 1, :] + neg_mask
                wm.append((cb * jnp.exp2(seg)).astype(BF16))
            pair = xdt[:, (g * SSM_HG + 2 * pr) * SSM_P:(g * SSM_HG + 2 * pr + 2) * SSM_P]
            rhs = jnp.concatenate([jnp.where(left, pair, jnp.zeros_like(pair)),
                                   jnp.where(left, jnp.zeros_like(pair), pair)], axis=0)
            y_pairs.append(jnp.dot(jnp.concatenate(wm, axis=1), rhs, preferred_element_type=F32))
        y_parts.append(jnp.concatenate(y_pairs, axis=1) + y_off * e_in[:, gsl])
        st = lax.dot_general(xw[:, gsl], bm, (((0,), (0,)), ((), ())), preferred_element_type=F32)
        dec = jnp.concatenate([jnp.broadcast_to(cdec_t[g * SSM_HG + k:g * SSM_HG + k + 1, :], (SSM_P, 1))
                               for k in range(SSM_HG)], axis=0)
        hs_ref[d, gsl, :] = dec * hprev + st
    return jnp.concatenate(y_parts, axis=1), xs


def _ssd_kernel(*refs, nc, has_h0, emit_state):
    (xa_ref, xb_ref, dta_ref, dtb_ref, za_ref, zb_ref, a_ref, dsk_ref, ng_ref, r1_ref, r2_ref,
     wo_ref, xres_ref, gt_ref, g3_ref) = refs[:15]
    pos = 15
    h0_ref = None
    if has_h0:
        h0_ref = refs[pos]
        pos += 1
    y_ref = refs[pos]
    pos += 1
    st_ref = None
    if emit_state:
        st_ref = refs[pos]
        pos += 1
    hs_ref, yacc_ref = refs[pos:pos + 2]

    s = pl.program_id(1)
    L = CHUNK
    half = nc // 2
    off_a = pl.multiple_of(s * L, L)
    off_b = pl.multiple_of((nc - 1 - s) * L, L)

    @pl.when(s == 0)
    def _():
        hs_ref[...] = h0_ref[...] if has_h0 else jnp.zeros_like(hs_ref)

    ya, xs_a = _ssd_chain(xa_ref[...], dta_ref[...], a_ref, r1_ref, r2_ref, hs_ref, bwd=False)
    ya = ya + dsk_ref[...] * xs_a
    yb, _ = _ssd_chain(xb_ref[...], dtb_ref[...], a_ref, r1_ref, r2_ref, hs_ref, bwd=True)

    @pl.when(s < half)
    def _():
        yacc_ref[pl.ds(off_a, L), :] = ya
        yacc_ref[pl.ds(off_b, L), :] = yb

    @pl.when(s >= half)
    def _():
        for off, y_new, z_ref in ((off_a, ya, za_ref), (off_b, yb, zb_ref)):
            yt = (yacc_ref[pl.ds(off, L), :] + y_new) * _silu(z_ref[...].astype(F32))
            yn = _rms(yt, ng_ref[...]).astype(BF16)
            m = jnp.dot(yn, wo_ref[...], preferred_element_type=F32)
            y_ref[pl.ds(off, L), :] = xres_ref[pl.ds(off, L), :] + gt_ref[...] * _rms(m, g3_ref[...])

    if emit_state:
        @pl.when(s == nc - 1)
        def _():
            st_ref[...] = hs_ref[...]


def _ssd(z, xbc, dt, a_pad, dsk, norm_g, r1, r2, w_out, x, mods, gains, layer, h0, *,
         seq0, n_seq, seq_len, emit_state):
    nc = seq_len // CHUNK
    half = nc // 2
    chunk0 = seq0 // CHUNK
    seq_blk0 = seq0 // seq_len
    mod_base = layer * MOD_ROWS * N_MOD + 5
    mod_row0 = 0 if seq0 < CTX_TOK else 1
    mod_step = 0 if seq0 < CTX_TOK else N_MOD

    def fwd_chunk(b, s):
        return chunk0 + b * nc + s

    def bwd_chunk(b, s):
        return chunk0 + b * nc + nc - 1 - s

    def late(s):
        return jnp.maximum(s, half)

    const = lambda b, s: (0, 0)
    in_specs = [
        pl.BlockSpec((CHUNK, SSM_CONV_DIM), lambda b, s: (fwd_chunk(b, s), 0)),
        pl.BlockSpec((CHUNK, SSM_CONV_DIM), lambda b, s: (bwd_chunk(b, s), 0)),
        pl.BlockSpec((CHUNK, LANES), lambda b, s: (fwd_chunk(b, s), 0)),
        pl.BlockSpec((CHUNK, LANES), lambda b, s: (bwd_chunk(b, s), 0)),
        pl.BlockSpec((CHUNK, SSM_INNER), lambda b, s: (fwd_chunk(b, late(s)), 0)),
        pl.BlockSpec((CHUNK, SSM_INNER), lambda b, s: (bwd_chunk(b, late(s)), 0)),
        pl.BlockSpec((2, LANES), const),
        pl.BlockSpec((1, SSM_INNER), const),
        pl.BlockSpec((1, SSM_INNER), const),
        pl.BlockSpec(r1.shape, const),
        pl.BlockSpec(r2.shape, const),
        pl.BlockSpec((SSM_INNER, D), const, pipeline_mode=pl.Buffered(1)),
        pl.BlockSpec((seq_len, D), lambda b, s: (seq_blk0 + b, 0)),
        pl.BlockSpec((None, 1, D), lambda b, s: (mod_base + mod_row0 * N_MOD + b * mod_step, 0, 0)),
        pl.BlockSpec((None, 1, D), lambda b, s: (layer * 6 + 3, 0, 0)),
    ]
    args = [xbc, xbc, dt, dt, z, z, a_pad, dsk, norm_g, r1, r2, w_out, x, mods, gains]
    if h0 is not None:
        in_specs.append(pl.BlockSpec((None, 2, SSM_INNER, SSM_STATE), lambda b, s: (b, 0, 0, 0)))
        args.append(h0)
    out_specs = [pl.BlockSpec((seq_len, D), lambda b, s: (b, 0))]
    out_shape = [jax.ShapeDtypeStruct((n_seq * seq_len, D), F32)]
    if emit_state:
        out_specs.append(pl.BlockSpec((None, 2, SSM_INNER, SSM_STATE), lambda b, s: (b, 0, 0, 0)))
        out_shape.append(jax.ShapeDtypeStruct((n_seq, 2, SSM_INNER, SSM_STATE), F32))
    return pl.pallas_call(
        functools.partial(_ssd_kernel, nc=nc, has_h0=h0 is not None, emit_state=emit_state),
        grid=(n_seq, nc),
        in_specs=in_specs,
        out_specs=out_specs,
        out_shape=out_shape,
        scratch_shapes=[pltpu.VMEM((2, SSM_INNER, SSM_STATE), F32), pltpu.VMEM((seq_len, SSM_INNER), F32)],
        compiler_params=_cparams(("arbitrary", "arbitrary")),
        name=f"ssd_{seq_len}",
    )(*args)


def kernel(x_prompt, x_sample, cache_k, cache_v, state_ssm, c, c_ctx, w_mod, b_mod, norm_g, ffn_in, ffn_out,
           gmlp_in, gmlp_ln_g, gmlp_ln_b, gmlp_ws, gmlp_bs, gmlp_out, attn_qkv, attn_sink, attn_out,
           ssm_in, ssm_conv_w, ssm_conv_b, ssm_dt_bias, ssm_a_log, ssm_d, ssm_norm, ssm_out):
    cond = jnp.concatenate([c_ctx[None, :], c, jnp.zeros((MOD_ROWS - 1 - N_LAT_SEQ, D), F32)], axis=0)
    mods = _modulation(cond, w_mod, b_mod).reshape(DEPTH * MOD_ROWS * N_MOD, 1, D)
    gains = norm_g.reshape(DEPTH * 6, 1, D)

    x = (x_prompt.reshape(CTX_TOK, D), x_sample.reshape(LAT_TOK, D))
    new_k = new_v = new_s = None
    for layer in range(DEPTH):
        x = _ffn(x, mods, gains, ffn_in, ffn_out, layer, 0, split_in=layer == 0)
        kind, j = layer % 3, layer // 3
        if kind == 0:
            x = _gmlp(x, mods, gains, gmlp_in, gmlp_ln_g[:, None, :], gmlp_ln_b[:, None, :], gmlp_ws,
                      gmlp_bs[..., None], gmlp_out, j, layer)
        elif kind == 1:
            qkv, kc, vc = _qkv_proj(x, mods, gains, attn_qkv, j, layer)
            new_k = kc.reshape(N_CTX_SEQ, N_KV_HEADS, HEAD_DIM, CTX_LEN).transpose(0, 3, 1, 2)
            new_v = vc.reshape(N_CTX_SEQ, N_KV_HEADS, HEAD_DIM, CTX_LEN).transpose(0, 3, 1, 2)
            o_ctx, o_lat = _attention(qkv, attn_sink[j],
                                      cache_k[:, j].reshape(N_LAT_SEQ, CTX_LEN, KV_DIM),
                                      cache_v[:, j].reshape(N_LAT_SEQ, CTX_LEN, KV_DIM))
            x = _out_proj(o_ctx, o_lat, attn_out, j, x, mods, gains, layer)
        else:
            pad = LANES - 2 * SSM_HEADS
            b_dt = jnp.pad(ssm_dt_bias[j].reshape(1, 2 * SSM_HEADS), ((0, 0), (0, pad)))
            z, xbc, dt = _ssm_in(x, mods, gains, jnp.transpose(ssm_in, (0, 2, 1)), j,
                                 ssm_conv_w[j], ssm_conv_b[j][None, :], b_dt, layer)
            a_pad = jnp.pad(-jnp.exp(ssm_a_log[j]), ((0, 0), (0, LANES - SSM_HEADS)))
            dsk = jnp.repeat(ssm_d[j], SSM_P)[None, :]
            ng = ssm_norm[j][None, :]
            r1, r2 = _spread_consts()
            w_o = ssm_out[j].astype(BF16)
            x_ctx, st = _ssd(z, xbc, dt, a_pad, dsk, ng, r1, r2, w_o, x, mods, gains, layer, None,
                             seq0=0, n_seq=N_CTX_SEQ, seq_len=CTX_LEN, emit_state=True)
            (x_lat,) = _ssd(z, xbc, dt, a_pad, dsk, ng, r1, r2, w_o, x, mods, gains, layer,
                            state_ssm[:, j].reshape(N_LAT_SEQ, 2, SSM_INNER, SSM_STATE),
                            seq0=CTX_TOK, n_seq=N_LAT_SEQ, seq_len=LAT_LEN, emit_state=False)
            new_s = st.reshape(N_CTX_SEQ, 2, SSM_HEADS, SSM_P, SSM_STATE)
            x = (x_ctx, x_lat)
        x = _ffn(x, mods, gains, ffn_in, ffn_out, layer, 1, split_in=isinstance(x, tuple),
                 split_out=layer == DEPTH - 1)

    y_prompt = x[0].reshape(N_CTX_SEQ, CTX_LEN, D)
    y_sample = x[1].reshape(N_LAT_SEQ, LAT_LEN, D)
    return (y_prompt, y_sample, new_k[:, None], new_v[:, None], new_s[:, None])
```

```python
import functools
import math

import jax
import jax.numpy as jnp
from jax import lax
from jax.experimental import pallas as pl
from jax.experimental.pallas import tpu as pltpu

F32 = jnp.float32
BF16 = jnp.bfloat16

D = 1024
N_CTX_SEQ, CTX_LEN = 16, 256
N_LAT_SEQ, LAT_LEN = 4, 1024
CTX_TOK = N_CTX_SEQ * CTX_LEN
LAT_TOK = N_LAT_SEQ * LAT_LEN
TOK = CTX_TOK + LAT_TOK
DEPTH = 4
N_MOD = 9
MOD_ROWS = 8
D_FF = 2816
EPS = 1e-6
GRID_W = 64
GMLP_HALF = 3 * D
GMLP_GROUPS = 8
GMLP_GD = GMLP_HALF // GMLP_GROUPS
CHUNK = 128
HEAD_DIM = 64
N_Q_HEADS = 16
N_KV_HEADS = 4
Q_PER_KV = 4
Q_DIM = N_Q_HEADS * HEAD_DIM
KV_DIM = N_KV_HEADS * HEAD_DIM
QKV_DIM = Q_DIM + 2 * KV_DIM
ATTN_SCALE = HEAD_DIM ** -0.5
ROPE_BASE = 10000.0
ROT_PAIRS = HEAD_DIM // 4
NEG_INF = -1e30
SSM_INNER = 2 * D
SSM_HEADS = 32
SSM_P = 64
SSM_GROUPS = 4
SSM_HG = SSM_HEADS // SSM_GROUPS
SSM_STATE = 128
SSM_GN = SSM_GROUPS * SSM_STATE
SSM_CONV_DIM = SSM_INNER + 2 * SSM_GN
SSM_MAIN = SSM_INNER + SSM_CONV_DIM
LANES = 128
LOG2E = 1.0 / math.log(2.0)

VMEM_LIMIT = 56 * 1024 * 1024
VMEM_LIMIT_BIG = 60 * 1024 * 1024


def _cparams(sem, limit=VMEM_LIMIT):
    return pltpu.CompilerParams(dimension_semantics=sem, vmem_limit_bytes=limit)


def _sigmoid(x):
    return 1.0 / (1.0 + jnp.exp(-x))


def _silu(x):
    return x * _sigmoid(x)


def _rms(x, g):
    return x * lax.rsqrt(jnp.mean(x * x, axis=-1, keepdims=True) + EPS) * g


def _adaln(x, g, shift, scale):
    return _rms(x, g) * (1.0 + scale) + shift


def _mod_row(i, tm):
    t0 = i * tm
    return jnp.where(t0 < CTX_TOK, 0, 1 + (t0 - CTX_TOK) // LAT_LEN)


def _mod_spec(layer, k, tm, grid_rank):
    base = layer * MOD_ROWS * N_MOD + k
    if grid_rank == 1:
        return pl.BlockSpec((None, 1, D), lambda i: (base + _mod_row(i, tm) * N_MOD, 0, 0))
    return pl.BlockSpec((None, 1, D), lambda i, j: (base + _mod_row(i, tm) * N_MOD, 0, 0))


def _gain_spec(layer, k, grid_rank):
    idx = layer * 6 + k
    if grid_rank == 1:
        return pl.BlockSpec((None, 1, D), lambda i: (idx, 0, 0))
    return pl.BlockSpec((None, 1, D), lambda i, j: (idx, 0, 0))


def _split_specs(tm, width, grid_rank):
    na = CTX_TOK // tm
    if grid_rank == 1:
        return (pl.BlockSpec((tm, width), lambda i: (jnp.minimum(i, na - 1), 0)),
                pl.BlockSpec((tm, width), lambda i: (jnp.maximum(i - na, 0), 0)))
    return (pl.BlockSpec((tm, width), lambda i, j: (jnp.minimum(i, na - 1), 0)),
            pl.BlockSpec((tm, width), lambda i, j: (jnp.maximum(i - na, 0), 0)))


def _mod_kernel(cond_ref, w_ref, b_ref, o_ref):
    s = _silu(cond_ref[...]).astype(BF16)
    o_ref[...] = jnp.dot(s, w_ref[...].astype(BF16), preferred_element_type=F32) + b_ref[...]


def _modulation(cond, w_mod, b_mod):
    tn = 2304
    n = N_MOD * D
    return pl.pallas_call(
        _mod_kernel,
        grid=(DEPTH, n // tn),
        in_specs=[
            pl.BlockSpec((MOD_ROWS, D), lambda l, j: (0, 0)),
            pl.BlockSpec((None, D, tn), lambda l, j: (l, 0, j)),
            pl.BlockSpec((None, 1, tn), lambda l, j: (l, 0, j)),
        ],
        out_specs=pl.BlockSpec((None, MOD_ROWS, tn), lambda l, j: (l, 0, j)),
        out_shape=jax.ShapeDtypeStruct((DEPTH, MOD_ROWS, n), F32),
        compiler_params=_cparams(("arbitrary", "arbitrary")),
        name="modulation",
    )(cond, w_mod, b_mod.reshape(DEPTH, 1, n))


FFN_TF = 256
FFN_SLABS = D_FF // FFN_TF
FFN_SPS = 4
FFN_STEP_SLABS = (3, 4, 4)
FFN_STEPS = len(FFN_STEP_SLABS)
assert sum(FFN_STEP_SLABS) == FFN_SLABS and max(FFN_STEP_SLABS) == FFN_SPS


def _ffn_kernel(*refs, tm, split_in, split_out):
    n_x = 2 if split_in else 1
    n_o = 2 if split_out else 1
    x_refs = refs[:n_x]
    sh_ref, sc_ref, gt_ref, g0_ref, g1_ref = refs[n_x:n_x + 5]
    w0 = n_x + 5
    wg = refs[w0:w0 + FFN_SPS]
    wu = refs[w0 + FFN_SPS:w0 + 2 * FFN_SPS]
    wo = refs[w0 + 2 * FFN_SPS:w0 + 3 * FFN_SPS]
    o_refs = refs[w0 + 3 * FFN_SPS:w0 + 3 * FFN_SPS + n_o]
    h_ref, acc_ref = refs[w0 + 3 * FFN_SPS + n_o:]
    i = pl.program_id(0)
    j = pl.program_id(1)
    is_ctx = i * tm < CTX_TOK
    last = FFN_STEPS - 1

    def load_x():
        if split_in:
            return jnp.where(is_ctx, x_refs[0][...], x_refs[1][...])
        return x_refs[0][...]

    def swiglu_slabs(h, n):
        acts = []
        for k in range(n):
            g = jnp.dot(h, wg[k][...].astype(BF16), preferred_element_type=F32)
            u = jnp.dot(h, wu[k][...].astype(BF16), preferred_element_type=F32)
            acts.append((_silu(g) * u).astype(BF16))
        w = jnp.concatenate([wo[k][...].astype(BF16) for k in range(n)], axis=0)
        return jnp.dot(jnp.concatenate(acts, axis=1), w, preferred_element_type=F32)

    @pl.when(j == 0)
    def _():
        h = _adaln(load_x(), g0_ref[...], sh_ref[...], sc_ref[...]).astype(BF16)
        h_ref[...] = h
        acc_ref[...] = swiglu_slabs(h, FFN_STEP_SLABS[0])

    for step in range(1, last):
        @pl.when(j == step)
        def _(step=step):
            acc_ref[...] += swiglu_slabs(h_ref[...], FFN_STEP_SLABS[step])

    @pl.when(j == last)
    def _():
        f = acc_ref[...] + swiglu_slabs(h_ref[...], FFN_STEP_SLABS[last])
        res = load_x() + (0.5 * gt_ref[...]) * _rms(f, g1_ref[...])
        if split_out:
            @pl.when(is_ctx)
            def _():
                o_refs[0][...] = res

            @pl.when(jnp.logical_not(is_ctx))
            def _():
                o_refs[1][...] = res
        else:
            o_refs[0][...] = res


def _ffn(x, mods, gains, ffn_in, ffn_out, layer, which, split_in=False, split_out=False):
    tm, tf = 1024, FFN_TF
    k0 = 0 if which == 0 else 6
    gi = 0 if which == 0 else 4
    if split_in:
        x_specs = list(_split_specs(tm, D, 2))
    else:
        x_specs = [pl.BlockSpec((tm, D), lambda i, j: (i, 0))]
    x_args = list(x) if split_in else [x]
    if split_out:
        out_specs = list(_split_specs(tm, D, 2))
        out_shape = [jax.ShapeDtypeStruct((CTX_TOK, D), F32), jax.ShapeDtypeStruct((LAT_TOK, D), F32)]
    else:
        out_specs = pl.BlockSpec((tm, D), lambda i, j: (i, 0))
        out_shape = jax.ShapeDtypeStruct((TOK, D), F32)

    starts = [sum(FFN_STEP_SLABS[:s]) for s in range(FFN_STEPS)]
    table = []
    for k in range(FFN_SPS):
        col = [starts[s] + k if k < FFN_STEP_SLABS[s] else None for s in range(FFN_STEPS)]
        for s in reversed(range(FFN_STEPS)):
            if col[s] is None:
                col[s] = col[s + 1] if s + 1 < FFN_STEPS else col[s - 1]
        table.append(col)

    def slab(j, k):
        idx = table[k][FFN_STEPS - 1]
        for s in reversed(range(FFN_STEPS - 1)):
            idx = jnp.where(j == s, table[k][s], idx)
        return idx

    def spec_in(k, col0):
        return pl.BlockSpec((None, None, D, tf), lambda i, j: (layer, which, 0, col0 + slab(j, k)))

    def spec_out(k):
        return pl.BlockSpec((None, None, tf, D), lambda i, j: (layer, which, slab(j, k), 0))

    w_specs = ([spec_in(k, 0) for k in range(FFN_SPS)] + [spec_in(k, FFN_SLABS) for k in range(FFN_SPS)]
               + [spec_out(k) for k in range(FFN_SPS)])
    return pl.pallas_call(
        functools.partial(_ffn_kernel, tm=tm, split_in=split_in, split_out=split_out),
        grid=(TOK // tm, FFN_STEPS),
        in_specs=x_specs + [
            _mod_spec(layer, k0 + 0, tm, 2),
            _mod_spec(layer, k0 + 1, tm, 2),
            _mod_spec(layer, k0 + 2, tm, 2),
            _gain_spec(layer, gi, 2),
            _gain_spec(layer, gi + 1, 2),
        ] + w_specs,
        out_specs=out_specs,
        out_shape=out_shape,
        scratch_shapes=[pltpu.VMEM((tm, D), BF16), pltpu.VMEM((tm, D), F32)],
        compiler_params=_cparams(("arbitrary", "arbitrary"), VMEM_LIMIT_BIG),
        name=f"ffn_l{layer}_{which}",
    )(*x_args, mods, mods, mods, gains, gains, *([ffn_in] * (2 * FFN_SPS)), *([ffn_out] * FFN_SPS))


def _out_kernel(aa_ref, ab_ref, w_ref, x_ref, gt_ref, g_ref, o_ref, *, tm):
    is_ctx = pl.program_id(0) * tm < CTX_TOK
    wb = w_ref[...].astype(BF16)
    for r in range(2):
        rows = slice(r * (tm // 2), (r + 1) * (tm // 2))
        a = jnp.where(is_ctx, aa_ref[rows, :], ab_ref[rows, :])
        m = jnp.dot(a, wb, preferred_element_type=F32)
        o_ref[rows, :] = x_ref[rows, :] + gt_ref[...] * _rms(m, g_ref[...])


def _out_proj(a_ctx, a_lat, w, widx, x, mods, gains, layer, tm=1024):
    k = a_ctx.shape[1]
    sa, sb = _split_specs(tm, k, 1)
    return pl.pallas_call(
        functools.partial(_out_kernel, tm=tm),
        grid=(TOK // tm,),
        in_specs=[
            sa, sb,
            pl.BlockSpec((None, k, D), lambda i: (widx, 0, 0)),
            pl.BlockSpec((tm, D), lambda i: (i, 0)),
            _mod_spec(layer, 5, tm, 1),
            _gain_spec(layer, 3, 1),
        ],
        out_specs=pl.BlockSpec((tm, D), lambda i: (i, 0)),
        out_shape=jax.ShapeDtypeStruct((TOK, D), F32),
        compiler_params=_cparams(("arbitrary",)),
        name=f"outproj_l{layer}",
    )(a_ctx, a_lat, w, x, mods, gains)


GMLP_GPS = 2
GMLP_SLABS = GMLP_GROUPS // GMLP_GPS
GMLP_COLS = GMLP_GPS * GMLP_GD


def _gelu_exact(x):
    return 0.5 * x * (1.0 + lax.erf(x * (1.0 / math.sqrt(2.0))))


def _fold_lanes(v):
    acc = v[:, :LANES]
    for cb in range(1, v.shape[1] // LANES):
        acc = acc + v[:, cb * LANES:(cb + 1) * LANES]
    return acc


GMLP_P1 = GMLP_SLABS // 2


def _gmlp_kernel(x_ref, sh_ref, sc_ref, g2_ref, wina_ref, winb_ref, lg_ref, lb_ref, ws_ref, bs_ref, wo_ref,
                 gt_ref, g3_ref, o_ref, h_ref, v_ref, c_ref, s1_ref, s2_ref, mu_ref, rstd_ref, *, tm):
    j = pl.program_id(1)
    last = GMLP_P1 + GMLP_SLABS - 1

    def slab(h, w_ref=wina_ref):
        return _gelu_exact(jnp.dot(h, w_ref[...].astype(BF16), preferred_element_type=F32))

    def park(y, idx):
        if idx == 0:
            c_ref[...] = jnp.sum(_fold_lanes(y), axis=-1, keepdims=True) * (1.0 / GMLP_COLS)
        dv = y - c_ref[...]
        if idx == 0:
            s1_ref[...] = _fold_lanes(dv)
            s2_ref[...] = _fold_lanes(dv * dv)
        else:
            s1_ref[...] += _fold_lanes(dv)
            s2_ref[...] += _fold_lanes(dv * dv)
        v_ref[idx] = y

    def mix(u, first):
        s = j - GMLP_P1
        vn = ((v_ref[s] - mu_ref[...]) * rstd_ref[...] * lg_ref[...] + lb_ref[...]).astype(BF16)
        rows = []
        for c in range(tm // CHUNK):
            cols = []
            for gg in range(GMLP_GPS):
                vg = vn[c * CHUNK:(c + 1) * CHUNK, gg * GMLP_GD:(gg + 1) * GMLP_GD]
                cols.append(jnp.dot(ws_ref[gg].astype(BF16), vg, preferred_element_type=F32) + bs_ref[gg])
            rows.append(jnp.concatenate(cols, axis=1))
        a = (u * jnp.concatenate(rows, axis=0)).astype(BF16)
        t = jnp.dot(a, wo_ref[...].astype(BF16), preferred_element_type=F32)
        return t if first else o_ref[...] + t

    @pl.when(j == 0)
    def _():
        h = _adaln(x_ref[...], g2_ref[...], sh_ref[...], sc_ref[...]).astype(BF16)
        h_ref[...] = h
        park(slab(h), 0)
        park(slab(h, winb_ref), 1)

    for step in range(1, GMLP_P1):
        @pl.when(j == step)
        def _(step=step):
            park(slab(h_ref[...]), 2 * step)
            park(slab(h_ref[...], winb_ref), 2 * step + 1)

    @pl.when(j == GMLP_P1)
    def _():
        d1 = jnp.sum(s1_ref[...], axis=-1, keepdims=True) * (1.0 / GMLP_HALF)
        d2 = jnp.sum(s2_ref[...], axis=-1, keepdims=True) * (1.0 / GMLP_HALF)
        mu_ref[...] = c_ref[...] + d1
        rstd_ref[...] = lax.rsqrt(d2 - d1 * d1 + EPS)
        o_ref[...] = mix(slab(h_ref[...]), True)

    @pl.when((j > GMLP_P1) & (j < last))
    def _():
        o_ref[...] = mix(slab(h_ref[...]), False)

    @pl.when(j == last)
    def _():
        m = mix(slab(h_ref[...]), False)
        o_ref[...] = x_ref[...] + gt_ref[...] * _rms(m, g3_ref[...])


def _gmlp(x, mods, gains, w_in, ln_g, ln_b, w_s, b_s, w_out, widx, layer, tm=1024):
    ns, p1 = GMLP_SLABS, GMLP_P1

    def u_slab(j):
        return jnp.maximum(j - p1, 0)

    def win_a(j):
        return jnp.where(j < p1, ns + 2 * j, j - p1)

    def win_b(j):
        return jnp.where(j < p1, ns + 2 * j + 1, ns + 1)

    return pl.pallas_call(
        functools.partial(_gmlp_kernel, tm=tm),
        grid=(TOK // tm, p1 + ns),
        in_specs=[
            pl.BlockSpec((tm, D), lambda i, j: (i, 0)),
            _mod_spec(layer, 3, tm, 2),
            _mod_spec(layer, 4, tm, 2),
            _gain_spec(layer, 2, 2),
            pl.BlockSpec((None, D, GMLP_COLS), lambda i, j: (widx, 0, win_a(j))),
            pl.BlockSpec((None, D, GMLP_COLS), lambda i, j: (widx, 0, win_b(j))),
            pl.BlockSpec((None, 1, GMLP_COLS), lambda i, j: (widx, 0, u_slab(j))),
            pl.BlockSpec((None, 1, GMLP_COLS), lambda i, j: (widx, 0, u_slab(j))),
            pl.BlockSpec((None, GMLP_GPS, CHUNK, CHUNK), lambda i, j: (widx, u_slab(j), 0, 0)),
            pl.BlockSpec((None, GMLP_GPS, CHUNK, 1), lambda i, j: (widx, u_slab(j), 0, 0)),
            pl.BlockSpec((None, GMLP_COLS, D), lambda i, j: (widx, u_slab(j), 0)),
            _mod_spec(layer, 5, tm, 2),
            _gain_spec(layer, 3, 2),
        ],
        out_specs=pl.BlockSpec((tm, D), lambda i, j: (i, 0)),
        out_shape=jax.ShapeDtypeStruct((TOK, D), F32),
        scratch_shapes=[pltpu.VMEM((tm, D), BF16), pltpu.VMEM((ns, tm, GMLP_COLS), F32),
                        pltpu.VMEM((tm, 1), F32), pltpu.VMEM((tm, LANES), F32), pltpu.VMEM((tm, LANES), F32),
                        pltpu.VMEM((tm, 1), F32), pltpu.VMEM((tm, 1), F32)],
        compiler_params=_cparams(("arbitrary", "arbitrary"), VMEM_LIMIT_BIG),
        name=f"gmlp_l{layer}",
    )(x, mods, mods, gains, w_in, w_in, ln_g, ln_b, w_s, b_s, w_out, mods, gains)


def _rope_tables():
    pos = jnp.arange(LAT_LEN)
    pos_r = (pos // GRID_W).astype(F32)
    pos_c = (pos % GRID_W).astype(F32)
    inv = ROPE_BASE ** (-jnp.arange(ROT_PAIRS, dtype=F32) / ROT_PAIRS)
    ang_r = pos_r[:, None] * inv
    ang_c = pos_c[:, None] * inv
    cos = jnp.concatenate([jnp.cos(ang_r)] * 2 + [jnp.cos(ang_c)] * 2, axis=1)
    sin = jnp.concatenate([-jnp.sin(ang_r), jnp.sin(ang_r), -jnp.sin(ang_c), jnp.sin(ang_c)], axis=1)
    reps = LANES // HEAD_DIM
    return jnp.tile(cos, (1, reps)), jnp.tile(sin, (1, reps))


def _qkv_kernel(x_ref, sh_ref, sc_ref, g_ref, w_ref, cos_ref, sin_ref, o_ref, kc_ref, vc_ref, *, tm):
    i = pl.program_id(0)
    hr = tm // 2

    def proj(rows, wb):
        h = _adaln(x_ref[rows, :], g_ref[...], sh_ref[...], sc_ref[...])
        return jnp.dot(h.astype(BF16), wb, preferred_element_type=F32)

    @pl.when(i * tm < CTX_TOK)
    def _():
        wb = w_ref[...].astype(BF16)
        for r in range(2):
            rows = slice(r * hr, (r + 1) * hr)
            y = proj(rows, wb)
            o_ref[rows, :] = y
            for sq in range(hr // CTX_LEN):
                blk = y[sq * CTX_LEN:(sq + 1) * CTX_LEN, :]
                b = r * (hr // CTX_LEN) + sq
                kc_ref[b] = blk[:, Q_DIM:Q_DIM + KV_DIM].T
                vc_ref[b] = blk[:, Q_DIM + KV_DIM:].T

    @pl.when(i * tm >= CTX_TOK)
    def _():
        wb = w_ref[...].astype(BF16)
        lane = lax.broadcasted_iota(jnp.int32, (hr, LANES), 1)
        first = (lane % (2 * ROT_PAIRS)) < ROT_PAIRS
        for r in range(2):
            rows = slice(r * hr, (r + 1) * hr)
            y = proj(rows, wb)
            cos = cos_ref[rows, :]
            sin = sin_ref[rows, :]
            for cb in range((Q_DIM + KV_DIM) // LANES):
                t = y[:, cb * LANES:(cb + 1) * LANES]
                partner = jnp.where(first, pltpu.roll(t, LANES - ROT_PAIRS, axis=1),
                                    pltpu.roll(t, ROT_PAIRS, axis=1))
                o_ref[rows, cb * LANES:(cb + 1) * LANES] = t * cos + partner * sin
            o_ref[rows, Q_DIM + KV_DIM:] = y[:, Q_DIM + KV_DIM:]


def _qkv_proj(x, mods, gains, w, widx, layer, tm=1024):
    cos, sin = _rope_tables()
    per_seq = LAT_LEN // tm
    tab = pl.BlockSpec((tm, LANES), lambda i: (jnp.maximum(i - CTX_TOK // tm, 0) % per_seq, 0))
    seq_per_tile = tm // CTX_LEN
    kv_spec = pl.BlockSpec((seq_per_tile, KV_DIM, CTX_LEN),
                           lambda i: (jnp.minimum(i, CTX_TOK // tm - 1), 0, 0))
    return pl.pallas_call(
        functools.partial(_qkv_kernel, tm=tm),
        grid=(TOK // tm,),
        in_specs=[
            pl.BlockSpec((tm, D), lambda i: (i, 0)),
            _mod_spec(layer, 3, tm, 1),
            _mod_spec(layer, 4, tm, 1),
            _gain_spec(layer, 2, 1),
            pl.BlockSpec((None, D, QKV_DIM), lambda i: (widx, 0, 0)),
            tab, tab,
        ],
        out_specs=[pl.BlockSpec((tm, QKV_DIM), lambda i: (i, 0)), kv_spec, kv_spec],
        out_shape=[jax.ShapeDtypeStruct((TOK, QKV_DIM), F32),
                   jax.ShapeDtypeStruct((N_CTX_SEQ, KV_DIM, CTX_LEN), F32),
                   jax.ShapeDtypeStruct((N_CTX_SEQ, KV_DIM, CTX_LEN), F32)],
        compiler_params=_cparams(("arbitrary",)),
        name=f"qkv_l{layer}",
    )(x, mods, mods, gains, w, cos, sin)


def _attend(q4, keys, vals, sink_rep, masks, merge_sums, transposed=()):
    logits = []
    for idx, (k, mk) in enumerate(zip(keys, masks)):
        if idx in transposed:
            s = jnp.dot(q4, k, preferred_element_type=F32) * ATTN_SCALE
        else:
            s = lax.dot_general(q4, k, (((1,), (1,)), ((), ())), preferred_element_type=F32) * ATTN_SCALE
        if mk is not None:
            s = jnp.where(mk, s, NEG_INF)
        logits.append([s[:, cb * LANES:(cb + 1) * LANES] for cb in range(s.shape[1] // LANES)])
    mx = None
    for blocks in logits:
        for blk in blocks:
            mx = blk if mx is None else jnp.maximum(mx, blk)
    m = jnp.maximum(sink_rep, jnp.max(mx, axis=-1, keepdims=True))
    acc = rs = None
    for idx, (blocks, v) in enumerate(zip(logits, vals)):
        p = jnp.concatenate([jnp.exp(blk - m) for blk in blocks], axis=1).astype(BF16)
        v_t = idx in transposed
        if merge_sums:
            v = jnp.concatenate([v, jnp.ones_like(v)], axis=0 if v_t else 1)
        else:
            ps = jnp.dot(p, jnp.ones((p.shape[1], LANES), BF16), preferred_element_type=F32)
            rs = ps if rs is None else rs + ps
        pv = _dot_nt(p, v) if v_t else jnp.dot(p, v, preferred_element_type=F32)
        acc = pv if acc is None else acc + pv
    if merge_sums:
        denom = pltpu.roll(acc, HEAD_DIM, axis=1) + jnp.exp(sink_rep - m)
        return (acc / denom)[:, :HEAD_DIM]
    return acc / (rs + jnp.exp(sink_rep - m))[:, :HEAD_DIM]


def _attn_ctx_kernel(sink_ref, q_ref, k_ref, v_ref, o_ref):
    rows = CTX_LEN
    outs = [None] * N_Q_HEADS
    for hk in range(N_KV_HEADS):
        k = k_ref[:, hk * HEAD_DIM:(hk + 1) * HEAD_DIM].astype(BF16)
        v = v_ref[:, hk * HEAD_DIM:(hk + 1) * HEAD_DIM].astype(BF16)
        heads = [hk * Q_PER_KV + g for g in range(Q_PER_KV)]
        q4 = jnp.concatenate([q_ref[:, h * HEAD_DIM:(h + 1) * HEAD_DIM] for h in heads], axis=0).astype(BF16)
        sink_rep = jnp.concatenate([jnp.full((rows, LANES), sink_ref[h], F32) for h in heads], axis=0)
        o4 = _attend(q4, [k], [v], sink_rep, [None], merge_sums=False)
        for g, h in enumerate(heads):
            outs[h] = o4[g * rows:(g + 1) * rows]
    o_ref[...] = jnp.concatenate(outs, axis=1).astype(o_ref.dtype)


def _attn_lat_kernel(sink_ref, q_ref, kp_ref, kc_ref, kn_ref, vp_ref, vc_ref, vn_ref, ck_ref, cv_ref,
                     o_ref, *, n_blk):
    qi = pl.program_id(1)
    rows = CHUNK
    r4 = Q_PER_KV * rows
    a = lax.broadcasted_iota(jnp.int32, (r4, CHUNK), 0) % rows
    s = lax.broadcasted_iota(jnp.int32, (r4, CHUNK), 1)
    mask_prev = (s >= a) & (qi > 0)
    mask_next = (s <= a) & (qi < n_blk - 1)
    outs = [None] * N_Q_HEADS
    for hk in range(N_KV_HEADS):
        sl = slice(hk * HEAD_DIM, (hk + 1) * HEAD_DIM)
        keys = [kp_ref[:, sl].astype(BF16), kc_ref[:, sl].astype(BF16), kn_ref[:, sl].astype(BF16),
                ck_ref[sl, :].astype(BF16)]
        vals = [vp_ref[:, sl].astype(BF16), vc_ref[:, sl].astype(BF16), vn_ref[:, sl].astype(BF16),
                cv_ref[sl, :].astype(BF16)]
        heads = [hk * Q_PER_KV + g for g in range(Q_PER_KV)]
        q4 = jnp.concatenate([q_ref[:, h * HEAD_DIM:(h + 1) * HEAD_DIM] for h in heads], axis=0).astype(BF16)
        sink_rep = jnp.concatenate([jnp.full((rows, LANES), sink_ref[h], F32) for h in heads], axis=0)
        o4 = _attend(q4, keys, vals, sink_rep, [mask_prev, None, mask_next, None], merge_sums=True,
                     transposed=(3,))
        for g, h in enumerate(heads):
            outs[h] = o4[g * rows:(g + 1) * rows]
    o_ref[...] = jnp.concatenate(outs, axis=1).astype(o_ref.dtype)


def _attention(qkv, sinks, ctx_k, ctx_v):
    smem = pl.BlockSpec(memory_space=pltpu.SMEM)
    kcol, vcol = Q_DIM // KV_DIM, Q_DIM // KV_DIM + 1
    ctx_blk = CTX_LEN
    o_ctx = pl.pallas_call(
        _attn_ctx_kernel,
        grid=(N_CTX_SEQ,),
        in_specs=[
            smem,
            pl.BlockSpec((ctx_blk, Q_DIM), lambda b: (b, 0)),
            pl.BlockSpec((ctx_blk, KV_DIM), lambda b: (b, kcol)),
            pl.BlockSpec((ctx_blk, KV_DIM), lambda b: (b, vcol)),
        ],
        out_specs=pl.BlockSpec((ctx_blk, Q_DIM), lambda b: (b, 0)),
        out_shape=jax.ShapeDtypeStruct((CTX_TOK, Q_DIM), BF16),
        compiler_params=_cparams(("arbitrary",)),
        name="attn_ctx",
    )(sinks, qkv, qkv, qkv)

    n_blk = LAT_LEN // CHUNK
    base = CTX_TOK // CHUNK

    def rb(b, qi, off):
        return base + b * n_blk + jnp.clip(qi + off, 0, n_blk - 1)

    def kv_spec(col, off):
        return pl.BlockSpec((CHUNK, KV_DIM), lambda b, qi: (rb(b, qi, off), col))

    o_lat = pl.pallas_call(
        functools.partial(_attn_lat_kernel, n_blk=n_blk),
        grid=(N_LAT_SEQ, n_blk),
        in_specs=[
            smem,
            pl.BlockSpec((CHUNK, Q_DIM), lambda b, qi: (rb(b, qi, 0), 0)),
            kv_spec(kcol, -1), kv_spec(kcol, 0), kv_spec(kcol, 1),
            kv_spec(vcol, -1), kv_spec(vcol, 0), kv_spec(vcol, 1),
            pl.BlockSpec((None, CTX_LEN, KV_DIM), lambda b, qi: (b, 0, 0)),
            pl.BlockSpec((None, CTX_LEN, KV_DIM), lambda b, qi: (b, 0, 0)),
        ],
        out_specs=pl.BlockSpec((CHUNK, Q_DIM), lambda b, qi: (b * n_blk + qi, 0)),
        out_shape=jax.ShapeDtypeStruct((LAT_TOK, Q_DIM), BF16),
        compiler_params=_cparams(("arbitrary", "arbitrary")),
        name="attn_lat",
    )(sinks, qkv, qkv, qkv, qkv, qkv, qkv, qkv, ctx_k, ctx_v)
    return o_ctx, o_lat


def _softplus(x):
    return jnp.maximum(x, 0.0) + jnp.log1p(jnp.exp(-jnp.abs(x)))


def _dot_nt(a, b):
    return lax.dot_general(a, b, (((1,), (1,)), ((), ())), preferred_element_type=F32)


SSM_TN = 1024
SSM_Z_STEPS = SSM_INNER // SSM_TN
SSM_C_STEPS = SSM_CONV_DIM // SSM_TN


def _ssm_in_kernel(x_ref, sh_ref, sc_ref, g_ref, w_ref, cw_ref, cb_ref, wdt_ref, bdt_ref,
                   z_ref, xbc_ref, dt_ref, h_ref, *, tm):
    i = pl.program_id(0)
    j = pl.program_id(1)

    @pl.when(j == 0)
    def _():
        h = _adaln(x_ref[...], g_ref[...], sh_ref[...], sc_ref[...]).astype(BF16)
        h_ref[...] = h
        y = _dot_nt(h, wdt_ref[...].astype(BF16))
        y = jnp.concatenate([y, jnp.zeros((tm, LANES - y.shape[1]), F32)], axis=1)
        dt_ref[...] = _softplus(y + bdt_ref[...])
        z_ref[...] = _dot_nt(h, w_ref[...].astype(BF16)).astype(BF16)

    @pl.when((j > 0) & (j < SSM_Z_STEPS))
    def _():
        z_ref[...] = _dot_nt(h_ref[...], w_ref[...].astype(BF16)).astype(BF16)

    @pl.when(j >= SSM_Z_STEPS)
    def _():
        y = _dot_nt(h_ref[...], w_ref[...].astype(BF16))
        seq_len = jnp.where(i * tm < CTX_TOK, CTX_LEN, LAT_LEN)
        pos = lax.broadcasted_iota(jnp.int32, (tm, 1), 0) & (seq_len - 1)
        up = jnp.where(pos == 0, 0.0, pltpu.roll(y, 1, axis=0))
        dn = jnp.where(pos == seq_len - 1, 0.0, pltpu.roll(y, tm - 1, axis=0))
        c = up * cw_ref[0:1, :] + y * cw_ref[1:2, :] + dn * cw_ref[2:3, :] + cb_ref[...]
        xbc_ref[...] = _silu(c).astype(BF16)


def _ssm_in(x, mods, gains, w_t, widx, conv_w, conv_b, b_dt, layer, tm=1024):
    tn = SSM_TN
    n_main = SSM_Z_STEPS + SSM_C_STEPS
    n_dt = 2 * SSM_HEADS

    def conv_col(j):
        return jnp.clip(j - SSM_Z_STEPS, 0, SSM_C_STEPS - 1)

    return pl.pallas_call(
        functools.partial(_ssm_in_kernel, tm=tm),
        grid=(TOK // tm, n_main),
        in_specs=[
            pl.BlockSpec((tm, D), lambda i, j: (i, 0)),
            _mod_spec(layer, 3, tm, 2),
            _mod_spec(layer, 4, tm, 2),
            _gain_spec(layer, 2, 2),
            pl.BlockSpec((None, tn, D), lambda i, j: (widx, j, 0)),
            pl.BlockSpec((3, tn), lambda i, j: (0, conv_col(j))),
            pl.BlockSpec((1, tn), lambda i, j: (0, conv_col(j))),
            pl.BlockSpec((None, n_dt, D), lambda i, j: (widx, SSM_MAIN // n_dt, 0)),
            pl.BlockSpec((1, LANES), lambda i, j: (0, 0)),
        ],
        out_specs=[
            pl.BlockSpec((tm, tn), lambda i, j: (i, jnp.minimum(j, SSM_Z_STEPS - 1))),
            pl.BlockSpec((tm, tn), lambda i, j: (i, conv_col(j))),
            pl.BlockSpec((tm, LANES), lambda i, j: (i, 0)),
        ],
        out_shape=[jax.ShapeDtypeStruct((TOK, SSM_INNER), BF16), jax.ShapeDtypeStruct((TOK, SSM_CONV_DIM), BF16),
                   jax.ShapeDtypeStruct((TOK, LANES), F32)],
        scratch_shapes=[pltpu.VMEM((tm, D), BF16)],
        compiler_params=_cparams(("arbitrary", "arbitrary")),
        name=f"ssm_in_l{layer}",
    )(x, mods, mods, gains, w_t, conv_w, conv_b, w_t, b_dt)


def _split3(q):
    hi = q.astype(BF16)
    r1 = q - hi.astype(F32)
    mid = r1.astype(BF16)
    lo = (r1 - mid.astype(F32)).astype(BF16)
    return hi, mid, lo


def _spread(q, r_ref):
    hi, mid, lo = _split3(q[:, :SSM_HEADS])
    return jnp.dot(jnp.concatenate([hi, mid, lo], axis=1), r_ref[...], preferred_element_type=F32)


def _spread_consts():
    r = jnp.arange(3 * SSM_HEADS) % SSM_HEADS
    r1 = (jnp.arange(SSM_HEADS * SSM_P)[None, :] // SSM_P == r[:, None]).astype(BF16)
    r2 = (jnp.arange(SSM_HEADS * CHUNK)[None, :] // CHUNK == r[:, None]).astype(BF16)
    return r1, r2


def _ssd_chain(xbc, dt_all, a_ref, r1_ref, r2_ref, hs_ref, bwd):
    L = CHUNK
    d = 1 if bwd else 0
    xs = xbc[:, :SSM_INNER].astype(F32)
    bcb = xbc[:, SSM_INNER:]

    dt = pltpu.roll(dt_all, LANES - SSM_HEADS, axis=1) if bwd else dt_all
    a_row = a_ref[d:d + 1, :]
    ii = lax.broadcasted_iota(jnp.int32, (L, L), 0)
    jj = lax.broadcasted_iota(jnp.int32, (L, L), 1)
    tri = (jj >= ii) if bwd else (ii >= jj)
    tri_b = jnp.where(tri, 1.0, 0.0).astype(BF16)
    neg_mask = jnp.where(tri, 0.0, -jnp.inf)
    hi_p, mid_p, lo_p = _split3(dt * a_row)
    acum = (jnp.dot(tri_b, hi_p, preferred_element_type=F32)
            + jnp.dot(tri_b, mid_p, preferred_element_type=F32)
            + jnp.dot(tri_b, lo_p, preferred_element_type=F32))
    end = 0 if bwd else L - 1
    total = acum[end:end + 1, :]
    acum2 = acum * LOG2E
    acum_t = acum2.T
    cdec_t = jnp.exp2(acum_t[:, end:end + 1])

    e_acum = _spread(acum2, r2_ref)
    e_dt = _spread(dt, r1_ref)
    e_end = _spread(dt * jnp.exp(total - acum), r1_ref)
    e_in = _spread(jnp.exp(acum), r1_ref)
    xdt = (xs * e_dt).astype(BF16)
    xw = (xs * e_end).astype(BF16)
    left = lax.broadcasted_iota(jnp.int32, (L, LANES), 1) < SSM_P

    y_parts = []
    for g in range(SSM_GROUPS):
        bm = bcb[:, g * SSM_STATE:(g + 1) * SSM_STATE]
        cm = bcb[:, SSM_GN + g * SSM_STATE:SSM_GN + (g + 1) * SSM_STATE]
        cb = lax.dot_general(cm, bm, (((1,), (1,)), ((), ())), preferred_element_type=F32)
        gsl = slice(g * SSM_HG * SSM_P, (g + 1) * SSM_HG * SSM_P)
        hprev = hs_ref[d, gsl, :]
        y_off = lax.dot_general(cm, hprev.astype(BF16), (((1,), (1,)), ((), ())),
                                preferred_element_type=F32)
        y_pairs = []
        for pr in range(SSM_HG // 2):
            wm = []
            for h in (g * SSM_HG + 2 * pr, g * SSM_HG + 2 * pr + 1):
                seg = e_acum[:, h * L:(h + 1) * L] - acum_t[h:h + 1, :] + neg_mask
                wm.append((cb * jnp.exp2(seg)).astype(BF16))
            pair = xdt[:, (g * SSM_HG + 2 * pr) * SSM_P:(g * SSM_HG + 2 * pr + 2) * SSM_P]
            rhs = jnp.concatenate([jnp.where(left, pair, jnp.zeros_like(pair)),
                                   jnp.where(left, jnp.zeros_like(pair), pair)], axis=0)
            y_pairs.append(jnp.dot(jnp.concatenate(wm, axis=1), rhs, preferred_element_type=F32))
        y_parts.append(jnp.concatenate(y_pairs, axis=1) + y_off * e_in[:, gsl])
        st = lax.dot_general(xw[:, gsl], bm, (((0,), (0,)), ((), ())), preferred_element_type=F32)
        dec = jnp.concatenate([jnp.broadcast_to(cdec_t[g * SSM_HG + k:g * SSM_HG + k + 1, :], (SSM_P, 1))
                               for k in range(SSM_HG)], axis=0)
        hs_ref[d, gsl, :] = dec * hprev + st
    return jnp.concatenate(y_parts, axis=1), xs


def _ssd_kernel(*refs, nc, has_h0, emit_state):
    (xa_ref, xb_ref, dta_ref, dtb_ref, za_ref, zb_ref, a_ref, dsk_ref, ng_ref, r1_ref, r2_ref,
     wo_ref, xres_ref, gt_ref, g3_ref) = refs[:15]
    pos = 15
    h0_ref = None
    if has_h0:
        h0_ref = refs[pos]
        pos += 1
    y_ref = refs[pos]
    pos += 1
    st_ref = None
    if emit_state:
        st_ref = refs[pos]
        pos += 1
    hs_ref, yacc_ref = refs[pos:pos + 2]

    s = pl.program_id(1)
    L = CHUNK
    half = nc // 2
    off_a = pl.multiple_of(s * L, L)
    off_b = pl.multiple_of((nc - 1 - s) * L, L)

    @pl.when(s == 0)
    def _():
        hs_ref[...] = h0_ref[...] if has_h0 else jnp.zeros_like(hs_ref)

    ya, xs_a = _ssd_chain(xa_ref[...], dta_ref[...], a_ref, r1_ref, r2_ref, hs_ref, bwd=False)
    ya = ya + dsk_ref[...] * xs_a
    yb, _ = _ssd_chain(xb_ref[...], dtb_ref[...], a_ref, r1_ref, r2_ref, hs_ref, bwd=True)

    @pl.when(s < half)
    def _():
        yacc_ref[pl.ds(off_a, L), :] = ya
        yacc_ref[pl.ds(off_b, L), :] = yb

    @pl.when(s >= half)
    def _():
        for off, y_new, z_ref in ((off_a, ya, za_ref), (off_b, yb, zb_ref)):
            yt = (yacc_ref[pl.ds(off, L), :] + y_new) * _silu(z_ref[...].astype(F32))
            yn = _rms(yt, ng_ref[...]).astype(BF16)
            m = jnp.dot(yn, wo_ref[...], preferred_element_type=F32)
            y_ref[pl.ds(off, L), :] = xres_ref[pl.ds(off, L), :] + gt_ref[...] * _rms(m, g3_ref[...])

    if emit_state:
        @pl.when(s == nc - 1)
        def _():
            st_ref[...] = hs_ref[...]


def _ssd(z, xbc, dt, a_pad, dsk, norm_g, r1, r2, w_out, x, mods, gains, layer, h0, *,
         seq0, n_seq, seq_len, emit_state):
    nc = seq_len // CHUNK
    half = nc // 2
    chunk0 = seq0 // CHUNK
    seq_blk0 = seq0 // seq_len
    mod_base = layer * MOD_ROWS * N_MOD + 5
    mod_row0 = 0 if seq0 < CTX_TOK else 1
    mod_step = 0 if seq0 < CTX_TOK else N_MOD

    def fwd_chunk(b, s):
        return chunk0 + b * nc + s

    def bwd_chunk(b, s):
        return chunk0 + b * nc + nc - 1 - s

    def late(s):
        return jnp.maximum(s, half)

    const = lambda b, s: (0, 0)
    in_specs = [
        pl.BlockSpec((CHUNK, SSM_CONV_DIM), lambda b, s: (fwd_chunk(b, s), 0)),
        pl.BlockSpec((CHUNK, SSM_CONV_DIM), lambda b, s: (bwd_chunk(b, s), 0)),
        pl.BlockSpec((CHUNK, LANES), lambda b, s: (fwd_chunk(b, s), 0)),
        pl.BlockSpec((CHUNK, LANES), lambda b, s: (bwd_chunk(b, s), 0)),
        pl.BlockSpec((CHUNK, SSM_INNER), lambda b, s: (fwd_chunk(b, late(s)), 0)),
        pl.BlockSpec((CHUNK, SSM_INNER), lambda b, s: (bwd_chunk(b, late(s)), 0)),
        pl.BlockSpec((2, LANES), const),
        pl.BlockSpec((1, SSM_INNER), const),
        pl.BlockSpec((1, SSM_INNER), const),
        pl.BlockSpec(r1.shape, const),
        pl.BlockSpec(r2.shape, const),
        pl.BlockSpec((SSM_INNER, D), const, pipeline_mode=pl.Buffered(1)),
        pl.BlockSpec((seq_len, D), lambda b, s: (seq_blk0 + b, 0)),
        pl.BlockSpec((None, 1, D), lambda b, s: (mod_base + mod_row0 * N_MOD + b * mod_step, 0, 0)),
        pl.BlockSpec((None, 1, D), lambda b, s: (layer * 6 + 3, 0, 0)),
    ]
    args = [xbc, xbc, dt, dt, z, z, a_pad, dsk, norm_g, r1, r2, w_out, x, mods, gains]
    if h0 is not None:
        in_specs.append(pl.BlockSpec((None, 2, SSM_INNER, SSM_STATE), lambda b, s: (b, 0, 0, 0)))
        args.append(h0)
    out_specs = [pl.BlockSpec((seq_len, D), lambda b, s: (b, 0))]
    out_shape = [jax.ShapeDtypeStruct((n_seq * seq_len, D), F32)]
    if emit_state:
        out_specs.append(pl.BlockSpec((None, 2, SSM_INNER, SSM_STATE), lambda b, s: (b, 0, 0, 0)))
        out_shape.append(jax.ShapeDtypeStruct((n_seq, 2, SSM_INNER, SSM_STATE), F32))
    return pl.pallas_call(
        functools.partial(_ssd_kernel, nc=nc, has_h0=h0 is not None, emit_state=emit_state),
        grid=(n_seq, nc),
        in_specs=in_specs,
        out_specs=out_specs,
        out_shape=out_shape,
        scratch_shapes=[pltpu.VMEM((2, SSM_INNER, SSM_STATE), F32), pltpu.VMEM((seq_len, SSM_INNER), F32)],
        compiler_params=_cparams(("arbitrary", "arbitrary")),
        name=f"ssd_{seq_len}",
    )(*args)


def kernel(x_prompt, x_sample, cache_k, cache_v, state_ssm, c, c_ctx, w_mod, b_mod, norm_g, ffn_in, ffn_out,
           gmlp_in, gmlp_ln_g, gmlp_ln_b, gmlp_ws, gmlp_bs, gmlp_out, attn_qkv, attn_sink, attn_out,
           ssm_in, ssm_conv_w, ssm_conv_b, ssm_dt_bias, ssm_a_log, ssm_d, ssm_norm, ssm_out):
    cond = jnp.concatenate([c_ctx[None, :], c, jnp.zeros((MOD_ROWS - 1 - N_LAT_SEQ, D), F32)], axis=0)
    mods = _modulation(cond, w_mod, b_mod).reshape(DEPTH * MOD_ROWS * N_MOD, 1, D)
    gains = norm_g.reshape(DEPTH * 6, 1, D)

    x = (x_prompt.reshape(CTX_TOK, D), x_sample.reshape(LAT_TOK, D))
    new_k = new_v = new_s = None
    for layer in range(DEPTH):
        x = _ffn(x, mods, gains, ffn_in, ffn_out, layer, 0, split_in=layer == 0)
        kind, j = layer % 3, layer // 3
        if kind == 0:
            x = _gmlp(x, mods, gains, gmlp_in, gmlp_ln_g[:, None, :], gmlp_ln_b[:, None, :], gmlp_ws,
                      gmlp_bs[..., None], gmlp_out, j, layer)
        elif kind == 1:
            qkv, kc, vc = _qkv_proj(x, mods, gains, attn_qkv, j, layer)
            new_k = kc.reshape(N_CTX_SEQ, N_KV_HEADS, HEAD_DIM, CTX_LEN).transpose(0, 3, 1, 2)
            new_v = vc.reshape(N_CTX_SEQ, N_KV_HEADS, HEAD_DIM, CTX_LEN).transpose(0, 3, 1, 2)
            o_ctx, o_lat = _attention(qkv, attn_sink[j],
                                      cache_k[:, j].transpose(0, 2, 3, 1).reshape(N_LAT_SEQ, KV_DIM, CTX_LEN),
                                      cache_v[:, j].transpose(0, 2, 3, 1).reshape(N_LAT_SEQ, KV_DIM, CTX_LEN))
            x = _out_proj(o_ctx, o_lat, attn_out, j, x, mods, gains, layer)
        else:
            pad = LANES - 2 * SSM_HEADS
            b_dt = jnp.pad(ssm_dt_bias[j].reshape(1, 2 * SSM_HEADS), ((0, 0), (0, pad)))
            z, xbc, dt = _ssm_in(x, mods, gains, jnp.transpose(ssm_in, (0, 2, 1)), j,
                                 ssm_conv_w[j], ssm_conv_b[j][None, :], b_dt, layer)
            a_pad = jnp.pad(-jnp.exp(ssm_a_log[j]), ((0, 0), (0, LANES - SSM_HEADS)))
            dsk = jnp.repeat(ssm_d[j], SSM_P)[None, :]
            ng = ssm_norm[j][None, :]
            r1, r2 = _spread_consts()
            w_o = ssm_out[j].astype(BF16)
            x_ctx, st = _ssd(z, xbc, dt, a_pad, dsk, ng, r1, r2, w_o, x, mods, gains, layer, None,
                             seq0=0, n_seq=N_CTX_SEQ, seq_len=CTX_LEN, emit_state=True)
            (x_lat,) = _ssd(z, xbc, dt, a_pad, dsk, ng, r1, r2, w_o, x, mods, gains, layer,
                            state_ssm[:, j].reshape(N_LAT_SEQ, 2, SSM_INNER, SSM_STATE),
                            seq0=CTX_TOK, n_seq=N_LAT_SEQ, seq_len=LAT_LEN, emit_state=False)
            new_s = st.reshape(N_CTX_SEQ, 2, SSM_HEADS, SSM_P, SSM_STATE)
            x = (x_ctx, x_lat)
        x = _ffn(x, mods, gains, ffn_in, ffn_out, layer, 1, split_in=isinstance(x, tuple),
                 split_out=layer == DEPTH - 1)

    y_prompt = x[0].reshape(N_CTX_SEQ, CTX_LEN, D)
    y_sample = x[1].reshape(N_LAT_SEQ, LAT_LEN, D)
    return (y_prompt, y_sample, new_k[:, None], new_v[:, None], new_s[:, None])
```
